```python
import jax, jax.numpy as jnp
from jax import lax
import numpy as np

D_MODEL = 1024
BATCH = 4
SEQ = 4096
DEPTH = 1

N_MEM = 256
CONV_WIDTH = 512
CONV_KERNEL = 31
MLA_HEADS = 8
QK_NOPE = 64
QK_ROPE = 32
V_HEAD = 64
Q_LORA = 384
KV_LORA = 256
MLA_WIDTH = MLA_HEADS * V_HEAD
X_HEADS = 4
X_HEAD_DIM = 128
X_WIDTH = X_HEADS * X_HEAD_DIM
N_BRANCH = 3
ROPE_THETA = 10000.0
BLOCK_Q = 128
EPS = 1e-6
IN_SPLITS = (CONV_WIDTH, CONV_WIDTH, CONV_WIDTH, Q_LORA, KV_LORA, QK_ROPE, MLA_WIDTH, X_WIDTH, X_WIDTH, N_BRANCH * D_MODEL)
D_IN = sum(IN_SPLITS)

kernel_name = 'hybrid_conv_mla_xattn_gated_block'


def rms_norm(x, g):
    xf = x.astype(jnp.float32)
    y = xf * lax.rsqrt(jnp.mean(xf * xf, axis=-1, keepdims=True) + EPS)
    return (y * g.astype(jnp.float32)).astype(x.dtype)


def layer_norm(x, g, b):
    xf = x.astype(jnp.float32)
    mu = jnp.mean(xf, axis=-1, keepdims=True)
    var = jnp.mean(jnp.square(xf - mu), axis=-1, keepdims=True)
    y = (xf - mu) * lax.rsqrt(var + EPS)
    return (y * g.astype(jnp.float32) + b.astype(jnp.float32)).astype(x.dtype)


def apply_rope(x, cos, sin):
    xf = x.astype(jnp.float32)
    x1, x2 = jnp.split(xf, 2, axis=-1)
    out = jnp.concatenate([x1 * cos - x2 * sin, x1 * sin + x2 * cos], axis=-1)
    return out.astype(x.dtype)


def conv_branch(val, glu, gate, conv_w, conv_b, ln_g, ln_b, w_o):
    u = val * jax.nn.sigmoid(glu)
    u = lax.conv_general_dilated(
        u, conv_w[:, None, :], window_strides=(1,),
        padding=[(CONV_KERNEL - 1, 0)],
        dimension_numbers=('NWC', 'WIO', 'NWC'),
        feature_group_count=CONV_WIDTH) + conv_b
    u = jax.nn.silu(layer_norm(u, ln_g, ln_b))
    u = u * jax.nn.silu(gate)
    return u @ w_o


def mla_branch(q_down, kv_down, k_rope_raw, gate, cos, sin,
               q_norm_g, w_uq, kv_norm_g, w_ukv, w_o):
    B, S, _ = q_down.shape
    cq = rms_norm(q_down, q_norm_g)
    q = (cq @ w_uq).reshape(B, S, MLA_HEADS, QK_NOPE + QK_ROPE)
    q_nope = q[..., :QK_NOPE]
    q_rope = apply_rope(q[..., QK_NOPE:], cos[:, :, None, :], sin[:, :, None, :])
    ckv = rms_norm(kv_down, kv_norm_g)
    kv = (ckv @ w_ukv).reshape(B, S, MLA_HEADS, QK_NOPE + V_HEAD)
    k_nope = kv[..., :QK_NOPE]
    v = kv[..., QK_NOPE:]
    k_rope = apply_rope(k_rope_raw, cos, sin)
    scale = (QK_NOPE + QK_ROPE) ** -0.5
    n_blk = S // BLOCK_Q
    qn_b = (q_nope * scale).reshape(B, n_blk, BLOCK_Q, MLA_HEADS, QK_NOPE).transpose(1, 0, 2, 3, 4)
    qr_b = (q_rope * scale).reshape(B, n_blk, BLOCK_Q, MLA_HEADS, QK_ROPE).transpose(1, 0, 2, 3, 4)
    k_pos = jnp.arange(S)
    neg = jnp.finfo(jnp.float32).min

    def attend(args):
        qn, qr, blk = args
        s = (jnp.einsum('bqhd,bkhd->bhqk', qn, k_nope)
             + jnp.einsum('bqhr,bkr->bhqk', qr, k_rope)).astype(jnp.float32)
        q_pos = blk * BLOCK_Q + jnp.arange(BLOCK_Q)
        causal = k_pos[None, :] <= q_pos[:, None]
        p = jax.nn.softmax(jnp.where(causal, s, neg), axis=-1).astype(v.dtype)
        return jnp.einsum('bhqk,bkhd->bqhd', p, v)

    o = lax.map(attend, (qn_b, qr_b, jnp.arange(n_blk)))
    o = o.transpose(1, 0, 2, 3, 4).reshape(B, S, MLA_WIDTH)
    return (o * jax.nn.silu(gate)) @ w_o


def cross_branch(xq, gate, mem, mem_norm_g, w_mem_kv, w_o):
    B, S, _ = xq.shape
    M = mem.shape[1]
    q = xq.reshape(B, S, X_HEADS, X_HEAD_DIM)
    mkv = rms_norm(mem, mem_norm_g) @ w_mem_kv
    k = mkv[..., :X_WIDTH].reshape(B, M, X_HEADS, X_HEAD_DIM)
    v = mkv[..., X_WIDTH:].reshape(B, M, X_HEADS, X_HEAD_DIM)
    s = jnp.einsum('bshd,bmhd->bhsm', q, k).astype(jnp.float32) * (X_HEAD_DIM ** -0.5)
    p = jax.nn.softmax(s, axis=-1).astype(v.dtype)
    o = jnp.einsum('bhsm,bmhd->bshd', p, v).reshape(B, S, X_WIDTH)
    return (o * jax.nn.silu(gate)) @ w_o


def setup_inputs(seed: int = 0) -> dict:
    key = jax.random.key(seed)
    ks = jax.random.split(key, 24)
    f32 = jnp.float32

    def nrm(k, shape, fan_in):
        return jax.random.normal(k, shape, f32) * (fan_in ** -0.5)

    def gain(k, shape):
        return 1.0 + 0.02 * jax.random.normal(k, shape, f32)

    x = jax.random.normal(ks[0], (BATCH, SEQ, D_MODEL), f32)
    mem = jax.random.normal(ks[1], (BATCH, N_MEM, D_MODEL), f32)
    offset = jax.random.randint(ks[2], (BATCH, 1), 0, 1024, dtype=jnp.int32)
    positions = (offset + jnp.arange(SEQ, dtype=jnp.int32)[None, :]).astype(jnp.int32)
    return {
        'x': x,
        'mem': mem,
        'positions': positions,
        'norm_g': gain(ks[3], (DEPTH, D_MODEL)),
        'w_in': nrm(ks[4], (DEPTH, D_MODEL, D_IN), D_MODEL),
        'b_gate': 0.01 * jax.random.normal(ks[5], (DEPTH, N_BRANCH * D_MODEL), f32),
        'conv_w': nrm(ks[6], (DEPTH, CONV_KERNEL, CONV_WIDTH), CONV_KERNEL),
        'conv_b': 0.01 * jax.random.normal(ks[7], (DEPTH, CONV_WIDTH), f32),
        'conv_ln_g': gain(ks[8], (DEPTH, CONV_WIDTH)),
        'conv_ln_b': 0.01 * jax.random.normal(ks[9], (DEPTH, CONV_WIDTH), f32),
        'w_conv_o': nrm(ks[10], (DEPTH, CONV_WIDTH, D_MODEL), CONV_WIDTH),
        'q_norm_g': gain(ks[11], (DEPTH, Q_LORA)),
        'w_uq': nrm(ks[12], (DEPTH, Q_LORA, MLA_HEADS * (QK_NOPE + QK_ROPE)), Q_LORA),
        'kv_norm_g': gain(ks[13], (DEPTH, KV_LORA)),
        'w_ukv': nrm(ks[14], (DEPTH, KV_LORA, MLA_HEADS * (QK_NOPE + V_HEAD)), KV_LORA),
        'w_mla_o': nrm(ks[15], (DEPTH, MLA_WIDTH, D_MODEL), MLA_WIDTH),
        'mem_norm_g': gain(ks[16], (DEPTH, D_MODEL)),
        'w_mem_kv': nrm(ks[17], (DEPTH, D_MODEL, 2 * X_WIDTH), D_MODEL),
        'w_x_o': nrm(ks[18], (DEPTH, X_WIDTH, D_MODEL), X_WIDTH),
        'w_out': nrm(ks[19], (DEPTH, D_MODEL, D_MODEL), D_MODEL),
        'final_norm_g': gain(ks[20], (D_MODEL,)),
    }


def reference(x, mem, positions, norm_g, w_in, b_gate, conv_w, conv_b, conv_ln_g,
              conv_ln_b, w_conv_o, q_norm_g, w_uq, kv_norm_g, w_ukv, w_mla_o,
              mem_norm_g, w_mem_kv, w_x_o, w_out, final_norm_g):
    B, S, D = x.shape
    split_idx = np.cumsum(np.array(IN_SPLITS))[:-1].tolist()
    inv_freq = ROPE_THETA ** (-jnp.arange(0, QK_ROPE, 2, dtype=jnp.float32) / QK_ROPE)
    angles = positions.astype(jnp.float32)[..., None] * inv_freq
    cos, sin = jnp.cos(angles), jnp.sin(angles)
    for l in range(DEPTH):
        h = rms_norm(x, norm_g[l])
        z = h @ w_in[l]
        (c_val, c_glu, c_gate, q_down, kv_down, k_rope_raw, m_gate,
         x_q, x_gate, g_logits) = jnp.split(z, split_idx, axis=-1)
        y_conv = conv_branch(c_val, c_glu, c_gate, conv_w[l], conv_b[l],
                             conv_ln_g[l], conv_ln_b[l], w_conv_o[l])
        y_mla = mla_branch(q_down, kv_down, k_rope_raw, m_gate, cos, sin,
                           q_norm_g[l], w_uq[l], kv_norm_g[l], w_ukv[l], w_mla_o[l])
        y_x = cross_branch(x_q, x_gate, mem, mem_norm_g[l], w_mem_kv[l], w_x_o[l])
        g = jax.nn.sigmoid((g_logits + b_gate[l]).reshape(B, S, N_BRANCH, D))
        merged = g[:, :, 0] * y_conv + g[:, :, 1] * y_mla + g[:, :, 2] * y_x
        x = x + merged @ w_out[l]
    return rms_norm(x, final_norm_g)
```

```python
import functools

import jax
import jax.numpy as jnp
import numpy as np
from jax import lax
from jax.experimental import pallas as pl
from jax.experimental.pallas import tpu as pltpu

F32 = jnp.float32
BF16 = jnp.bfloat16

D_MODEL = 1024
N_MEM = 256
CONV_WIDTH = 512
CONV_KERNEL = 31
MLA_HEADS = 8
QK_NOPE = 64
QK_ROPE = 32
V_HEAD = 64
Q_LORA = 384
KV_LORA = 256
MLA_WIDTH = MLA_HEADS * V_HEAD
X_HEADS = 4
X_HEAD_DIM = 128
X_WIDTH = X_HEADS * X_HEAD_DIM
ROPE_THETA = 10000.0
EPS = 1e-6
IN_SPLITS = (CONV_WIDTH, CONV_WIDTH, CONV_WIDTH, Q_LORA, KV_LORA, QK_ROPE, MLA_WIDTH,
             X_WIDTH, X_WIDTH, 3 * D_MODEL)

LANES = 128
HEAD_PAD = LANES
ROPE_LO = QK_NOPE
ROPE_HALF = QK_ROPE // 2
ROPE_HI = QK_NOPE + QK_ROPE
MLA_SCALE = (QK_NOPE + QK_ROPE) ** -0.5
X_SCALE = X_HEAD_DIM ** -0.5
NEG = float(np.finfo(np.float32).min)

TS_QKV = 512
TQ = 512
TK = 512
TS_BLK = 512
HALO = 32
CONV_ROWS = 32
VMEM_LIMIT = 56 * 1024 * 1024


def _rms(x, g):
    return x * lax.rsqrt(jnp.mean(x * x, axis=-1, keepdims=True) + EPS) * g


def _silu(x):
    return x * jax.nn.sigmoid(x)


def _dot(a, b):
    return jnp.dot(a, b, preferred_element_type=F32)


def _dot_nt(a, b):
    return lax.dot_general(a, b, (((1,), (1,)), ((), ())), preferred_element_type=F32)


def _mem_kv_kernel(mem_ref, g_ref, w_ref, k_ref, v_ref):
    hb = _rms(mem_ref[...], g_ref[...]).astype(BF16)
    kv = _dot(hb, w_ref[...])
    k_ref[...] = kv[:, :X_WIDTH].astype(BF16)
    v_ref[...] = kv[:, X_WIDTH:].astype(BF16)


def _mem_kv(mem, g, w):
    b, m, d = mem.shape
    return pl.pallas_call(
        _mem_kv_kernel,
        grid=(b,),
        in_specs=[pl.BlockSpec((None, m, d), lambda i: (i, 0, 0)),
                  pl.BlockSpec((1, d), lambda i: (0, 0)),
                  pl.BlockSpec((d, 2 * X_WIDTH), lambda i: (0, 0))],
        out_specs=[pl.BlockSpec((None, m, X_WIDTH), lambda i: (i, 0, 0)),
                   pl.BlockSpec((None, m, X_WIDTH), lambda i: (i, 0, 0))],
        out_shape=[jax.ShapeDtypeStruct((b, m, X_WIDTH), BF16)] * 2,
        compiler_params=pltpu.CompilerParams(dimension_semantics=("arbitrary",)),
        name="mem_kv",
    )(mem, g, w)


def _qkv_kernel(x_ref, pos_ref, ng_ref, wa_ref, qg_ref, wuq_ref, kvg_ref, wuk_ref, wuv_ref,
                invf_ref, q_ref, k_ref, v_ref):
    hb = _rms(x_ref[...], ng_ref[...]).astype(BF16)
    za = _dot(hb, wa_ref[...])
    cq = _rms(za[:, :Q_LORA], qg_ref[...]).astype(BF16)
    ckv = _rms(za[:, Q_LORA:Q_LORA + KV_LORA], kvg_ref[...]).astype(BF16)
    kr = za[:, Q_LORA + KV_LORA:]

    ang = pos_ref[...].astype(F32) * invf_ref[...]
    cos, sin = jnp.cos(ang), jnp.sin(ang)
    lane = lax.broadcasted_iota(jnp.int32, (1, LANES), 1)
    in_rope = (lane >= ROPE_LO) & (lane < ROPE_HI)
    first_half = in_rope & (lane < ROPE_LO + ROPE_HALF)
    c_tab = jnp.where(lane < ROPE_LO, 1.0, jnp.where(in_rope, cos, 0.0))
    s_from_lo = jnp.where(in_rope & ~first_half, sin, 0.0)
    s_from_hi = jnp.where(first_half, -sin, 0.0)

    def rope(t):
        return (t * c_tab + pltpu.roll(t, ROPE_HALF, 1) * s_from_lo
                + pltpu.roll(t, LANES - ROPE_HALF, 1) * s_from_hi)

    kr_rot = rope(kr)
    q = _dot(cq, wuq_ref[...])
    kn = _dot(ckv, wuk_ref[...])
    for h in range(MLA_HEADS):
        sl = slice(HEAD_PAD * h, HEAD_PAD * (h + 1))
        q_ref[:, sl] = (rope(q[:, sl]) * MLA_SCALE).astype(BF16)
        k_ref[:, sl] = (kn[:, sl] + kr_rot).astype(BF16)
    v_ref[...] = _dot(ckv, wuv_ref[...]).astype(BF16)


def _qkv(x, pos3, ng, wa, qg, wuq, kvg, wuk, wuv, invf):
    b, s, d = x.shape
    ts = TS_QKV
    const = lambda shape: pl.BlockSpec(shape, lambda i, j: (0,) * len(shape))
    hp = MLA_HEADS * HEAD_PAD
    return pl.pallas_call(
        _qkv_kernel,
        grid=(b, s // ts),
        in_specs=[pl.BlockSpec((None, ts, d), lambda i, j: (i, j, 0)),
                  pl.BlockSpec((None, ts, 1), lambda i, j: (i, j, 0)),
                  const((1, d)), const(wa.shape), const((1, Q_LORA)), const(wuq.shape),
                  const((1, KV_LORA)), const(wuk.shape), const(wuv.shape), const((1, LANES))],
        out_specs=[pl.BlockSpec((None, ts, hp), lambda i, j: (i, j, 0)),
                   pl.BlockSpec((None, ts, hp), lambda i, j: (i, j, 0)),
                   pl.BlockSpec((None, ts, MLA_WIDTH), lambda i, j: (i, j, 0))],
        out_shape=[jax.ShapeDtypeStruct((b, s, hp), BF16),
                   jax.ShapeDtypeStruct((b, s, hp), BF16),
                   jax.ShapeDtypeStruct((b, s, MLA_WIDTH), BF16)],
        compiler_params=pltpu.CompilerParams(dimension_semantics=("arbitrary", "arbitrary"),
                                             vmem_limit_bytes=VMEM_LIMIT),
        name="qkv",
    )(x, pos3, ng, wa, qg, wuq, kvg, wuk, wuv, invf)


def _attn_kernel(q_ref, k_ref, v_ref, o_ref):
    qi = pl.program_id(2)
    row = lax.broadcasted_iota(jnp.int32, (TQ, TK), 0)
    col = lax.broadcasted_iota(jnp.int32, (TQ, TK), 1)
    causal = col <= row
    outs = []
    for hh in range(2):
        hs = slice(HEAD_PAD * hh, HEAD_PAD * (hh + 1))
        q = q_ref[:, hs]

        def step(j, carry, masked):
            m, l, acc = carry
            k0 = pl.multiple_of(j * TK, TK)
            k = k_ref[pl.ds(k0, TK), hs]
            v = v_ref[pl.ds(k0, TK), :]
            s = _dot_nt(q, k)
            if masked:
                s = jnp.where(causal, s, NEG)
            m_new = jnp.maximum(m, jnp.max(s, axis=-1, keepdims=True))
            alpha = jnp.exp(m - m_new)
            p = jnp.exp(s - m_new)
            l = alpha * l + jnp.sum(p, axis=-1, keepdims=True)
            acc = alpha * acc + _dot(p.astype(BF16), v)
            return m_new, l, acc

        init = (jnp.full((TQ, 1), NEG, F32), jnp.zeros((TQ, 1), F32), jnp.zeros((TQ, LANES), F32))
        carry = lax.fori_loop(0, qi, functools.partial(step, masked=False), init)
        _, l, acc = step(qi, carry, masked=True)
        outs.append(acc / l)
    lane = lax.broadcasted_iota(jnp.int32, (1, LANES), 1)
    o_ref[...] = jnp.where(lane < V_HEAD, outs[0], outs[1]).astype(BF16)


def _mla_attn(qp, kp, v):
    b, s, _ = qp.shape
    pairs = MLA_HEADS // 2
    return pl.pallas_call(
        _attn_kernel,
        grid=(b, pairs, s // TQ),
        in_specs=[pl.BlockSpec((None, TQ, 2 * HEAD_PAD), lambda i, p, j: (i, j, p)),
                  pl.BlockSpec((None, s, 2 * HEAD_PAD), lambda i, p, j: (i, 0, p)),
                  pl.BlockSpec((None, s, LANES), lambda i, p, j: (i, 0, p))],
        out_specs=pl.BlockSpec((None, TQ, LANES), lambda i, p, j: (i, j, p)),
        out_shape=jax.ShapeDtypeStruct((b, s, MLA_WIDTH), BF16),
        compiler_params=pltpu.CompilerParams(
            dimension_semantics=("arbitrary", "arbitrary", "arbitrary"),
            vmem_limit_bytes=VMEM_LIMIT),
        name="mla_attn",
    )(qp, kp, v)


_C_VAL, _C_GLU, _C_GATE, _M_GATE, _X_Q, _X_GATE, _G0 = (
    0, CONV_WIDTH, 2 * CONV_WIDTH, 3 * CONV_WIDTH, 3 * CONV_WIDTH + MLA_WIDTH,
    3 * CONV_WIDTH + MLA_WIDTH + X_WIDTH, 3 * CONV_WIDTH + MLA_WIDTH + 2 * X_WIDTH)
W_B_COLS = _G0 + 3 * D_MODEL


def _block_kernel(x_ref, o_mla_ref, kx_ref, vx_ref, ng_ref, wb_ref, bg_ref, cw_ref, cb_ref,
                  lng_ref, lnb_ref, wco_ref, wmo_ref, wxo_ref, wout_ref, fg_ref,
                  out_ref, ubuf):
    ts = TS_BLK
    x = x_ref[...]
    hb = _rms(x, ng_ref[...]).astype(BF16)

    def proj(lo, width):
        return _dot(hb, wb_ref[:, lo:lo + width])

    def gate(i):
        lo = _G0 + i * D_MODEL
        return jax.nn.sigmoid(proj(lo, D_MODEL) + bg_ref[:, i * D_MODEL:(i + 1) * D_MODEL])

    @pl.when(pl.program_id(1) == 0)
    def _():
        ubuf[0:HALO, :] = jnp.zeros((HALO, CONV_WIDTH), F32)

    ubuf[HALO:HALO + ts, :] = proj(_C_VAL, CONV_WIDTH) * jax.nn.sigmoid(proj(_C_GLU, CONV_WIDTH))
    first = HALO - (CONV_KERNEL - 1)
    chunks = []
    for r0 in range(0, ts, CONV_ROWS):
        acc = jnp.zeros((CONV_ROWS, CONV_WIDTH), F32) + cb_ref[...]
        for kk in range(CONV_KERNEL):
            acc = acc + ubuf[r0 + first + kk:r0 + first + kk + CONV_ROWS, :] * cw_ref[kk:kk + 1, :]
        chunks.append(acc)
    conv = jnp.concatenate(chunks, axis=0)
    ubuf[0:HALO, :] = ubuf[ts:ts + HALO, :]
    mu = jnp.mean(conv, axis=-1, keepdims=True)
    cen = conv - mu
    var = jnp.mean(cen * cen, axis=-1, keepdims=True)
    ln = cen * lax.rsqrt(var + EPS) * lng_ref[...] + lnb_ref[...]
    a_conv = _silu(ln) * _silu(proj(_C_GATE, CONV_WIDTH))
    merged = gate(0) * _dot(a_conv.astype(BF16), wco_ref[...])

    a_mla = o_mla_ref[...].astype(F32) * _silu(proj(_M_GATE, MLA_WIDTH))
    merged = merged + gate(1) * _dot(a_mla.astype(BF16), wmo_ref[...])

    xq = proj(_X_Q, X_WIDTH).astype(BF16)
    heads = []
    for h in range(X_HEADS):
        hs = slice(X_HEAD_DIM * h, X_HEAD_DIM * (h + 1))
        s = _dot_nt(xq[:, hs], kx_ref[:, hs]) * X_SCALE
        p = jnp.exp(s - jnp.max(s, axis=-1, keepdims=True))
        p = p / jnp.sum(p, axis=-1, keepdims=True)
        heads.append(_dot(p.astype(BF16), vx_ref[:, hs]))
    a_x = jnp.concatenate(heads, axis=-1) * _silu(proj(_X_GATE, X_WIDTH))
    merged = merged + gate(2) * _dot(a_x.astype(BF16), wxo_ref[...])

    y = x + _dot(merged.astype(BF16), wout_ref[...])
    out_ref[...] = _rms(y, fg_ref[...])


def _block(x, o_mla, kx, vx, ng, wb, bg, cw, cb, lng, lnb, wco, wmo, wxo, wout, fg):
    b, s, d = x.shape
    ts = TS_BLK
    const = lambda shape: pl.BlockSpec(shape, lambda i, j: (0,) * len(shape),
                                       pipeline_mode=pl.Buffered(1))
    return pl.pallas_call(
        _block_kernel,
        grid=(b, s // ts),
        in_specs=[pl.BlockSpec((None, ts, d), lambda i, j: (i, j, 0)),
                  pl.BlockSpec((None, ts, MLA_WIDTH), lambda i, j: (i, j, 0)),
                  pl.BlockSpec((None, N_MEM, X_WIDTH), lambda i, j: (i, 0, 0)),
                  pl.BlockSpec((None, N_MEM, X_WIDTH), lambda i, j: (i, 0, 0)),
                  const((1, d)), const(wb.shape), const(bg.shape), const(cw.shape),
                  const((1, CONV_WIDTH)), const((1, CONV_WIDTH)), const((1, CONV_WIDTH)),
                  const(wco.shape), const(wmo.shape), const(wxo.shape), const(wout.shape),
                  const((1, d))],
        out_specs=pl.BlockSpec((None, ts, d), lambda i, j: (i, j, 0)),
        out_shape=jax.ShapeDtypeStruct((b, s, d), F32),
        scratch_shapes=[pltpu.VMEM((HALO + ts, CONV_WIDTH), F32)],
        compiler_params=pltpu.CompilerParams(dimension_semantics=("arbitrary", "arbitrary"),
                                             vmem_limit_bytes=VMEM_LIMIT),
        name="block",
    )(x, o_mla, kx, vx, ng, wb, bg, cw, cb, lng, lnb, wco, wmo, wxo, wout, fg)


def _inv_freq_lanes():
    inv = ROPE_THETA ** (-np.arange(0, QK_ROPE, 2, dtype=np.float32) / QK_ROPE)
    lanes = np.zeros((1, LANES), np.float32)
    lanes[0, ROPE_LO:ROPE_LO + ROPE_HALF] = inv
    lanes[0, ROPE_LO + ROPE_HALF:ROPE_HI] = inv
    return jnp.asarray(lanes)


def kernel(x, mem, positions, norm_g, w_in, b_gate, conv_w, conv_b, conv_ln_g, conv_ln_b,
           w_conv_o, q_norm_g, w_uq, kv_norm_g, w_ukv, w_mla_o, mem_norm_g, w_mem_kv, w_x_o,
           w_out, final_norm_g):
    assert norm_g.shape[0] == 1, "single-layer trunk"
    b, s, d = x.shape
    row = lambda v: v.reshape(1, -1)
    edges = np.cumsum((0,) + IN_SPLITS)
    (w_cval, w_cglu, w_cgate, w_qd, w_kvd, w_kr, w_mg, w_xq, w_xg, w_g) = (
        w_in[0][:, edges[i]:edges[i + 1]] for i in range(len(IN_SPLITS)))

    w_kr_pad = jnp.pad(w_kr, ((0, 0), (ROPE_LO, LANES - ROPE_HI)))
    wa = jnp.concatenate([w_qd, w_kvd, w_kr_pad], axis=1).astype(BF16)
    wb = jnp.concatenate([w_cval, w_cglu, w_cgate, w_mg, w_xq, w_xg, w_g], axis=1).astype(BF16)
    wuq = jnp.pad(w_uq[0].reshape(Q_LORA, MLA_HEADS, QK_NOPE + QK_ROPE),
                  ((0, 0), (0, 0), (0, HEAD_PAD - QK_NOPE - QK_ROPE)))
    wuq = wuq.reshape(Q_LORA, MLA_HEADS * HEAD_PAD).astype(BF16)
    wukv = w_ukv[0].reshape(KV_LORA, MLA_HEADS, QK_NOPE + V_HEAD)
    wuk = jnp.pad(wukv[:, :, :QK_NOPE], ((0, 0), (0, 0), (0, HEAD_PAD - QK_NOPE)))
    wuk = wuk.reshape(KV_LORA, MLA_HEADS * HEAD_PAD).astype(BF16)
    wuv = wukv[:, :, QK_NOPE:].reshape(KV_LORA, MLA_WIDTH).astype(BF16)

    kx, vx = _mem_kv(mem, row(mem_norm_g[0]), w_mem_kv[0].astype(BF16))
    qp, kp, v = _qkv(x, positions.reshape(b, s, 1), row(norm_g[0]), wa, row(q_norm_g[0]), wuq,
                     row(kv_norm_g[0]), wuk, wuv, _inv_freq_lanes())
    o_mla = _mla_attn(qp, kp, v)
    return _block(x, o_mla, kx, vx, row(norm_g[0]), wb, row(b_gate[0]), conv_w[0],
                  row(conv_b[0]), row(conv_ln_g[0]), row(conv_ln_b[0]),
                  w_conv_o[0].astype(BF16), w_mla_o[0].astype(BF16), w_x_o[0].astype(BF16),
                  w_out[0].astype(BF16), row(final_norm_g))
```

```python
import functools

import jax
import jax.numpy as jnp
import numpy as np
from jax import lax
from jax.experimental import pallas as pl
from jax.experimental.pallas import tpu as pltpu

F32 = jnp.float32
BF16 = jnp.bfloat16

D_MODEL = 1024
N_MEM = 256
CONV_WIDTH = 512
CONV_KERNEL = 31
MLA_HEADS = 8
QK_NOPE = 64
QK_ROPE = 32
V_HEAD = 64
Q_LORA = 384
KV_LORA = 256
MLA_WIDTH = MLA_HEADS * V_HEAD
X_HEADS = 4
X_HEAD_DIM = 128
X_WIDTH = X_HEADS * X_HEAD_DIM
ROPE_THETA = 10000.0
EPS = 1e-6
IN_SPLITS = (CONV_WIDTH, CONV_WIDTH, CONV_WIDTH, Q_LORA, KV_LORA, QK_ROPE, MLA_WIDTH,
             X_WIDTH, X_WIDTH, 3 * D_MODEL)

LANES = 128
HEAD_PAD = LANES
ROPE_LO = QK_NOPE
ROPE_HALF = QK_ROPE // 2
ROPE_HI = QK_NOPE + QK_ROPE
MLA_SCALE = (QK_NOPE + QK_ROPE) ** -0.5
X_SCALE = X_HEAD_DIM ** -0.5
LOG2E = float(np.log2(np.e))
NEG = float(np.finfo(np.float32).min)

TS_QKV = 512
TQ = 512
TK = 512
TS_BLK = 512
HALO = 32
CONV_ROWS = 32
VMEM_LIMIT = 56 * 1024 * 1024


def _rms(x, g):
    return x * lax.rsqrt(jnp.mean(x * x, axis=-1, keepdims=True) + EPS) * g


def _silu(x):
    return x * jax.nn.sigmoid(x)


def _dot(a, b):
    return jnp.dot(a, b, preferred_element_type=F32)


def _dot_nt(a, b):
    return lax.dot_general(a, b, (((1,), (1,)), ((), ())), preferred_element_type=F32)


def _mem_kv_kernel(mem_ref, g_ref, w_ref, k_ref, v_ref):
    hb = _rms(mem_ref[...], g_ref[...]).astype(BF16)
    kv = _dot(hb, w_ref[...])
    k_ref[...] = kv[:, :X_WIDTH].astype(BF16)
    v_ref[...] = kv[:, X_WIDTH:].astype(BF16)


def _mem_kv(mem, g, w):
    b, m, d = mem.shape
    return pl.pallas_call(
        _mem_kv_kernel,
        grid=(b,),
        in_specs=[pl.BlockSpec((None, m, d), lambda i: (i, 0, 0)),
                  pl.BlockSpec((1, d), lambda i: (0, 0)),
                  pl.BlockSpec((d, 2 * X_WIDTH), lambda i: (0, 0))],
        out_specs=[pl.BlockSpec((None, m, X_WIDTH), lambda i: (i, 0, 0)),
                   pl.BlockSpec((None, m, X_WIDTH), lambda i: (i, 0, 0))],
        out_shape=[jax.ShapeDtypeStruct((b, m, X_WIDTH), BF16)] * 2,
        compiler_params=pltpu.CompilerParams(dimension_semantics=("arbitrary",)),
        name="mem_kv",
    )(mem, g, w)


def _qkv_kernel(x_ref, pos_ref, ng_ref, wa_ref, qg_ref, wuq_ref, kvg_ref, wuk_ref, wuv_ref,
                invf_ref, qt_ref, k_ref, vt_ref):
    hb = _rms(x_ref[...], ng_ref[...]).astype(BF16)
    za = _dot(hb, wa_ref[...])
    cq = _rms(za[:, :Q_LORA], qg_ref[...]).astype(BF16)
    ckv = _rms(za[:, Q_LORA:Q_LORA + KV_LORA], kvg_ref[...]).astype(BF16)
    kr = za[:, Q_LORA + KV_LORA:]

    ang = pos_ref[...].astype(F32) * invf_ref[...]
    cos, sin = jnp.cos(ang), jnp.sin(ang)
    lane = lax.broadcasted_iota(jnp.int32, (1, LANES), 1)
    in_rope = (lane >= ROPE_LO) & (lane < ROPE_HI)
    first_half = in_rope & (lane < ROPE_LO + ROPE_HALF)
    c_tab = jnp.where(lane < ROPE_LO, 1.0, jnp.where(in_rope, cos, 0.0))
    s_from_lo = jnp.where(in_rope & ~first_half, sin, 0.0)
    s_from_hi = jnp.where(first_half, -sin, 0.0)

    def rope(t):
        return (t * c_tab + pltpu.roll(t, ROPE_HALF, 1) * s_from_lo
                + pltpu.roll(t, LANES - ROPE_HALF, 1) * s_from_hi)

    kr_rot = rope(kr)
    q = _dot(cq, wuq_ref[...])
    kn = _dot(ckv, wuk_ref[...])
    for h in range(MLA_HEADS):
        sl = slice(HEAD_PAD * h, HEAD_PAD * (h + 1))
        qt_ref[h] = (rope(q[:, sl]) * MLA_SCALE * LOG2E).T.astype(BF16)
        k_ref[:, sl] = (kn[:, sl] + kr_rot).astype(BF16)
    vt_ref[...] = _dot(ckv, wuv_ref[...]).T.astype(BF16)


def _qkv(x, pos3, ng, wa, qg, wuq, kvg, wuk, wuv, invf):
    b, s, d = x.shape
    ts = TS_QKV
    const = lambda shape: pl.BlockSpec(shape, lambda i, j: (0,) * len(shape))
    hp = MLA_HEADS * HEAD_PAD
    return pl.pallas_call(
        _qkv_kernel,
        grid=(b, s // ts),
        in_specs=[pl.BlockSpec((None, ts, d), lambda i, j: (i, j, 0)),
                  pl.BlockSpec((None, ts, 1), lambda i, j: (i, j, 0)),
                  const((1, d)), const(wa.shape), const((1, Q_LORA)), const(wuq.shape),
                  const((1, KV_LORA)), const(wuk.shape), const(wuv.shape), const((1, LANES))],
        out_specs=[pl.BlockSpec((None, MLA_HEADS, HEAD_PAD, ts), lambda i, j: (i, 0, 0, j)),
                   pl.BlockSpec((None, ts, hp), lambda i, j: (i, j, 0)),
                   pl.BlockSpec((None, MLA_WIDTH, ts), lambda i, j: (i, 0, j))],
        out_shape=[jax.ShapeDtypeStruct((b, MLA_HEADS, HEAD_PAD, s), BF16),
                   jax.ShapeDtypeStruct((b, s, hp), BF16),
                   jax.ShapeDtypeStruct((b, MLA_WIDTH, s), BF16)],
        compiler_params=pltpu.CompilerParams(dimension_semantics=("arbitrary", "arbitrary"),
                                             vmem_limit_bytes=VMEM_LIMIT),
        name="qkv",
    )(x, pos3, ng, wa, qg, wuq, kvg, wuk, wuv, invf)


def _attn_kernel(qt_ref, k_ref, vt_ref, o_ref, s_scr, m_scr, l_scr, acc_scr):
    nb = pl.program_id(2) + 1
    heads = (0, 1)

    def scores(j, slot):
        k0 = pl.multiple_of(j * TK, TK)
        for hh in heads:
            k = k_ref[pl.ds(k0, TK), HEAD_PAD * hh:HEAD_PAD * (hh + 1)]
            s_scr[slot, hh] = _dot(k, qt_ref[hh])

    def absorb(j, slot, masked):
        k0 = pl.multiple_of(j * TK, TK)
        for hh in heads:
            s = s_scr[slot, hh]
            if masked:
                key = lax.broadcasted_iota(jnp.int32, (TK, TQ), 0)
                qry = lax.broadcasted_iota(jnp.int32, (TK, TQ), 1)
                s = jnp.where(key <= qry, s, NEG)
            m_old = m_scr[hh]
            m_new = jnp.maximum(m_old, jnp.max(s, axis=0, keepdims=True))
            alpha = jnp.exp2(m_old - m_new)
            p = jnp.exp2(s - m_new)
            m_scr[hh] = m_new
            l_scr[hh] = alpha * l_scr[hh] + jnp.sum(p, axis=0, keepdims=True)
            vt = vt_ref[V_HEAD * hh:V_HEAD * (hh + 1), pl.ds(k0, TK)]
            acc_scr[hh] = alpha * acc_scr[hh] + _dot(vt, p.astype(BF16))

    m_scr[...] = jnp.full(m_scr.shape, NEG, F32)
    l_scr[...] = jnp.zeros(l_scr.shape, F32)
    acc_scr[...] = jnp.zeros(acc_scr.shape, F32)
    scores(0, 0)

    @pl.loop(0, (nb - 1) // 2)
    def _(t):
        scores(2 * t + 1, 1)
        absorb(2 * t, 0, masked=False)
        scores(2 * t + 2, 0)
        absorb(2 * t + 1, 1, masked=False)

    @pl.when(nb % 2 == 1)
    def _():
        absorb(nb - 1, 0, masked=True)

    @pl.when(nb % 2 == 0)
    def _():
        scores(nb - 1, 1)
        absorb(nb - 2, 0, masked=False)
        absorb(nb - 1, 1, masked=True)

    ot = jnp.concatenate([acc_scr[hh] / l_scr[hh] for hh in heads], axis=0)
    o_ref[...] = ot.T.astype(BF16)


def _mla_attn(qt, kp, vt):
    b, s, _ = kp.shape
    pairs = MLA_HEADS // 2
    return pl.pallas_call(
        _attn_kernel,
        grid=(b, pairs, s // TQ),
        in_specs=[pl.BlockSpec((None, 2, HEAD_PAD, TQ), lambda i, p, j: (i, p, 0, j)),
                  pl.BlockSpec((None, s, 2 * HEAD_PAD), lambda i, p, j: (i, 0, p)),
                  pl.BlockSpec((None, 2 * V_HEAD, s), lambda i, p, j: (i, p, 0))],
        out_specs=pl.BlockSpec((None, TQ, LANES), lambda i, p, j: (i, j, p)),
        out_shape=jax.ShapeDtypeStruct((b, s, MLA_WIDTH), BF16),
        scratch_shapes=[pltpu.VMEM((2, 2, TK, TQ), F32),
                        pltpu.VMEM((2, 1, TQ), F32),
                        pltpu.VMEM((2, 1, TQ), F32),
                        pltpu.VMEM((2, V_HEAD, TQ), F32)],
        compiler_params=pltpu.CompilerParams(
            dimension_semantics=("arbitrary", "arbitrary", "arbitrary"),
            vmem_limit_bytes=VMEM_LIMIT),
        name="mla_attn",
    )(qt, kp, vt)


_C_VAL, _C_GLU, _C_GATE, _M_GATE, _X_Q, _X_GATE, _G0 = (
    0, CONV_WIDTH, 2 * CONV_WIDTH, 3 * CONV_WIDTH, 3 * CONV_WIDTH + MLA_WIDTH,
    3 * CONV_WIDTH + MLA_WIDTH + X_WIDTH, 3 * CONV_WIDTH + MLA_WIDTH + 2 * X_WIDTH)
W_B_COLS = _G0 + 3 * D_MODEL


def _block_kernel(x_ref, o_mla_ref, kx_ref, vx_ref, ng_ref, wb_ref, bg_ref, cw_ref, cb_ref,
                  lng_ref, lnb_ref, wco_ref, wmo_ref, wxo_ref, wout_ref, fg_ref,
                  out_ref, ubuf):
    ts = TS_BLK
    x = x_ref[...]
    hb = _rms(x, ng_ref[...]).astype(BF16)

    def proj(lo, width):
        return _dot(hb, wb_ref[:, lo:lo + width])

    def gate(i):
        lo = _G0 + i * D_MODEL
        return jax.nn.sigmoid(proj(lo, D_MODEL) + bg_ref[:, i * D_MODEL:(i + 1) * D_MODEL])

    @pl.when(pl.program_id(1) == 0)
    def _():
        ubuf[0:HALO, :] = jnp.zeros((HALO, CONV_WIDTH), F32)

    ubuf[HALO:HALO + ts, :] = proj(_C_VAL, CONV_WIDTH) * jax.nn.sigmoid(proj(_C_GLU, CONV_WIDTH))
    first = HALO - (CONV_KERNEL - 1)
    chunks = []
    for r0 in range(0, ts, CONV_ROWS):
        acc = jnp.zeros((CONV_ROWS, CONV_WIDTH), F32) + cb_ref[...]
        for kk in range(CONV_KERNEL):
            acc = acc + ubuf[r0 + first + kk:r0 + first + kk + CONV_ROWS, :] * cw_ref[kk:kk + 1, :]
        chunks.append(acc)
    conv = jnp.concatenate(chunks, axis=0)
    ubuf[0:HALO, :] = ubuf[ts:ts + HALO, :]
    mu = jnp.mean(conv, axis=-1, keepdims=True)
    cen = conv - mu
    var = jnp.mean(cen * cen, axis=-1, keepdims=True)
    ln = cen * lax.rsqrt(var + EPS) * lng_ref[...] + lnb_ref[...]
    a_conv = _silu(ln) * _silu(proj(_C_GATE, CONV_WIDTH))
    merged = gate(0) * _dot(a_conv.astype(BF16), wco_ref[...])

    a_mla = o_mla_ref[...].astype(F32) * _silu(proj(_M_GATE, MLA_WIDTH))
    merged = merged + gate(1) * _dot(a_mla.astype(BF16), wmo_ref[...])

    xq = proj(_X_Q, X_WIDTH).astype(BF16)
    heads = []
    for h in range(X_HEADS):
        hs = slice(X_HEAD_DIM * h, X_HEAD_DIM * (h + 1))
        s = _dot_nt(xq[:, hs], kx_ref[:, hs]) * X_SCALE
        p = jnp.exp(s - jnp.max(s, axis=-1, keepdims=True))
        p = p / jnp.sum(p, axis=-1, keepdims=True)
        heads.append(_dot(p.astype(BF16), vx_ref[:, hs]))
    a_x = jnp.concatenate(heads, axis=-1) * _silu(proj(_X_GATE, X_WIDTH))
    merged = merged + gate(2) * _dot(a_x.astype(BF16), wxo_ref[...])

    y = x + _dot(merged.astype(BF16), wout_ref[...])
    out_ref[...] = _rms(y, fg_ref[...])


def _block(x, o_mla, kx, vx, ng, wb, bg, cw, cb, lng, lnb, wco, wmo, wxo, wout, fg):
    b, s, d = x.shape
    ts = TS_BLK
    const = lambda shape: pl.BlockSpec(shape, lambda i, j: (0,) * len(shape),
                                       pipeline_mode=pl.Buffered(1))
    return pl.pallas_call(
        _block_kernel,
        grid=(b, s // ts),
        in_specs=[pl.BlockSpec((None, ts, d), lambda i, j: (i, j, 0)),
                  pl.BlockSpec((None, ts, MLA_WIDTH), lambda i, j: (i, j, 0)),
                  pl.BlockSpec((None, N_MEM, X_WIDTH), lambda i, j: (i, 0, 0)),
                  pl.BlockSpec((None, N_MEM, X_WIDTH), lambda i, j: (i, 0, 0)),
                  const((1, d)), const(wb.shape), const(bg.shape), const(cw.shape),
                  const((1, CONV_WIDTH)), const((1, CONV_WIDTH)), const((1, CONV_WIDTH)),
                  const(wco.shape), const(wmo.shape), const(wxo.shape), const(wout.shape),
                  const((1, d))],
        out_specs=pl.BlockSpec((None, ts, d), lambda i, j: (i, j, 0)),
        out_shape=jax.ShapeDtypeStruct((b, s, d), F32),
        scratch_shapes=[pltpu.VMEM((HALO + ts, CONV_WIDTH), F32)],
        compiler_params=pltpu.CompilerParams(dimension_semantics=("arbitrary", "arbitrary"),
                                             vmem_limit_bytes=VMEM_LIMIT),
        name="block",
    )(x, o_mla, kx, vx, ng, wb, bg, cw, cb, lng, lnb, wco, wmo, wxo, wout, fg)


def _inv_freq_lanes():
    inv = ROPE_THETA ** (-np.arange(0, QK_ROPE, 2, dtype=np.float32) / QK_ROPE)
    lanes = np.zeros((1, LANES), np.float32)
    lanes[0, ROPE_LO:ROPE_LO + ROPE_HALF] = inv
    lanes[0, ROPE_LO + ROPE_HALF:ROPE_HI] = inv
    return jnp.asarray(lanes)


def kernel(x, mem, positions, norm_g, w_in, b_gate, conv_w, conv_b, conv_ln_g, conv_ln_b,
           w_conv_o, q_norm_g, w_uq, kv_norm_g, w_ukv, w_mla_o, mem_norm_g, w_mem_kv, w_x_o,
           w_out, final_norm_g):
    assert norm_g.shape[0] == 1, "single-layer trunk"
    b, s, d = x.shape
    row = lambda v: v.reshape(1, -1)
    edges = np.cumsum((0,) + IN_SPLITS)
    (w_cval, w_cglu, w_cgate, w_qd, w_kvd, w_kr, w_mg, w_xq, w_xg, w_g) = (
        w_in[0][:, edges[i]:edges[i + 1]] for i in range(len(IN_SPLITS)))

    w_kr_pad = jnp.pad(w_kr, ((0, 0), (ROPE_LO, LANES - ROPE_HI)))
    wa = jnp.concatenate([w_qd, w_kvd, w_kr_pad], axis=1).astype(BF16)
    wb = jnp.concatenate([w_cval, w_cglu, w_cgate, w_mg, w_xq, w_xg, w_g], axis=1).astype(BF16)
    wuq = jnp.pad(w_uq[0].reshape(Q_LORA, MLA_HEADS, QK_NOPE + QK_ROPE),
                  ((0, 0), (0, 0), (0, HEAD_PAD - QK_NOPE - QK_ROPE)))
    wuq = wuq.reshape(Q_LORA, MLA_HEADS * HEAD_PAD).astype(BF16)
    wukv = w_ukv[0].reshape(KV_LORA, MLA_HEADS, QK_NOPE + V_HEAD)
    wuk = jnp.pad(wukv[:, :, :QK_NOPE], ((0, 0), (0, 0), (0, HEAD_PAD - QK_NOPE)))
    wuk = wuk.reshape(KV_LORA, MLA_HEADS * HEAD_PAD).astype(BF16)
    wuv = wukv[:, :, QK_NOPE:].reshape(KV_LORA, MLA_WIDTH).astype(BF16)

    kx, vx = _mem_kv(mem, row(mem_norm_g[0]), w_mem_kv[0].astype(BF16))
    qt, kp, vt = _qkv(x, positions.reshape(b, s, 1), row(norm_g[0]), wa, row(q_norm_g[0]), wuq,
                      row(kv_norm_g[0]), wuk, wuv, _inv_freq_lanes())
    o_mla = _mla_attn(qt, kp, vt)
    return _block(x, o_mla, kx, vx, row(norm_g[0]), wb, row(b_gate[0]), conv_w[0],
                  row(conv_b[0]), row(conv_ln_g[0]), row(conv_ln_b[0]),
                  w_conv_o[0].astype(BF16), w_mla_o[0].astype(BF16), w_x_o[0].astype(BF16),
                  w_out[0].astype(BF16), row(final_norm_g))
```

```python
import functools

import jax
import jax.numpy as jnp
import numpy as np
from jax import lax
from jax.experimental import pallas as pl
from jax.experimental.pallas import tpu as pltpu

F32 = jnp.float32
BF16 = jnp.bfloat16

D_MODEL = 1024
N_MEM = 256
CONV_WIDTH = 512
CONV_KERNEL = 31
MLA_HEADS = 8
QK_NOPE = 64
QK_ROPE = 32
V_HEAD = 64
Q_LORA = 384
KV_LORA = 256
MLA_WIDTH = MLA_HEADS * V_HEAD
X_HEADS = 4
X_HEAD_DIM = 128
X_WIDTH = X_HEADS * X_HEAD_DIM
ROPE_THETA = 10000.0
EPS = 1e-6
IN_SPLITS = (CONV_WIDTH, CONV_WIDTH, CONV_WIDTH, Q_LORA, KV_LORA, QK_ROPE, MLA_WIDTH,
             X_WIDTH, X_WIDTH, 3 * D_MODEL)

LANES = 128
HEAD_PAD = LANES
ROPE_LO = QK_NOPE
ROPE_HALF = QK_ROPE // 2
ROPE_HI = QK_NOPE + QK_ROPE
MLA_SCALE = (QK_NOPE + QK_ROPE) ** -0.5
X_SCALE = X_HEAD_DIM ** -0.5
LOG2E = float(np.log2(np.e))
NEG = float(np.finfo(np.float32).min)

TS_QKV = 512
TQ = 512
TK = 512
TS_BLK = 512
HALO = 32
CONV_ROWS = 32
SUBLANES = 8
SHIFT_ROWS = TS_BLK + HALO - SUBLANES
VMEM_LIMIT = 56 * 1024 * 1024


def _rms(x, g):
    return x * lax.rsqrt(jnp.mean(x * x, axis=-1, keepdims=True) + EPS) * g


def _sigmoid(x):
    return 0.5 * jnp.tanh(0.5 * x) + 0.5


def _silu(x):
    h = 0.5 * x
    return h * jnp.tanh(h) + h


def _dot(a, b):
    return jnp.dot(a, b, preferred_element_type=F32)


def _dot_nt(a, b):
    return lax.dot_general(a, b, (((1,), (1,)), ((), ())), preferred_element_type=F32)


def _mem_kv_kernel(mem_ref, g_ref, w_ref, k_ref, v_ref):
    hb = _rms(mem_ref[...], g_ref[...]).astype(BF16)
    kv = _dot(hb, w_ref[...])
    k_ref[...] = kv[:, :X_WIDTH].astype(BF16)
    v_ref[...] = kv[:, X_WIDTH:].astype(BF16)


def _mem_kv(mem, g, w):
    b, m, d = mem.shape
    return pl.pallas_call(
        _mem_kv_kernel,
        grid=(b,),
        in_specs=[pl.BlockSpec((None, m, d), lambda i: (i, 0, 0)),
                  pl.BlockSpec((1, d), lambda i: (0, 0)),
                  pl.BlockSpec((d, 2 * X_WIDTH), lambda i: (0, 0))],
        out_specs=[pl.BlockSpec((None, m, X_WIDTH), lambda i: (i, 0, 0)),
                   pl.BlockSpec((None, m, X_WIDTH), lambda i: (i, 0, 0))],
        out_shape=[jax.ShapeDtypeStruct((b, m, X_WIDTH), BF16)] * 2,
        compiler_params=pltpu.CompilerParams(dimension_semantics=("arbitrary",)),
        name="mem_kv",
    )(mem, g, w)


def _qkv_kernel(x_ref, pos_ref, ng_ref, wa_ref, qg_ref, wuq_ref, kvg_ref, wuk_ref, wuv_ref,
                invf_ref, qt_ref, k_ref, vt_ref):
    hb = _rms(x_ref[...], ng_ref[...]).astype(BF16)
    za = _dot(hb, wa_ref[...])
    cq = _rms(za[:, :Q_LORA], qg_ref[...]).astype(BF16)
    ckv = _rms(za[:, Q_LORA:Q_LORA + KV_LORA], kvg_ref[...]).astype(BF16)
    kr = za[:, Q_LORA + KV_LORA:]

    ang = pos_ref[...].astype(F32) * invf_ref[...]
    cos, sin = jnp.cos(ang), jnp.sin(ang)
    lane = lax.broadcasted_iota(jnp.int32, (1, LANES), 1)
    in_rope = (lane >= ROPE_LO) & (lane < ROPE_HI)
    first_half = in_rope & (lane < ROPE_LO + ROPE_HALF)
    c_tab = jnp.where(lane < ROPE_LO, 1.0, jnp.where(in_rope, cos, 0.0))
    s_from_lo = jnp.where(in_rope & ~first_half, sin, 0.0)
    s_from_hi = jnp.where(first_half, -sin, 0.0)

    def rope(t):
        return (t * c_tab + pltpu.roll(t, ROPE_HALF, 1) * s_from_lo
                + pltpu.roll(t, LANES - ROPE_HALF, 1) * s_from_hi)

    kr_rot = rope(kr)
    q = _dot(cq, wuq_ref[...])
    kn = _dot(ckv, wuk_ref[...])
    for h in range(MLA_HEADS):
        sl = slice(HEAD_PAD * h, HEAD_PAD * (h + 1))
        qt_ref[h] = (rope(q[:, sl]) * MLA_SCALE * LOG2E).T.astype(BF16)
        k_ref[:, sl] = (kn[:, sl] + kr_rot).astype(BF16)
    vt_ref[...] = _dot(ckv, wuv_ref[...]).T.astype(BF16)


def _qkv(x, pos3, ng, wa, qg, wuq, kvg, wuk, wuv, invf):
    b, s, d = x.shape
    ts = TS_QKV
    const = lambda shape: pl.BlockSpec(shape, lambda i, j: (0,) * len(shape))
    hp = MLA_HEADS * HEAD_PAD
    return pl.pallas_call(
        _qkv_kernel,
        grid=(b, s // ts),
        in_specs=[pl.BlockSpec((None, ts, d), lambda i, j: (i, j, 0)),
                  pl.BlockSpec((None, ts, 1), lambda i, j: (i, j, 0)),
                  const((1, d)), const(wa.shape), const((1, Q_LORA)), const(wuq.shape),
                  const((1, KV_LORA)), const(wuk.shape), const(wuv.shape), const((1, LANES))],
        out_specs=[pl.BlockSpec((None, MLA_HEADS, HEAD_PAD, ts), lambda i, j: (i, 0, 0, j)),
                   pl.BlockSpec((None, ts, hp), lambda i, j: (i, j, 0)),
                   pl.BlockSpec((None, MLA_WIDTH, ts), lambda i, j: (i, 0, j))],
        out_shape=[jax.ShapeDtypeStruct((b, MLA_HEADS, HEAD_PAD, s), BF16),
                   jax.ShapeDtypeStruct((b, s, hp), BF16),
                   jax.ShapeDtypeStruct((b, MLA_WIDTH, s), BF16)],
        compiler_params=pltpu.CompilerParams(dimension_semantics=("arbitrary", "arbitrary"),
                                             vmem_limit_bytes=VMEM_LIMIT),
        name="qkv",
    )(x, pos3, ng, wa, qg, wuq, kvg, wuk, wuv, invf)


def _attn_kernel(qt_ref, k_ref, vt_ref, o_ref, s_scr, m_scr, l_scr, acc_scr):
    nb = pl.program_id(2) + 1
    heads = (0, 1)

    def scores(j, slot):
        k0 = pl.multiple_of(j * TK, TK)
        for hh in heads:
            k = k_ref[pl.ds(k0, TK), HEAD_PAD * hh:HEAD_PAD * (hh + 1)]
            s_scr[slot, hh] = _dot(k, qt_ref[hh])

    def absorb(j, slot, masked):
        k0 = pl.multiple_of(j * TK, TK)
        for hh in heads:
            s = s_scr[slot, hh]
            if masked:
                key = lax.broadcasted_iota(jnp.int32, (TK, TQ), 0)
                qry = lax.broadcasted_iota(jnp.int32, (TK, TQ), 1)
                s = jnp.where(key <= qry, s, NEG)
            m_old = m_scr[hh]
            m_new = jnp.maximum(m_old, jnp.max(s, axis=0, keepdims=True))
            alpha = jnp.exp2(m_old - m_new)
            p = jnp.exp2(s - m_new)
            m_scr[hh] = m_new
            l_scr[hh] = alpha * l_scr[hh] + jnp.sum(p, axis=0, keepdims=True)
            vt = vt_ref[V_HEAD * hh:V_HEAD * (hh + 1), pl.ds(k0, TK)]
            acc_scr[hh] = alpha * acc_scr[hh] + _dot(vt, p.astype(BF16))

    m_scr[...] = jnp.full(m_scr.shape, NEG, F32)
    l_scr[...] = jnp.zeros(l_scr.shape, F32)
    acc_scr[...] = jnp.zeros(acc_scr.shape, F32)
    scores(0, 0)

    @pl.loop(0, (nb - 1) // 2)
    def _(t):
        scores(2 * t + 1, 1)
        absorb(2 * t, 0, masked=False)
        scores(2 * t + 2, 0)
        absorb(2 * t + 1, 1, masked=False)

    @pl.when(nb % 2 == 1)
    def _():
        absorb(nb - 1, 0, masked=True)

    @pl.when(nb % 2 == 0)
    def _():
        scores(nb - 1, 1)
        absorb(nb - 2, 0, masked=False)
        absorb(nb - 1, 1, masked=True)

    ot = jnp.concatenate([acc_scr[hh] / l_scr[hh] for hh in heads], axis=0)
    o_ref[...] = ot.T.astype(BF16)


def _mla_attn(qt, kp, vt):
    b, s, _ = kp.shape
    pairs = MLA_HEADS // 2
    return pl.pallas_call(
        _attn_kernel,
        grid=(b, pairs, s // TQ),
        in_specs=[pl.BlockSpec((None, 2, HEAD_PAD, TQ), lambda i, p, j: (i, p, 0, j)),
                  pl.BlockSpec((None, s, 2 * HEAD_PAD), lambda i, p, j: (i, 0, p)),
                  pl.BlockSpec((None, 2 * V_HEAD, s), lambda i, p, j: (i, p, 0))],
        out_specs=pl.BlockSpec((None, TQ, LANES), lambda i, p, j: (i, j, p)),
        out_shape=jax.ShapeDtypeStruct((b, s, MLA_WIDTH), BF16),
        scratch_shapes=[pltpu.VMEM((2, 2, TK, TQ), F32),
                        pltpu.VMEM((2, 1, TQ), F32),
                        pltpu.VMEM((2, 1, TQ), F32),
                        pltpu.VMEM((2, V_HEAD, TQ), F32)],
        compiler_params=pltpu.CompilerParams(
            dimension_semantics=("arbitrary", "arbitrary", "arbitrary"),
            vmem_limit_bytes=VMEM_LIMIT),
        name="mla_attn",
    )(qt, kp, vt)


_C_VAL, _C_GLU, _C_GATE, _M_GATE, _X_Q, _X_GATE, _G0 = (
    0, CONV_WIDTH, 2 * CONV_WIDTH, 3 * CONV_WIDTH, 3 * CONV_WIDTH + MLA_WIDTH,
    3 * CONV_WIDTH + MLA_WIDTH + X_WIDTH, 3 * CONV_WIDTH + MLA_WIDTH + 2 * X_WIDTH)
W_B_COLS = _G0 + 3 * D_MODEL


def _block_kernel(x_ref, o_mla_ref, kx_ref, vx_ref, ng_ref, wb_ref, bg_ref, cw_ref, cb_ref,
                  lng_ref, lnb_ref, wco_ref, wmo_ref, wxo_ref, wout_ref, fg_ref,
                  out_ref, hb_scr, ubuf, ush, conv_scr, part_scr, cgate_scr, g0_scr):
    ts = TS_BLK
    hb_scr[...] = _rms(x_ref[...], ng_ref[...]).astype(BF16)

    def proj(lo, width):
        return _dot(hb_scr[...], wb_ref[:, lo:lo + width])

    def gate(i, lo, width):
        c0 = i * D_MODEL + lo
        return _sigmoid(proj(_G0 + c0, width) + bg_ref[:, c0:c0 + width])

    @pl.when(pl.program_id(1) == 0)
    def _():
        ubuf[0:HALO, :] = jnp.zeros((HALO, CONV_WIDTH), F32)

    ubuf[HALO:HALO + ts, :] = proj(_C_VAL, CONV_WIDTH) * _sigmoid(proj(_C_GLU, CONV_WIDTH))
    for r in range(1, SUBLANES):
        ush[r - 1] = ubuf[r:r + SHIFT_ROWS, :]
    first = HALO - (CONV_KERNEL - 1)

    def conv_chunk(i):
        r0 = i * CONV_ROWS
        groups = CONV_ROWS // SUBLANES
        acc = jnp.zeros((groups, SUBLANES, CONV_WIDTH), F32) + cb_ref[...]
        for kk in range(CONV_KERNEL):
            r = (first + kk) % SUBLANES
            row = r0 + first + kk - r
            tap = ubuf[row:row + CONV_ROWS, :] if r == 0 else ush[r - 1, row:row + CONV_ROWS, :]
            w = cw_ref[kk * SUBLANES:(kk + 1) * SUBLANES, :]
            acc = acc + tap.reshape(groups, SUBLANES, CONV_WIDTH) * w
        conv_scr[r0:r0 + CONV_ROWS, :] = acc.reshape(CONV_ROWS, CONV_WIDTH)

    st = {}
    half = D_MODEL // 2

    def mla_gate():
        st["a_mla"] = (o_mla_ref[...].astype(F32) * _silu(proj(_M_GATE, MLA_WIDTH))).astype(BF16)

    def mla_out():
        st["y_mla"] = _dot(st["a_mla"], wmo_ref[...])

    def mla_merge(lo):
        part_scr[:, lo:lo + half] = gate(1, lo, half) * st["y_mla"][:, lo:lo + half]

    def x_query():
        st["xq"] = proj(_X_Q, X_WIDTH).astype(BF16)
        st["ox"] = []

    def x_heads(h0):
        for h in (h0, h0 + 1):
            hs = slice(X_HEAD_DIM * h, X_HEAD_DIM * (h + 1))
            s = _dot_nt(st["xq"][:, hs], kx_ref[:, hs]) * X_SCALE
            p = jnp.exp(s - jnp.max(s, axis=-1, keepdims=True))
            p = p / jnp.sum(p, axis=-1, keepdims=True)
            st["ox"].append(_dot(p.astype(BF16), vx_ref[:, hs]))

    def x_gate():
        a_x = jnp.concatenate(st["ox"], axis=-1) * _silu(proj(_X_GATE, X_WIDTH))
        st["a_x"] = a_x.astype(BF16)

    def x_out():
        st["y_x"] = _dot(st["a_x"], wxo_ref[...])

    def x_merge(lo):
        part_scr[:, lo:lo + half] += gate(2, lo, half) * st["y_x"][:, lo:lo + half]

    def conv_gate():
        cgate_scr[...] = _silu(proj(_C_GATE, CONV_WIDTH))

    def conv_merge_gate(lo):
        g0_scr[:, lo:lo + half] = gate(0, lo, half)

    side = [mla_gate, mla_out, functools.partial(mla_merge, 0), functools.partial(mla_merge, half),
            x_query, functools.partial(x_heads, 0), functools.partial(x_heads, 2), x_gate, x_out,
            functools.partial(x_merge, 0), functools.partial(x_merge, half), conv_gate,
            functools.partial(conv_merge_gate, 0), functools.partial(conv_merge_gate, half)]
    n_chunks = ts // CONV_ROWS
    for i in range(n_chunks):
        conv_chunk(i)
        for j, piece in enumerate(side):
            if j * n_chunks // len(side) == i:
                piece()
    ubuf[0:HALO, :] = ubuf[ts:ts + HALO, :]

    conv = conv_scr[...]
    mu = jnp.mean(conv, axis=-1, keepdims=True)
    cen = conv - mu
    var = jnp.mean(cen * cen, axis=-1, keepdims=True)
    ln = cen * lax.rsqrt(var + EPS) * lng_ref[...] + lnb_ref[...]
    a_conv = _silu(ln) * cgate_scr[...]
    merged = g0_scr[...] * _dot(a_conv.astype(BF16), wco_ref[...]) + part_scr[...]
    y = x_ref[...] + _dot(merged.astype(BF16), wout_ref[...])
    out_ref[...] = _rms(y, fg_ref[...])


def _block(x, o_mla, kx, vx, ng, wb, bg, cw, cb, lng, lnb, wco, wmo, wxo, wout, fg):
    b, s, d = x.shape
    ts = TS_BLK
    const = lambda shape: pl.BlockSpec(shape, lambda i, j: (0,) * len(shape),
                                       pipeline_mode=pl.Buffered(1))
    return pl.pallas_call(
        _block_kernel,
        grid=(b, s // ts),
        in_specs=[pl.BlockSpec((None, ts, d), lambda i, j: (i, j, 0)),
                  pl.BlockSpec((None, ts, MLA_WIDTH), lambda i, j: (i, j, 0)),
                  pl.BlockSpec((None, N_MEM, X_WIDTH), lambda i, j: (i, 0, 0)),
                  pl.BlockSpec((None, N_MEM, X_WIDTH), lambda i, j: (i, 0, 0)),
                  const((1, d)), const(wb.shape), const(bg.shape), const(cw.shape),
                  const((1, CONV_WIDTH)), const((1, CONV_WIDTH)), const((1, CONV_WIDTH)),
                  const(wco.shape), const(wmo.shape), const(wxo.shape), const(wout.shape),
                  const((1, d))],
        out_specs=pl.BlockSpec((None, ts, d), lambda i, j: (i, j, 0)),
        out_shape=jax.ShapeDtypeStruct((b, s, d), F32),
        scratch_shapes=[pltpu.VMEM((ts, d), BF16),
                        pltpu.VMEM((HALO + ts, CONV_WIDTH), F32),
                        pltpu.VMEM((SUBLANES - 1, SHIFT_ROWS, CONV_WIDTH), F32),
                        pltpu.VMEM((ts, CONV_WIDTH), F32),
                        pltpu.VMEM((ts, d), F32),
                        pltpu.VMEM((ts, CONV_WIDTH), F32),
                        pltpu.VMEM((ts, d), F32)],
        compiler_params=pltpu.CompilerParams(dimension_semantics=("arbitrary", "arbitrary"),
                                             vmem_limit_bytes=VMEM_LIMIT),
        name="block",
    )(x, o_mla, kx, vx, ng, wb, bg, cw, cb, lng, lnb, wco, wmo, wxo, wout, fg)


def _inv_freq_lanes():
    inv = ROPE_THETA ** (-np.arange(0, QK_ROPE, 2, dtype=np.float32) / QK_ROPE)
    lanes = np.zeros((1, LANES), np.float32)
    lanes[0, ROPE_LO:ROPE_LO + ROPE_HALF] = inv
    lanes[0, ROPE_LO + ROPE_HALF:ROPE_HI] = inv
    return jnp.asarray(lanes)


def kernel(x, mem, positions, norm_g, w_in, b_gate, conv_w, conv_b, conv_ln_g, conv_ln_b,
           w_conv_o, q_norm_g, w_uq, kv_norm_g, w_ukv, w_mla_o, mem_norm_g, w_mem_kv, w_x_o,
           w_out, final_norm_g):
    assert norm_g.shape[0] == 1, "single-layer trunk"
    b, s, d = x.shape
    row = lambda v: v.reshape(1, -1)
    edges = np.cumsum((0,) + IN_SPLITS)
    (w_cval, w_cglu, w_cgate, w_qd, w_kvd, w_kr, w_mg, w_xq, w_xg, w_g) = (
        w_in[0][:, edges[i]:edges[i + 1]] for i in range(len(IN_SPLITS)))

    w_kr_pad = jnp.pad(w_kr, ((0, 0), (ROPE_LO, LANES - ROPE_HI)))
    wa = jnp.concatenate([w_qd, w_kvd, w_kr_pad], axis=1).astype(BF16)
    wb = jnp.concatenate([w_cval, w_cglu, w_cgate, w_mg, w_xq, w_xg, w_g], axis=1).astype(BF16)
    wuq = jnp.pad(w_uq[0].reshape(Q_LORA, MLA_HEADS, QK_NOPE + QK_ROPE),
                  ((0, 0), (0, 0), (0, HEAD_PAD - QK_NOPE - QK_ROPE)))
    wuq = wuq.reshape(Q_LORA, MLA_HEADS * HEAD_PAD).astype(BF16)
    wukv = w_ukv[0].reshape(KV_LORA, MLA_HEADS, QK_NOPE + V_HEAD)
    wuk = jnp.pad(wukv[:, :, :QK_NOPE], ((0, 0), (0, 0), (0, HEAD_PAD - QK_NOPE)))
    wuk = wuk.reshape(KV_LORA, MLA_HEADS * HEAD_PAD).astype(BF16)
    wuv = wukv[:, :, QK_NOPE:].reshape(KV_LORA, MLA_WIDTH).astype(BF16)

    kx, vx = _mem_kv(mem, row(mem_norm_g[0]), w_mem_kv[0].astype(BF16))
    qt, kp, vt = _qkv(x, positions.reshape(b, s, 1), row(norm_g[0]), wa, row(q_norm_g[0]), wuq,
                      row(kv_norm_g[0]), wuk, wuv, _inv_freq_lanes())
    o_mla = _mla_attn(qt, kp, vt)
    return _block(x, o_mla, kx, vx, row(norm_g[0]), wb, row(b_gate[0]),
                  jnp.repeat(conv_w[0], SUBLANES, axis=0),
                  row(conv_b[0]), row(conv_ln_g[0]), row(conv_ln_b[0]),
                  w_conv_o[0].astype(BF16), w_mla_o[0].astype(BF16), w_x_o[0].astype(BF16),
                  w_out[0].astype(BF16), row(final_norm_g))
```

```python
import functools

import jax
import jax.numpy as jnp
import numpy as np
from jax import lax
from jax.experimental import pallas as pl
from jax.experimental.pallas import tpu as pltpu

F32 = jnp.float32
BF16 = jnp.bfloat16

D_MODEL = 1024
N_MEM = 256
CONV_WIDTH = 512
CONV_KERNEL = 31
MLA_HEADS = 8
QK_NOPE = 64
QK_ROPE = 32
V_HEAD = 64
Q_LORA = 384
KV_LORA = 256
MLA_WIDTH = MLA_HEADS * V_HEAD
X_HEADS = 4
X_HEAD_DIM = 128
X_WIDTH = X_HEADS * X_HEAD_DIM
ROPE_THETA = 10000.0
EPS = 1e-6
IN_SPLITS = (CONV_WIDTH, CONV_WIDTH, CONV_WIDTH, Q_LORA, KV_LORA, QK_ROPE, MLA_WIDTH,
             X_WIDTH, X_WIDTH, 3 * D_MODEL)

LANES = 128
HEAD_PAD = LANES
BF16_ROWS = 16
V_AUG = V_HEAD + BF16_ROWS
ROPE_LO = QK_NOPE
ROPE_HALF = QK_ROPE // 2
ROPE_HI = QK_NOPE + QK_ROPE
MLA_SCALE = (QK_NOPE + QK_ROPE) ** -0.5
X_SCALE = X_HEAD_DIM ** -0.5
LOG2E = float(np.log2(np.e))
NEG = float(np.finfo(np.float32).min)

TS_QKV = 512
TQ = 512
TK = 512
TS_BLK = 512
HALO = 32
CONV_ROWS = 32
SUBLANES = 8
SHIFT_ROWS = TS_BLK + HALO - SUBLANES
VMEM_LIMIT = 56 * 1024 * 1024


def _rms(x, g):
    return x * lax.rsqrt(jnp.mean(x * x, axis=-1, keepdims=True) + EPS) * g


def _sigmoid(x):
    return 0.5 * jnp.tanh(0.5 * x) + 0.5


def _silu(x):
    h = 0.5 * x
    return h * jnp.tanh(h) + h


def _dot(a, b):
    return jnp.dot(a, b, preferred_element_type=F32)


def _dot_nt(a, b):
    return lax.dot_general(a, b, (((1,), (1,)), ((), ())), preferred_element_type=F32)


def _mem_kv_kernel(mem_ref, g_ref, w_ref, k_ref, v_ref):
    hb = _rms(mem_ref[...], g_ref[...]).astype(BF16)
    kv = _dot(hb, w_ref[...])
    k_ref[...] = kv[:, :X_WIDTH].astype(BF16)
    v_ref[...] = kv[:, X_WIDTH:].astype(BF16)


def _mem_kv(mem, g, w):
    b, m, d = mem.shape
    return pl.pallas_call(
        _mem_kv_kernel,
        grid=(b,),
        in_specs=[pl.BlockSpec((None, m, d), lambda i: (i, 0, 0)),
                  pl.BlockSpec((1, d), lambda i: (0, 0)),
                  pl.BlockSpec((d, 2 * X_WIDTH), lambda i: (0, 0))],
        out_specs=[pl.BlockSpec((None, m, X_WIDTH), lambda i: (i, 0, 0)),
                   pl.BlockSpec((None, m, X_WIDTH), lambda i: (i, 0, 0))],
        out_shape=[jax.ShapeDtypeStruct((b, m, X_WIDTH), BF16)] * 2,
        compiler_params=pltpu.CompilerParams(dimension_semantics=("arbitrary",)),
        name="mem_kv",
    )(mem, g, w)


def _qkv_kernel(x_ref, pos_ref, ng_ref, wa_ref, qg_ref, wuq_ref, kvg_ref, wuk_ref, wuv_ref,
                invf_ref, qt_ref, k_ref, vt_ref):
    hb = _rms(x_ref[...], ng_ref[...]).astype(BF16)
    za = _dot(hb, wa_ref[...])
    cq = _rms(za[:, :Q_LORA], qg_ref[...]).astype(BF16)
    ckv = _rms(za[:, Q_LORA:Q_LORA + KV_LORA], kvg_ref[...]).astype(BF16)
    kr = za[:, Q_LORA + KV_LORA:]

    ang = pos_ref[...].astype(F32) * invf_ref[...]
    cos, sin = jnp.cos(ang), jnp.sin(ang)
    lane = lax.broadcasted_iota(jnp.int32, (1, LANES), 1)
    in_rope = (lane >= ROPE_LO) & (lane < ROPE_HI)
    first_half = in_rope & (lane < ROPE_LO + ROPE_HALF)
    c_tab = jnp.where(lane < ROPE_LO, 1.0, jnp.where(in_rope, cos, 0.0))
    s_from_lo = jnp.where(in_rope & ~first_half, sin, 0.0)
    s_from_hi = jnp.where(first_half, -sin, 0.0)

    def rope(t):
        return (t * c_tab + pltpu.roll(t, ROPE_HALF, 1) * s_from_lo
                + pltpu.roll(t, LANES - ROPE_HALF, 1) * s_from_hi)

    kr_rot = rope(kr)
    q = _dot(cq, wuq_ref[...])
    kn = _dot(ckv, wuk_ref[...])
    for h in range(MLA_HEADS):
        sl = slice(HEAD_PAD * h, HEAD_PAD * (h + 1))
        qt_ref[h] = (rope(q[:, sl]) * MLA_SCALE * LOG2E).T.astype(BF16)
        k_ref[:, sl] = (kn[:, sl] + kr_rot).astype(BF16)
    vt = _dot(ckv, wuv_ref[...]).T.astype(BF16)
    ones_row = (lax.broadcasted_iota(jnp.int32, (V_AUG - V_HEAD, vt.shape[1]), 0) == 0).astype(BF16)
    for h in range(MLA_HEADS):
        vt_ref[h, 0:V_HEAD, :] = vt[V_HEAD * h:V_HEAD * (h + 1), :]
        vt_ref[h, V_HEAD:V_AUG, :] = ones_row


def _qkv(x, pos3, ng, wa, qg, wuq, kvg, wuk, wuv, invf):
    b, s, d = x.shape
    ts = TS_QKV
    const = lambda shape: pl.BlockSpec(shape, lambda i, j: (0,) * len(shape))
    hp = MLA_HEADS * HEAD_PAD
    return pl.pallas_call(
        _qkv_kernel,
        grid=(b, s // ts),
        in_specs=[pl.BlockSpec((None, ts, d), lambda i, j: (i, j, 0)),
                  pl.BlockSpec((None, ts, 1), lambda i, j: (i, j, 0)),
                  const((1, d)), const(wa.shape), const((1, Q_LORA)), const(wuq.shape),
                  const((1, KV_LORA)), const(wuk.shape), const(wuv.shape), const((1, LANES))],
        out_specs=[pl.BlockSpec((None, MLA_HEADS, HEAD_PAD, ts), lambda i, j: (i, 0, 0, j)),
                   pl.BlockSpec((None, ts, hp), lambda i, j: (i, j, 0)),
                   pl.BlockSpec((None, MLA_HEADS, V_AUG, ts), lambda i, j: (i, 0, 0, j))],
        out_shape=[jax.ShapeDtypeStruct((b, MLA_HEADS, HEAD_PAD, s), BF16),
                   jax.ShapeDtypeStruct((b, s, hp), BF16),
                   jax.ShapeDtypeStruct((b, MLA_HEADS, V_AUG, s), BF16)],
        compiler_params=pltpu.CompilerParams(dimension_semantics=("arbitrary", "arbitrary"),
                                             vmem_limit_bytes=VMEM_LIMIT),
        name="qkv",
    )(x, pos3, ng, wa, qg, wuq, kvg, wuk, wuv, invf)


ATTN_UNROLL = 4


def _attn_schedule(n_tiles):
    below = [(qi, j) for qi in range(n_tiles) for j in range(qi)]
    diag = [(qi, qi) for qi in range(n_tiles)]
    assert len(below) % ATTN_UNROLL == 0 and len(diag) % ATTN_UNROLL == 0 and ATTN_UNROLL % 2 == 0
    table = np.array(below + diag + [(0, 0)], np.int32).T
    return table, len(below), len(diag)


def _attn_kernel(n_below, n_diag, tbl_ref, qt_ref, k_ref, vt_ref, o_ref, s_scr, m_scr, acc_scr):
    heads = (0, 1)

    def scores(f, slot):
        q0 = pl.multiple_of(tbl_ref[0, f] * TQ, TQ)
        k0 = pl.multiple_of(tbl_ref[1, f] * TK, TK)
        for hh in heads:
            k = k_ref[pl.ds(k0, TK), HEAD_PAD * hh:HEAD_PAD * (hh + 1)]
            s_scr[slot, hh] = _dot(k, qt_ref[hh, :, pl.ds(q0, TQ)])

    def absorb(f, slot, masked):
        qi = tbl_ref[0, f]
        k0 = pl.multiple_of(tbl_ref[1, f] * TK, TK)
        for hh in heads:
            s = s_scr[slot, hh]
            if masked:
                key = lax.broadcasted_iota(jnp.int32, (TK, TQ), 0)
                qry = lax.broadcasted_iota(jnp.int32, (TK, TQ), 1)
                s = jnp.where(key <= qry, s, NEG)
            m_old = m_scr[qi, hh]
            m_new = jnp.maximum(m_old, jnp.max(s, axis=0, keepdims=True))
            alpha = jnp.exp2(m_old - m_new)
            p = jnp.exp2((s - m_new).astype(BF16))
            m_scr[qi, hh] = m_new
            vt = vt_ref[hh, :, pl.ds(k0, TK)]
            acc_scr[qi, hh] = alpha * acc_scr[qi, hh] + _dot(vt, p)

    def trips(first, count, masked):
        @pl.loop(0, count // ATTN_UNROLL)
        def _(t):
            f = first + t * ATTN_UNROLL
            for u in range(ATTN_UNROLL):
                scores(f + u + 1, (u + 1) % 2)
                absorb(f + u, u % 2, masked)

    m_scr[...] = jnp.full(m_scr.shape, NEG, F32)
    acc_scr[...] = jnp.zeros(acc_scr.shape, F32)
    scores(0, 0)
    trips(0, n_below, masked=False)
    trips(n_below, n_diag, masked=True)

    @pl.loop(0, n_diag)
    def _(qi):
        ot = jnp.concatenate([acc_scr[qi, hh, 0:V_HEAD] / acc_scr[qi, hh, V_HEAD:V_HEAD + 1]
                              for hh in heads], axis=0)
        o_ref[pl.ds(pl.multiple_of(qi * TQ, TQ), TQ), :] = ot.T.astype(BF16)


def _mla_attn(qt, kp, vt):
    b, s, _ = kp.shape
    pairs = MLA_HEADS // 2
    n_tiles = s // TQ
    table, n_below, n_diag = _attn_schedule(n_tiles)
    grid_spec = pltpu.PrefetchScalarGridSpec(
        num_scalar_prefetch=1,
        grid=(b, pairs),
        in_specs=[pl.BlockSpec((None, 2, HEAD_PAD, s), lambda i, p, tbl: (i, p, 0, 0)),
                  pl.BlockSpec((None, s, 2 * HEAD_PAD), lambda i, p, tbl: (i, 0, p)),
                  pl.BlockSpec((None, 2, V_AUG, s), lambda i, p, tbl: (i, p, 0, 0))],
        out_specs=pl.BlockSpec((None, s, LANES), lambda i, p, tbl: (i, 0, p)),
        scratch_shapes=[pltpu.VMEM((2, 2, TK, TQ), F32),
                        pltpu.VMEM((n_tiles, 2, 1, TQ), F32),
                        pltpu.VMEM((n_tiles, 2, V_AUG, TQ), F32)])
    return pl.pallas_call(
        functools.partial(_attn_kernel, n_below, n_diag),
        grid_spec=grid_spec,
        out_shape=jax.ShapeDtypeStruct((b, s, MLA_WIDTH), BF16),
        compiler_params=pltpu.CompilerParams(dimension_semantics=("arbitrary", "arbitrary"),
                                             vmem_limit_bytes=VMEM_LIMIT),
        name="mla_attn",
    )(jnp.asarray(table), qt, kp, vt)


_C_VAL, _C_GLU, _C_GATE, _M_GATE, _X_Q, _X_GATE, _G0 = (
    0, CONV_WIDTH, 2 * CONV_WIDTH, 3 * CONV_WIDTH, 3 * CONV_WIDTH + MLA_WIDTH,
    3 * CONV_WIDTH + MLA_WIDTH + X_WIDTH, 3 * CONV_WIDTH + MLA_WIDTH + 2 * X_WIDTH)
W_B_COLS = _G0 + 3 * D_MODEL


def _block_kernel(x_ref, o_mla_ref, kx_ref, vx_ref, ng_ref, wb_ref, bg_ref, cw_ref, cb_ref,
                  lng_ref, lnb_ref, wco_ref, wmo_ref, wxo_ref, wout_ref, fg_ref,
                  out_ref, hb_scr, ubuf, ush, conv_scr, part_scr, cgate_scr, g0_scr):
    ts = TS_BLK
    hb_scr[...] = _rms(x_ref[...], ng_ref[...]).astype(BF16)

    def proj(lo, width):
        return _dot(hb_scr[...], wb_ref[:, lo:lo + width])

    def gate(i, lo, width):
        c0 = i * D_MODEL + lo
        return _sigmoid(proj(_G0 + c0, width) + bg_ref[:, c0:c0 + width])

    @pl.when(pl.program_id(1) == 0)
    def _():
        ubuf[0:HALO, :] = jnp.zeros((HALO, CONV_WIDTH), F32)

    ubuf[HALO:HALO + ts, :] = proj(_C_VAL, CONV_WIDTH) * _sigmoid(proj(_C_GLU, CONV_WIDTH))
    for r in range(1, SUBLANES):
        ush[r - 1] = ubuf[r:r + SHIFT_ROWS, :]
    first = HALO - (CONV_KERNEL - 1)

    def conv_chunk(i):
        r0 = i * CONV_ROWS
        groups = CONV_ROWS // SUBLANES
        acc = jnp.zeros((groups, SUBLANES, CONV_WIDTH), F32) + cb_ref[...]
        for kk in range(CONV_KERNEL):
            r = (first + kk) % SUBLANES
            row = r0 + first + kk - r
            tap = ubuf[row:row + CONV_ROWS, :] if r == 0 else ush[r - 1, row:row + CONV_ROWS, :]
            w = cw_ref[kk * SUBLANES:(kk + 1) * SUBLANES, :]
            acc = acc + tap.reshape(groups, SUBLANES, CONV_WIDTH) * w
        conv_scr[r0:r0 + CONV_ROWS, :] = acc.reshape(CONV_ROWS, CONV_WIDTH)

    st = {}
    half = D_MODEL // 2

    def mla_gate():
        st["a_mla"] = (o_mla_ref[...].astype(F32) * _silu(proj(_M_GATE, MLA_WIDTH))).astype(BF16)

    def mla_out():
        st["y_mla"] = _dot(st["a_mla"], wmo_ref[...])

    def mla_merge(lo):
        part_scr[:, lo:lo + half] = gate(1, lo, half) * st["y_mla"][:, lo:lo + half]

    def x_query():
        st["xq"] = proj(_X_Q, X_WIDTH).astype(BF16)
        st["ox"] = []

    def x_heads(h0):
        for h in (h0, h0 + 1):
            hs = slice(X_HEAD_DIM * h, X_HEAD_DIM * (h + 1))
            s = _dot_nt(st["xq"][:, hs], kx_ref[:, hs]) * X_SCALE
            p = jnp.exp(s - jnp.max(s, axis=-1, keepdims=True))
            p = p / jnp.sum(p, axis=-1, keepdims=True)
            st["ox"].append(_dot(p.astype(BF16), vx_ref[:, hs]))

    def x_gate():
        a_x = jnp.concatenate(st["ox"], axis=-1) * _silu(proj(_X_GATE, X_WIDTH))
        st["a_x"] = a_x.astype(BF16)

    def x_out():
        st["y_x"] = _dot(st["a_x"], wxo_ref[...])

    def x_merge(lo):
        part_scr[:, lo:lo + half] += gate(2, lo, half) * st["y_x"][:, lo:lo + half]

    def conv_gate():
        cgate_scr[...] = _silu(proj(_C_GATE, CONV_WIDTH))

    def conv_merge_gate(lo):
        g0_scr[:, lo:lo + half] = gate(0, lo, half)

    side = [mla_gate, mla_out, functools.partial(mla_merge, 0), functools.partial(mla_merge, half),
            x_query, functools.partial(x_heads, 0), functools.partial(x_heads, 2), x_gate, x_out,
            functools.partial(x_merge, 0), functools.partial(x_merge, half), conv_gate,
            functools.partial(conv_merge_gate, 0), functools.partial(conv_merge_gate, half)]
    n_chunks = ts // CONV_ROWS
    for i in range(n_chunks):
        conv_chunk(i)
        for j, piece in enumerate(side):
            if j * n_chunks // len(side) == i:
                piece()
    ubuf[0:HALO, :] = ubuf[ts:ts + HALO, :]

    conv = conv_scr[...]
    mu = jnp.mean(conv, axis=-1, keepdims=True)
    cen = conv - mu
    var = jnp.mean(cen * cen, axis=-1, keepdims=True)
    ln = cen * lax.rsqrt(var + EPS) * lng_ref[...] + lnb_ref[...]
    a_conv = _silu(ln) * cgate_scr[...]
    merged = g0_scr[...] * _dot(a_conv.astype(BF16), wco_ref[...]) + part_scr[...]
    y = x_ref[...] + _dot(merged.astype(BF16), wout_ref[...])
    out_ref[...] = _rms(y, fg_ref[...])


def _block(x, o_mla, kx, vx, ng, wb, bg, cw, cb, lng, lnb, wco, wmo, wxo, wout, fg):
    b, s, d = x.shape
    ts = TS_BLK
    const = lambda shape: pl.BlockSpec(shape, lambda i, j: (0,) * len(shape),
                                       pipeline_mode=pl.Buffered(1))
    return pl.pallas_call(
        _block_kernel,
        grid=(b, s // ts),
        in_specs=[pl.BlockSpec((None, ts, d), lambda i, j: (i, j, 0)),
                  pl.BlockSpec((None, ts, MLA_WIDTH), lambda i, j: (i, j, 0)),
                  pl.BlockSpec((None, N_MEM, X_WIDTH), lambda i, j: (i, 0, 0)),
                  pl.BlockSpec((None, N_MEM, X_WIDTH), lambda i, j: (i, 0, 0)),
                  const((1, d)), const(wb.shape), const(bg.shape), const(cw.shape),
                  const((1, CONV_WIDTH)), const((1, CONV_WIDTH)), const((1, CONV_WIDTH)),
                  const(wco.shape), const(wmo.shape), const(wxo.shape), const(wout.shape),
                  const((1, d))],
        out_specs=pl.BlockSpec((None, ts, d), lambda i, j: (i, j, 0)),
        out_shape=jax.ShapeDtypeStruct((b, s, d), F32),
        scratch_shapes=[pltpu.VMEM((ts, d), BF16),
                        pltpu.VMEM((HALO + ts, CONV_WIDTH), F32),
                        pltpu.VMEM((SUBLANES - 1, SHIFT_ROWS, CONV_WIDTH), F32),
                        pltpu.VMEM((ts, CONV_WIDTH), F32),
                        pltpu.VMEM((ts, d), F32),
                        pltpu.VMEM((ts, CONV_WIDTH), F32),
                        pltpu.VMEM((ts, d), F32)],
        compiler_params=pltpu.CompilerParams(dimension_semantics=("arbitrary", "arbitrary"),
                                             vmem_limit_bytes=VMEM_LIMIT),
        name="block",
    )(x, o_mla, kx, vx, ng, wb, bg, cw, cb, lng, lnb, wco, wmo, wxo, wout, fg)


def _inv_freq_lanes():
    inv = ROPE_THETA ** (-np.arange(0, QK_ROPE, 2, dtype=np.float32) / QK_ROPE)
    lanes = np.zeros((1, LANES), np.float32)
    lanes[0, ROPE_LO:ROPE_LO + ROPE_HALF] = inv
    lanes[0, ROPE_LO + ROPE_HALF:ROPE_HI] = inv
    return jnp.asarray(lanes)


def kernel(x, mem, positions, norm_g, w_in, b_gate, conv_w, conv_b, conv_ln_g, conv_ln_b,
           w_conv_o, q_norm_g, w_uq, kv_norm_g, w_ukv, w_mla_o, mem_norm_g, w_mem_kv, w_x_o,
           w_out, final_norm_g):
    assert norm_g.shape[0] == 1, "single-layer trunk"
    b, s, d = x.shape
    row = lambda v: v.reshape(1, -1)
    edges = np.cumsum((0,) + IN_SPLITS)
    (w_cval, w_cglu, w_cgate, w_qd, w_kvd, w_kr, w_mg, w_xq, w_xg, w_g) = (
        w_in[0][:, edges[i]:edges[i + 1]] for i in range(len(IN_SPLITS)))

    w_kr_pad = jnp.pad(w_kr, ((0, 0), (ROPE_LO, LANES - ROPE_HI)))
    wa = jnp.concatenate([w_qd, w_kvd, w_kr_pad], axis=1).astype(BF16)
    wb = jnp.concatenate([w_cval, w_cglu, w_cgate, w_mg, w_xq, w_xg, w_g], axis=1).astype(BF16)
    wuq = jnp.pad(w_uq[0].reshape(Q_LORA, MLA_HEADS, QK_NOPE + QK_ROPE),
                  ((0, 0), (0, 0), (0, HEAD_PAD - QK_NOPE - QK_ROPE)))
    wuq = wuq.reshape(Q_LORA, MLA_HEADS * HEAD_PAD).astype(BF16)
    wukv = w_ukv[0].reshape(KV_LORA, MLA_HEADS, QK_NOPE + V_HEAD)
    wuk = jnp.pad(wukv[:, :, :QK_NOPE], ((0, 0), (0, 0), (0, HEAD_PAD - QK_NOPE)))
    wuk = wuk.reshape(KV_LORA, MLA_HEADS * HEAD_PAD).astype(BF16)
    wuv = wukv[:, :, QK_NOPE:].reshape(KV_LORA, MLA_WIDTH).astype(BF16)

    kx, vx = _mem_kv(mem, row(mem_norm_g[0]), w_mem_kv[0].astype(BF16))
    qt, kp, vt = _qkv(x, positions.reshape(b, s, 1), row(norm_g[0]), wa, row(q_norm_g[0]), wuq,
                      row(kv_norm_g[0]), wuk, wuv, _inv_freq_lanes())
    o_mla = _mla_attn(qt, kp, vt)
    return _block(x, o_mla, kx, vx, row(norm_g[0]), wb, row(b_gate[0]),
                  jnp.repeat(conv_w[0], SUBLANES, axis=0),
                  row(conv_b[0]), row(conv_ln_g[0]), row(conv_ln_b[0]),
                  w_conv_o[0].astype(BF16), w_mla_o[0].astype(BF16), w_x_o[0].astype(BF16),
                  w_out[0].astype(BF16), row(final_norm_g))
```

```python
import functools

import jax
import jax.numpy as jnp
import numpy as np
from jax import lax
from jax.experimental import pallas as pl
from jax.experimental.pallas import tpu as pltpu

F32 = jnp.float32
BF16 = jnp.bfloat16

D_MODEL = 1024
N_MEM = 256
CONV_WIDTH = 512
CONV_KERNEL = 31
MLA_HEADS = 8
QK_NOPE = 64
QK_ROPE = 32
V_HEAD = 64
Q_LORA = 384
KV_LORA = 256
MLA_WIDTH = MLA_HEADS * V_HEAD
X_HEADS = 4
X_HEAD_DIM = 128
X_WIDTH = X_HEADS * X_HEAD_DIM
ROPE_THETA = 10000.0
EPS = 1e-6
IN_SPLITS = (CONV_WIDTH, CONV_WIDTH, CONV_WIDTH, Q_LORA, KV_LORA, QK_ROPE, MLA_WIDTH,
             X_WIDTH, X_WIDTH, 3 * D_MODEL)

LANES = 128
HEAD_PAD = LANES
BF16_ROWS = 16
V_AUG = V_HEAD + BF16_ROWS
ROPE_LO = QK_NOPE
ROPE_HALF = QK_ROPE // 2
ROPE_HI = QK_NOPE + QK_ROPE
MLA_SCALE = (QK_NOPE + QK_ROPE) ** -0.5
X_SCALE = X_HEAD_DIM ** -0.5
LOG2E = float(np.log2(np.e))
NEG = float(np.finfo(np.float32).min)

TS_QKV = 512
TQ = 512
TK = 512
TS_BLK = 512
HALO = 32
CONV_ROWS = 32
SUBLANES = 8
SHIFT_ROWS = TS_BLK + HALO - SUBLANES
VMEM_LIMIT = 56 * 1024 * 1024


def _rms(x, g):
    return x * lax.rsqrt(jnp.mean(x * x, axis=-1, keepdims=True) + EPS) * g


def _sigmoid(x):
    return 0.5 * jnp.tanh(0.5 * x) + 0.5


def _silu(x):
    h = 0.5 * x
    return h * jnp.tanh(h) + h


def _dot(a, b):
    return jnp.dot(a, b, preferred_element_type=F32)


def _dot_nt(a, b):
    return lax.dot_general(a, b, (((1,), (1,)), ((), ())), preferred_element_type=F32)


def _mem_kv_kernel(mem_ref, g_ref, w_ref, k_ref, v_ref):
    hb = _rms(mem_ref[...], g_ref[...]).astype(BF16)
    kv = _dot(hb, w_ref[...])
    k_ref[...] = kv[:, :X_WIDTH].astype(BF16)
    v_ref[...] = kv[:, X_WIDTH:].astype(BF16)


def _mem_kv(mem, g, w):
    b, m, d = mem.shape
    return pl.pallas_call(
        _mem_kv_kernel,
        grid=(b,),
        in_specs=[pl.BlockSpec((None, m, d), lambda i: (i, 0, 0)),
                  pl.BlockSpec((1, d), lambda i: (0, 0)),
                  pl.BlockSpec((d, 2 * X_WIDTH), lambda i: (0, 0))],
        out_specs=[pl.BlockSpec((None, m, X_WIDTH), lambda i: (i, 0, 0)),
                   pl.BlockSpec((None, m, X_WIDTH), lambda i: (i, 0, 0))],
        out_shape=[jax.ShapeDtypeStruct((b, m, X_WIDTH), BF16)] * 2,
        compiler_params=pltpu.CompilerParams(dimension_semantics=("arbitrary",)),
        name="mem_kv",
    )(mem, g, w)


def _qkv_kernel(x_ref, pos_ref, ng_ref, wa_ref, qg_ref, wuq_ref, kvg_ref, wuk_ref, wuv_ref,
                invf_ref, qt_ref, k_ref, vt_ref):
    hb = _rms(x_ref[...], ng_ref[...]).astype(BF16)
    za = _dot(hb, wa_ref[...])
    cq = _rms(za[:, :Q_LORA], qg_ref[...]).astype(BF16)
    ckv = _rms(za[:, Q_LORA:Q_LORA + KV_LORA], kvg_ref[...]).astype(BF16)
    kr = za[:, Q_LORA + KV_LORA:]

    ang = invf_ref[...] * pos_ref[...].astype(F32)
    cos, sin = jnp.cos(ang), jnp.sin(ang)

    def rope_t(t):
        x1 = t[ROPE_LO:ROPE_LO + ROPE_HALF]
        x2 = t[ROPE_LO + ROPE_HALF:ROPE_HI]
        return jnp.concatenate(
            [t[:ROPE_LO], x1 * cos - x2 * sin, x1 * sin + x2 * cos, t[ROPE_HI:]], axis=0)

    kr_rot = rope_t(kr.T).T
    q = _dot(cq, wuq_ref[...])
    kn = _dot(ckv, wuk_ref[...])
    for h in range(MLA_HEADS):
        sl = slice(HEAD_PAD * h, HEAD_PAD * (h + 1))
        qt_ref[h] = (rope_t(q[:, sl].T) * (MLA_SCALE * LOG2E)).astype(BF16)
        k_ref[:, sl] = (kn[:, sl] + kr_rot).astype(BF16)
    vt = _dot(ckv, wuv_ref[...]).T.astype(BF16)
    ones_row = (lax.broadcasted_iota(jnp.int32, (V_AUG - V_HEAD, vt.shape[1]), 0) == 0).astype(BF16)
    for h in range(MLA_HEADS):
        vt_ref[h, 0:V_HEAD, :] = vt[V_HEAD * h:V_HEAD * (h + 1), :]
        vt_ref[h, V_HEAD:V_AUG, :] = ones_row


def _qkv(x, pos3, ng, wa, qg, wuq, kvg, wuk, wuv, invf):
    b, s, d = x.shape
    ts = TS_QKV
    const = lambda shape: pl.BlockSpec(shape, lambda i, j: (0,) * len(shape))
    hp = MLA_HEADS * HEAD_PAD
    return pl.pallas_call(
        _qkv_kernel,
        grid=(b, s // ts),
        in_specs=[pl.BlockSpec((None, ts, d), lambda i, j: (i, j, 0)),
                  pl.BlockSpec((None, 1, ts), lambda i, j: (i, 0, j)),
                  const((1, d)), const(wa.shape), const((1, Q_LORA)), const(wuq.shape),
                  const((1, KV_LORA)), const(wuk.shape), const(wuv.shape), const((ROPE_HALF, 1))],
        out_specs=[pl.BlockSpec((None, MLA_HEADS, HEAD_PAD, ts), lambda i, j: (i, 0, 0, j)),
                   pl.BlockSpec((None, ts, hp), lambda i, j: (i, j, 0)),
                   pl.BlockSpec((None, MLA_HEADS, V_AUG, ts), lambda i, j: (i, 0, 0, j))],
        out_shape=[jax.ShapeDtypeStruct((b, MLA_HEADS, HEAD_PAD, s), BF16),
                   jax.ShapeDtypeStruct((b, s, hp), BF16),
                   jax.ShapeDtypeStruct((b, MLA_HEADS, V_AUG, s), BF16)],
        compiler_params=pltpu.CompilerParams(dimension_semantics=("arbitrary", "arbitrary"),
                                             vmem_limit_bytes=VMEM_LIMIT),
        name="qkv",
    )(x, pos3, ng, wa, qg, wuq, kvg, wuk, wuv, invf)


ATTN_UNROLL = 4


def _attn_schedule(n_tiles):
    below = [(qi, j) for qi in range(n_tiles) for j in range(qi)]
    diag = [(qi, qi) for qi in range(n_tiles)]
    assert len(below) % ATTN_UNROLL == 0 and len(diag) % ATTN_UNROLL == 0 and ATTN_UNROLL % 2 == 0
    table = np.array(below + diag + [(0, 0)], np.int32).T
    return table, len(below), len(diag)


def _attn_kernel(n_below, n_diag, tbl_ref, qt_ref, k_ref, vt_ref, o_ref, s_scr, m_scr, acc_scr):
    heads = (0, 1)

    def scores(f, slot):
        q0 = pl.multiple_of(tbl_ref[0, f] * TQ, TQ)
        k0 = pl.multiple_of(tbl_ref[1, f] * TK, TK)
        for hh in heads:
            k = k_ref[pl.ds(k0, TK), HEAD_PAD * hh:HEAD_PAD * (hh + 1)]
            s_scr[slot, hh] = _dot(k, qt_ref[hh, :, pl.ds(q0, TQ)])

    def absorb(f, slot, masked):
        qi = tbl_ref[0, f]
        k0 = pl.multiple_of(tbl_ref[1, f] * TK, TK)
        for hh in heads:
            s = s_scr[slot, hh]
            if masked:
                key = lax.broadcasted_iota(jnp.int32, (TK, TQ), 0)
                qry = lax.broadcasted_iota(jnp.int32, (TK, TQ), 1)
                s = jnp.where(key <= qry, s, NEG)
            m_old = m_scr[qi, hh]
            m_new = jnp.maximum(m_old, jnp.max(s, axis=0, keepdims=True))
            alpha = jnp.exp2(m_old - m_new)
            p = jnp.exp2((s - m_new).astype(BF16))
            m_scr[qi, hh] = m_new
            vt = vt_ref[hh, :, pl.ds(k0, TK)]
            acc_scr[qi, hh] = alpha * acc_scr[qi, hh] + _dot(vt, p)

    def trips(first, count, masked):
        @pl.loop(0, count // ATTN_UNROLL)
        def _(t):
            f = first + t * ATTN_UNROLL
            for u in range(ATTN_UNROLL):
                scores(f + u + 1, (u + 1) % 2)
                absorb(f + u, u % 2, masked)

    m_scr[...] = jnp.full(m_scr.shape, NEG, F32)
    acc_scr[...] = jnp.zeros(acc_scr.shape, F32)
    scores(0, 0)
    trips(0, n_below, masked=False)
    trips(n_below, n_diag, masked=True)

    @pl.loop(0, n_diag)
    def _(qi):
        ot = jnp.concatenate([acc_scr[qi, hh, 0:V_HEAD] / acc_scr[qi, hh, V_HEAD:V_HEAD + 1]
                              for hh in heads], axis=0)
        o_ref[pl.ds(pl.multiple_of(qi * TQ, TQ), TQ), :] = ot.T.astype(BF16)


def _mla_attn(qt, kp, vt):
    b, s, _ = kp.shape
    pairs = MLA_HEADS // 2
    n_tiles = s // TQ
    table, n_below, n_diag = _attn_schedule(n_tiles)
    grid_spec = pltpu.PrefetchScalarGridSpec(
        num_scalar_prefetch=1,
        grid=(b, pairs),
        in_specs=[pl.BlockSpec((None, 2, HEAD_PAD, s), lambda i, p, tbl: (i, p, 0, 0)),
                  pl.BlockSpec((None, s, 2 * HEAD_PAD), lambda i, p, tbl: (i, 0, p)),
                  pl.BlockSpec((None, 2, V_AUG, s), lambda i, p, tbl: (i, p, 0, 0))],
        out_specs=pl.BlockSpec((None, s, LANES), lambda i, p, tbl: (i, 0, p)),
        scratch_shapes=[pltpu.VMEM((2, 2, TK, TQ), F32),
                        pltpu.VMEM((n_tiles, 2, 1, TQ), F32),
                        pltpu.VMEM((n_tiles, 2, V_AUG, TQ), F32)])
    return pl.pallas_call(
        functools.partial(_attn_kernel, n_below, n_diag),
        grid_spec=grid_spec,
        out_shape=jax.ShapeDtypeStruct((b, s, MLA_WIDTH), BF16),
        compiler_params=pltpu.CompilerParams(dimension_semantics=("arbitrary", "arbitrary"),
                                             vmem_limit_bytes=VMEM_LIMIT),
        name="mla_attn",
    )(jnp.asarray(table), qt, kp, vt)


_C_VAL, _C_GLU, _C_GATE, _M_GATE, _X_Q, _X_GATE, _G0 = (
    0, CONV_WIDTH, 2 * CONV_WIDTH, 3 * CONV_WIDTH, 3 * CONV_WIDTH + MLA_WIDTH,
    3 * CONV_WIDTH + MLA_WIDTH + X_WIDTH, 3 * CONV_WIDTH + MLA_WIDTH + 2 * X_WIDTH)
W_B_COLS = _G0 + 3 * D_MODEL


def _block_kernel(x_ref, o_mla_ref, kx_ref, vx_ref, ng_ref, wb_ref, bg_ref, cw_ref, cb_ref,
                  lng_ref, lnb_ref, wco_ref, wmo_ref, wxo_ref, wout_ref, fg_ref,
                  out_ref, hb_scr, ubuf, ush, conv_scr, part_scr, cgate_scr, g0_scr):
    ts = TS_BLK
    hb_scr[...] = _rms(x_ref[...], ng_ref[...]).astype(BF16)

    def proj(lo, width):
        return _dot(hb_scr[...], wb_ref[:, lo:lo + width])

    def gate(i, lo, width):
        c0 = i * D_MODEL + lo
        return _sigmoid(proj(_G0 + c0, width) + bg_ref[:, c0:c0 + width])

    @pl.when(pl.program_id(1) == 0)
    def _():
        ubuf[0:HALO, :] = jnp.zeros((HALO, CONV_WIDTH), F32)

    ubuf[HALO:HALO + ts, :] = proj(_C_VAL, CONV_WIDTH) * _sigmoid(proj(_C_GLU, CONV_WIDTH))
    for r in range(1, SUBLANES):
        ush[r - 1] = ubuf[r:r + SHIFT_ROWS, :]
    first = HALO - (CONV_KERNEL - 1)

    def conv_chunk(i):
        r0 = i * CONV_ROWS
        groups = CONV_ROWS // SUBLANES
        acc = jnp.zeros((groups, SUBLANES, CONV_WIDTH), F32) + cb_ref[...]
        for kk in range(CONV_KERNEL):
            r = (first + kk) % SUBLANES
            row = r0 + first + kk - r
            tap = ubuf[row:row + CONV_ROWS, :] if r == 0 else ush[r - 1, row:row + CONV_ROWS, :]
            w = cw_ref[kk * SUBLANES:(kk + 1) * SUBLANES, :]
            acc = acc + tap.reshape(groups, SUBLANES, CONV_WIDTH) * w
        conv_scr[r0:r0 + CONV_ROWS, :] = acc.reshape(CONV_ROWS, CONV_WIDTH)

    st = {}
    half = D_MODEL // 2

    def mla_gate():
        st["a_mla"] = (o_mla_ref[...].astype(F32) * _silu(proj(_M_GATE, MLA_WIDTH))).astype(BF16)

    def mla_out():
        st["y_mla"] = _dot(st["a_mla"], wmo_ref[...])

    def mla_merge(lo):
        part_scr[:, lo:lo + half] = gate(1, lo, half) * st["y_mla"][:, lo:lo + half]

    def x_query():
        st["xq"] = proj(_X_Q, X_WIDTH).astype(BF16)
        st["ox"] = []

    def x_heads(h0):
        for h in (h0, h0 + 1):
            hs = slice(X_HEAD_DIM * h, X_HEAD_DIM * (h + 1))
            s = _dot_nt(st["xq"][:, hs], kx_ref[:, hs]) * X_SCALE
            p = jnp.exp(s - jnp.max(s, axis=-1, keepdims=True))
            p = p / jnp.sum(p, axis=-1, keepdims=True)
            st["ox"].append(_dot(p.astype(BF16), vx_ref[:, hs]))

    def x_gate():
        a_x = jnp.concatenate(st["ox"], axis=-1) * _silu(proj(_X_GATE, X_WIDTH))
        st["a_x"] = a_x.astype(BF16)

    def x_out():
        st["y_x"] = _dot(st["a_x"], wxo_ref[...])

    def x_merge(lo):
        part_scr[:, lo:lo + half] += gate(2, lo, half) * st["y_x"][:, lo:lo + half]

    def conv_gate():
        cgate_scr[...] = _silu(proj(_C_GATE, CONV_WIDTH))

    def conv_merge_gate(lo):
        g0_scr[:, lo:lo + half] = gate(0, lo, half)

    side = [mla_gate, mla_out, functools.partial(mla_merge, 0), functools.partial(mla_merge, half),
            x_query, functools.partial(x_heads, 0), functools.partial(x_heads, 2), x_gate, x_out,
            functools.partial(x_merge, 0), functools.partial(x_merge, half), conv_gate,
            functools.partial(conv_merge_gate, 0), functools.partial(conv_merge_gate, half)]
    n_chunks = ts // CONV_ROWS
    for i in range(n_chunks):
        conv_chunk(i)
        for j, piece in enumerate(side):
            if j * n_chunks // len(side) == i:
                piece()
    ubuf[0:HALO, :] = ubuf[ts:ts + HALO, :]

    conv = conv_scr[...]
    mu = jnp.mean(conv, axis=-1, keepdims=True)
    cen = conv - mu
    var = jnp.mean(cen * cen, axis=-1, keepdims=True)
    ln = cen * lax.rsqrt(var + EPS) * lng_ref[...] + lnb_ref[...]
    a_conv = _silu(ln) * cgate_scr[...]
    merged = g0_scr[...] * _dot(a_conv.astype(BF16), wco_ref[...]) + part_scr[...]
    y = x_ref[...] + _dot(merged.astype(BF16), wout_ref[...])
    out_ref[...] = _rms(y, fg_ref[...])


def _block(x, o_mla, kx, vx, ng, wb, bg, cw, cb, lng, lnb, wco, wmo, wxo, wout, fg):
    b, s, d = x.shape
    ts = TS_BLK
    const = lambda shape: pl.BlockSpec(shape, lambda i, j: (0,) * len(shape),
                                       pipeline_mode=pl.Buffered(1))
    return pl.pallas_call(
        _block_kernel,
        grid=(b, s // ts),
        in_specs=[pl.BlockSpec((None, ts, d), lambda i, j: (i, j, 0)),
                  pl.BlockSpec((None, ts, MLA_WIDTH), lambda i, j: (i, j, 0)),
                  pl.BlockSpec((None, N_MEM, X_WIDTH), lambda i, j: (i, 0, 0)),
                  pl.BlockSpec((None, N_MEM, X_WIDTH), lambda i, j: (i, 0, 0)),
                  const((1, d)), const(wb.shape), const(bg.shape), const(cw.shape),
                  const((1, CONV_WIDTH)), const((1, CONV_WIDTH)), const((1, CONV_WIDTH)),
                  const(wco.shape), const(wmo.shape), const(wxo.shape), const(wout.shape),
                  const((1, d))],
        out_specs=pl.BlockSpec((None, ts, d), lambda i, j: (i, j, 0)),
        out_shape=jax.ShapeDtypeStruct((b, s, d), F32),
        scratch_shapes=[pltpu.VMEM((ts, d), BF16),
                        pltpu.VMEM((HALO + ts, CONV_WIDTH), F32),
                        pltpu.VMEM((SUBLANES - 1, SHIFT_ROWS, CONV_WIDTH), F32),
                        pltpu.VMEM((ts, CONV_WIDTH), F32),
                        pltpu.VMEM((ts, d), F32),
                        pltpu.VMEM((ts, CONV_WIDTH), F32),
                        pltpu.VMEM((ts, d), F32)],
        compiler_params=pltpu.CompilerParams(dimension_semantics=("arbitrary", "arbitrary"),
                                             vmem_limit_bytes=VMEM_LIMIT),
        name="block",
    )(x, o_mla, kx, vx, ng, wb, bg, cw, cb, lng, lnb, wco, wmo, wxo, wout, fg)


def _inv_freq_column():
    inv = ROPE_THETA ** (-jnp.arange(0, QK_ROPE, 2, dtype=F32) / QK_ROPE)
    return inv.reshape(ROPE_HALF, 1)


def kernel(x, mem, positions, norm_g, w_in, b_gate, conv_w, conv_b, conv_ln_g, conv_ln_b,
           w_conv_o, q_norm_g, w_uq, kv_norm_g, w_ukv, w_mla_o, mem_norm_g, w_mem_kv, w_x_o,
           w_out, final_norm_g):
    assert norm_g.shape[0] == 1, "single-layer trunk"
    b, s, d = x.shape
    row = lambda v: v.reshape(1, -1)
    edges = np.cumsum((0,) + IN_SPLITS)
    w_in_bf = w_in[0].astype(BF16)
    (w_cval, w_cglu, w_cgate, w_qd, w_kvd, w_kr, w_mg, w_xq, w_xg, w_g) = (
        w_in_bf[:, edges[i]:edges[i + 1]] for i in range(len(IN_SPLITS)))

    w_kr_pad = jnp.pad(w_kr, ((0, 0), (ROPE_LO, LANES - ROPE_HI)))
    wa = jnp.concatenate([w_qd, w_kvd, w_kr_pad], axis=1)
    wb = jnp.concatenate([w_cval, w_cglu, w_cgate, w_mg, w_xq, w_xg, w_g], axis=1)
    wuq = jnp.pad(w_uq[0].reshape(Q_LORA, MLA_HEADS, QK_NOPE + QK_ROPE),
                  ((0, 0), (0, 0), (0, HEAD_PAD - QK_NOPE - QK_ROPE)))
    wuq = wuq.reshape(Q_LORA, MLA_HEADS * HEAD_PAD).astype(BF16)
    wukv = w_ukv[0].reshape(KV_LORA, MLA_HEADS, QK_NOPE + V_HEAD)
    wuk = jnp.pad(wukv[:, :, :QK_NOPE], ((0, 0), (0, 0), (0, HEAD_PAD - QK_NOPE)))
    wuk = wuk.reshape(KV_LORA, MLA_HEADS * HEAD_PAD).astype(BF16)
    wuv = wukv[:, :, QK_NOPE:].reshape(KV_LORA, MLA_WIDTH).astype(BF16)

    kx, vx = _mem_kv(mem, row(mem_norm_g[0]), w_mem_kv[0].astype(BF16))
    qt, kp, vt = _qkv(x, positions.reshape(b, 1, s), row(norm_g[0]), wa, row(q_norm_g[0]), wuq,
                      row(kv_norm_g[0]), wuk, wuv, _inv_freq_column())
    o_mla = _mla_attn(qt, kp, vt)
    return _block(x, o_mla, kx, vx, row(norm_g[0]), wb, row(b_gate[0]),
                  jnp.repeat(conv_w[0], SUBLANES, axis=0),
                  row(conv_b[0]), row(conv_ln_g[0]), row(conv_ln_b[0]),
                  w_conv_o[0].astype(BF16), w_mla_o[0].astype(BF16), w_x_o[0].astype(BF16),
                  w_out[0].astype(BF16), row(final_norm_g))
```

```python
import functools

import jax
import jax.numpy as jnp
import numpy as np
from jax import lax
from jax.experimental import pallas as pl
from jax.experimental.pallas import tpu as pltpu

F32 = jnp.float32
BF16 = jnp.bfloat16

D_MODEL = 1024
N_MEM = 256
CONV_WIDTH = 512
CONV_KERNEL = 31
MLA_HEADS = 8
QK_NOPE = 64
QK_ROPE = 32
V_HEAD = 64
Q_LORA = 384
KV_LORA = 256
MLA_WIDTH = MLA_HEADS * V_HEAD
X_HEADS = 4
X_HEAD_DIM = 128
X_WIDTH = X_HEADS * X_HEAD_DIM
ROPE_THETA = 10000.0
EPS = 1e-6
IN_SPLITS = (CONV_WIDTH, CONV_WIDTH, CONV_WIDTH, Q_LORA, KV_LORA, QK_ROPE, MLA_WIDTH,
             X_WIDTH, X_WIDTH, 3 * D_MODEL)

LANES = 128
HEAD_PAD = LANES
BF16_ROWS = 16
V_AUG = V_HEAD + BF16_ROWS
ROPE_LO = QK_NOPE
ROPE_HALF = QK_ROPE // 2
ROPE_HI = QK_NOPE + QK_ROPE
MLA_SCALE = (QK_NOPE + QK_ROPE) ** -0.5
X_SCALE = X_HEAD_DIM ** -0.5
LOG2E = float(np.log2(np.e))
NEG = float(np.finfo(np.float32).min)

TS_QKV = 512
TQ = 512
TK = 512
TS_BLK = 512
HALO = 32
CONV_ROWS = 32
SUBLANES = 8
SHIFT_ROWS = TS_BLK + HALO - SUBLANES
VMEM_LIMIT = 56 * 1024 * 1024


def _rms(x, g):
    return x * lax.rsqrt(jnp.mean(x * x, axis=-1, keepdims=True) + EPS) * g


def _sigmoid(x):
    return 0.5 * jnp.tanh(0.5 * x) + 0.5


def _silu(x):
    h = 0.5 * x
    return h * jnp.tanh(h) + h


def _dot(a, b):
    return jnp.dot(a, b, preferred_element_type=F32)


def _dot_nt(a, b):
    return lax.dot_general(a, b, (((1,), (1,)), ((), ())), preferred_element_type=F32)


def _mem_kv_kernel(mem_ref, g_ref, w_ref, k_ref, v_ref):
    hb = _rms(mem_ref[...], g_ref[...]).astype(BF16)
    kv = _dot(hb, w_ref[...])
    k_ref[...] = kv[:, :X_WIDTH].astype(BF16)
    v_ref[...] = kv[:, X_WIDTH:].astype(BF16)


def _mem_kv(mem, g, w):
    b, m, d = mem.shape
    return pl.pallas_call(
        _mem_kv_kernel,
        grid=(b,),
        in_specs=[pl.BlockSpec((None, m, d), lambda i: (i, 0, 0)),
                  pl.BlockSpec((1, d), lambda i: (0, 0)),
                  pl.BlockSpec((d, 2 * X_WIDTH), lambda i: (0, 0))],
        out_specs=[pl.BlockSpec((None, m, X_WIDTH), lambda i: (i, 0, 0)),
                   pl.BlockSpec((None, m, X_WIDTH), lambda i: (i, 0, 0))],
        out_shape=[jax.ShapeDtypeStruct((b, m, X_WIDTH), BF16)] * 2,
        compiler_params=pltpu.CompilerParams(dimension_semantics=("arbitrary",)),
        name="mem_kv",
    )(mem, g, w)


def _qkv_kernel(x_ref, pos_ref, ng_ref, wa_ref, qg_ref, wuq_ref, kvg_ref, wuk_ref, wuv_ref,
                invf_ref, qt_ref, k_ref, vt_ref):
    hb = _rms(x_ref[...], ng_ref[...]).astype(BF16)
    za = _dot(hb, wa_ref[...])
    cq = _rms(za[:, :Q_LORA], qg_ref[...]).astype(BF16)
    ckv = _rms(za[:, Q_LORA:Q_LORA + KV_LORA], kvg_ref[...]).astype(BF16)
    kr = za[:, Q_LORA + KV_LORA:]

    ang = invf_ref[...] * pos_ref[...].astype(F32)
    cos, sin = jnp.cos(ang), jnp.sin(ang)

    def rope_t(t):
        x1 = t[ROPE_LO:ROPE_LO + ROPE_HALF]
        x2 = t[ROPE_LO + ROPE_HALF:ROPE_HI]
        return jnp.concatenate(
            [t[:ROPE_LO], x1 * cos - x2 * sin, x1 * sin + x2 * cos, t[ROPE_HI:]], axis=0)

    kr_rot = rope_t(kr.T).T
    q = _dot(cq, wuq_ref[...])
    kn = _dot(ckv, wuk_ref[...])
    for h in range(MLA_HEADS):
        sl = slice(HEAD_PAD * h, HEAD_PAD * (h + 1))
        qt_ref[h] = (rope_t(q[:, sl].T) * (MLA_SCALE * LOG2E)).astype(BF16)
        k_ref[:, sl] = (kn[:, sl] + kr_rot).astype(BF16)
    vt = _dot(ckv, wuv_ref[...]).T.astype(BF16)
    ones_row = (lax.broadcasted_iota(jnp.int32, (V_AUG - V_HEAD, vt.shape[1]), 0) == 0).astype(BF16)
    for h in range(MLA_HEADS):
        vt_ref[h, 0:V_HEAD, :] = vt[V_HEAD * h:V_HEAD * (h + 1), :]
        vt_ref[h, V_HEAD:V_AUG, :] = ones_row


def _qkv(x, pos3, ng, wa, qg, wuq, kvg, wuk, wuv, invf):
    b, s, d = x.shape
    ts = TS_QKV
    const = lambda shape: pl.BlockSpec(shape, lambda i, j: (0,) * len(shape))
    hp = MLA_HEADS * HEAD_PAD
    return pl.pallas_call(
        _qkv_kernel,
        grid=(b, s // ts),
        in_specs=[pl.BlockSpec((None, ts, d), lambda i, j: (i, j, 0)),
                  pl.BlockSpec((None, 1, ts), lambda i, j: (i, 0, j)),
                  const((1, d)), const(wa.shape), const((1, Q_LORA)), const(wuq.shape),
                  const((1, KV_LORA)), const(wuk.shape), const(wuv.shape), const((ROPE_HALF, 1))],
        out_specs=[pl.BlockSpec((None, MLA_HEADS, HEAD_PAD, ts), lambda i, j: (i, 0, 0, j)),
                   pl.BlockSpec((None, ts, hp), lambda i, j: (i, j, 0)),
                   pl.BlockSpec((None, MLA_HEADS, V_AUG, ts), lambda i, j: (i, 0, 0, j))],
        out_shape=[jax.ShapeDtypeStruct((b, MLA_HEADS, HEAD_PAD, s), BF16),
                   jax.ShapeDtypeStruct((b, s, hp), BF16),
                   jax.ShapeDtypeStruct((b, MLA_HEADS, V_AUG, s), BF16)],
        compiler_params=pltpu.CompilerParams(dimension_semantics=("arbitrary", "arbitrary"),
                                             vmem_limit_bytes=VMEM_LIMIT),
        name="qkv",
    )(x, pos3, ng, wa, qg, wuq, kvg, wuk, wuv, invf)


ATTN_SLOTS = 4
ATTN_UNROLL = 4
ATTN_STRIP = 256


def _attn_schedule(n_tiles):
    below = [(qi, j) for qi in range(n_tiles) for j in range(qi)]
    diag = [(qi, qi) for qi in range(n_tiles)]
    assert len(below) % ATTN_UNROLL == 0 and len(diag) % ATTN_UNROLL == 0
    table = np.array(below + diag + [(0, 0)], np.int32).T
    return table, len(below), len(diag)


def _attn_kernel(n_below, n_diag, tbl_ref, qt_ref, k_ref, vt_ref, o_ref, s_scr, m_scr, acc_scr):
    heads = (0, 1)
    units = [(hh, c) for hh in heads for c in range(TQ // ATTN_STRIP)]

    def key_rows(c, diag):
        return min(TK, (c + 1) * ATTN_STRIP) if diag else TK

    def scores(f, slot, hh, c, diag):
        q0 = pl.multiple_of(tbl_ref[0, f] * TQ + c * ATTN_STRIP, ATTN_STRIP)
        k0 = pl.multiple_of(tbl_ref[1, f] * TK, TK)
        nk = key_rows(c, diag)
        k = k_ref[pl.ds(k0, nk), HEAD_PAD * hh:HEAD_PAD * (hh + 1)]
        s_scr[slot, hh, 0:nk, c * ATTN_STRIP:(c + 1) * ATTN_STRIP] = _dot(
            k, qt_ref[hh, :, pl.ds(q0, ATTN_STRIP)])

    def softmax(f, slot, hh, c, diag):
        qi = tbl_ref[0, f]
        nk = key_rows(c, diag)
        cols = slice(c * ATTN_STRIP, (c + 1) * ATTN_STRIP)
        s = s_scr[slot, hh, 0:nk, cols]
        if diag:
            key = lax.broadcasted_iota(jnp.int32, (nk, ATTN_STRIP), 0)
            qry = lax.broadcasted_iota(jnp.int32, (nk, ATTN_STRIP), 1) + c * ATTN_STRIP
            s = jnp.where(key <= qry, s, NEG)
        m_old = m_scr[qi, hh, :, cols]
        m_new = jnp.maximum(m_old, jnp.max(s, axis=0, keepdims=True))
        m_scr[qi, hh, :, cols] = m_new
        alpha = jnp.exp2(m_old - m_new)
        p = jnp.exp2((s - m_new).astype(BF16))
        return alpha, p

    def values(f, hh, c, diag, alpha, p):
        qi = tbl_ref[0, f]
        k0 = pl.multiple_of(tbl_ref[1, f] * TK, TK)
        cols = slice(c * ATTN_STRIP, (c + 1) * ATTN_STRIP)
        vt = vt_ref[hh, :, pl.ds(k0, key_rows(c, diag))]
        acc_scr[qi, hh, :, cols] = alpha * acc_scr[qi, hh, :, cols] + _dot(vt, p)

    def trips(first, count, unroll, diag):
        assert count % unroll == 0 and (count == unroll or unroll % ATTN_SLOTS == 0)

        @pl.loop(0, count // unroll)
        def _(t):
            f = first + t * unroll
            for u in range(unroll):
                slot, nxt = (first + u) % ATTN_SLOTS, (first + u + 1) % ATTN_SLOTS
                for hh, c in units:
                    alpha, p = softmax(f + u, slot, hh, c, diag)
                    scores(f + u + 1, nxt, hh, c, diag)
                    values(f + u, hh, c, diag, alpha, p)

    m_scr[...] = jnp.full(m_scr.shape, NEG, F32)
    acc_scr[...] = jnp.zeros(acc_scr.shape, F32)
    for hh, c in units:
        scores(0, 0, hh, c, n_below == 0)
    trips(0, n_below, ATTN_UNROLL, diag=False)
    trips(n_below, n_diag, ATTN_UNROLL, diag=True)

    @pl.loop(0, n_diag)
    def _(qi):
        ot = jnp.concatenate([acc_scr[qi, hh, 0:V_HEAD] / acc_scr[qi, hh, V_HEAD:V_HEAD + 1]
                              for hh in heads], axis=0)
        o_ref[pl.ds(pl.multiple_of(qi * TQ, TQ), TQ), :] = ot.T.astype(BF16)


def _mla_attn(qt, kp, vt):
    b, s, _ = kp.shape
    pairs = MLA_HEADS // 2
    n_tiles = s // TQ
    table, n_below, n_diag = _attn_schedule(n_tiles)
    grid_spec = pltpu.PrefetchScalarGridSpec(
        num_scalar_prefetch=1,
        grid=(b, pairs),
        in_specs=[pl.BlockSpec((None, 2, HEAD_PAD, s), lambda i, p, tbl: (i, p, 0, 0)),
                  pl.BlockSpec((None, s, 2 * HEAD_PAD), lambda i, p, tbl: (i, 0, p)),
                  pl.BlockSpec((None, 2, V_AUG, s), lambda i, p, tbl: (i, p, 0, 0))],
        out_specs=pl.BlockSpec((None, s, LANES), lambda i, p, tbl: (i, 0, p)),
        scratch_shapes=[pltpu.VMEM((ATTN_SLOTS, 2, TK, TQ), F32),
                        pltpu.VMEM((n_tiles, 2, 1, TQ), F32),
                        pltpu.VMEM((n_tiles, 2, V_AUG, TQ), F32)])
    return pl.pallas_call(
        functools.partial(_attn_kernel, n_below, n_diag),
        grid_spec=grid_spec,
        out_shape=jax.ShapeDtypeStruct((b, s, MLA_WIDTH), BF16),
        compiler_params=pltpu.CompilerParams(dimension_semantics=("arbitrary", "arbitrary"),
                                             vmem_limit_bytes=VMEM_LIMIT),
        name="mla_attn",
    )(jnp.asarray(table), qt, kp, vt)


_C_VAL, _C_GLU, _C_GATE, _M_GATE, _X_Q, _X_GATE, _G0 = (
    0, CONV_WIDTH, 2 * CONV_WIDTH, 3 * CONV_WIDTH, 3 * CONV_WIDTH + MLA_WIDTH,
    3 * CONV_WIDTH + MLA_WIDTH + X_WIDTH, 3 * CONV_WIDTH + MLA_WIDTH + 2 * X_WIDTH)
W_B_COLS = _G0 + 3 * D_MODEL


def _block_kernel(x_ref, o_mla_ref, kx_ref, vx_ref, ng_ref, wc_ref, wr_ref, bg_ref, cw_ref, cb_ref,
                  lng_ref, lnb_ref, wco_ref, wmo_ref, wxo_ref, wout_ref, fg_ref,
                  out_ref, hb_scr, ubuf, ush, conv_scr, part_scr, cgate_scr, g0_scr):
    ts = TS_BLK
    hb_scr[...] = _rms(x_ref[...], ng_ref[...]).astype(BF16)

    def proj(lo, width):
        if lo < _M_GATE:
            return _dot(hb_scr[...], wc_ref[:, lo:lo + width])
        return _dot(hb_scr[...], wr_ref[:, lo - _M_GATE:lo - _M_GATE + width])

    def gate(i, lo, width):
        c0 = i * D_MODEL + lo
        return _sigmoid(proj(_G0 + c0, width) + bg_ref[:, c0:c0 + width])

    @pl.when(pl.program_id(1) == 0)
    def _():
        ubuf[0:HALO, :] = jnp.zeros((HALO, CONV_WIDTH), F32)

    ubuf[HALO:HALO + ts, :] = proj(_C_VAL, CONV_WIDTH) * _sigmoid(proj(_C_GLU, CONV_WIDTH))
    for r in range(1, SUBLANES):
        ush[r - 1] = ubuf[r:r + SHIFT_ROWS, :]
    first = HALO - (CONV_KERNEL - 1)

    def conv_chunk(i):
        r0 = i * CONV_ROWS
        groups = CONV_ROWS // SUBLANES
        acc = jnp.zeros((groups, SUBLANES, CONV_WIDTH), F32) + cb_ref[...]
        for kk in range(CONV_KERNEL):
            r = (first + kk) % SUBLANES
            row = r0 + first + kk - r
            tap = ubuf[row:row + CONV_ROWS, :] if r == 0 else ush[r - 1, row:row + CONV_ROWS, :]
            w = cw_ref[kk * SUBLANES:(kk + 1) * SUBLANES, :]
            acc = acc + tap.reshape(groups, SUBLANES, CONV_WIDTH) * w
        conv_scr[r0:r0 + CONV_ROWS, :] = acc.reshape(CONV_ROWS, CONV_WIDTH)

    st = {}
    half = D_MODEL // 2

    def mla_gate():
        st["a_mla"] = (o_mla_ref[...].astype(F32) * _silu(proj(_M_GATE, MLA_WIDTH))).astype(BF16)

    def mla_out():
        st["y_mla"] = _dot(st["a_mla"], wmo_ref[...])

    def mla_merge(lo):
        part_scr[:, lo:lo + half] = gate(1, lo, half) * st["y_mla"][:, lo:lo + half]

    def x_query():
        st["xq"] = proj(_X_Q, X_WIDTH).astype(BF16)
        st["ox"] = []

    def x_heads(h0):
        for h in (h0, h0 + 1):
            hs = slice(X_HEAD_DIM * h, X_HEAD_DIM * (h + 1))
            s = _dot_nt(st["xq"][:, hs], kx_ref[:, hs]) * X_SCALE
            p = jnp.exp(s - jnp.max(s, axis=-1, keepdims=True))
            p = p / jnp.sum(p, axis=-1, keepdims=True)
            st["ox"].append(_dot(p.astype(BF16), vx_ref[:, hs]))

    def x_gate():
        a_x = jnp.concatenate(st["ox"], axis=-1) * _silu(proj(_X_GATE, X_WIDTH))
        st["a_x"] = a_x.astype(BF16)

    def x_out():
        st["y_x"] = _dot(st["a_x"], wxo_ref[...])

    def x_merge(lo):
        part_scr[:, lo:lo + half] += gate(2, lo, half) * st["y_x"][:, lo:lo + half]

    def conv_gate():
        cgate_scr[...] = _silu(proj(_C_GATE, CONV_WIDTH))

    def conv_merge_gate(lo):
        g0_scr[:, lo:lo + half] = gate(0, lo, half)

    side = [mla_gate, mla_out, functools.partial(mla_merge, 0), functools.partial(mla_merge, half),
            x_query, functools.partial(x_heads, 0), functools.partial(x_heads, 2), x_gate, x_out,
            functools.partial(x_merge, 0), functools.partial(x_merge, half), conv_gate,
            functools.partial(conv_merge_gate, 0), functools.partial(conv_merge_gate, half)]
    n_chunks = ts // CONV_ROWS
    for i in range(n_chunks):
        conv_chunk(i)
        for j, piece in enumerate(side):
            if j * n_chunks // len(side) == i:
                piece()
    ubuf[0:HALO, :] = ubuf[ts:ts + HALO, :]

    conv = conv_scr[...]
    mu = jnp.mean(conv, axis=-1, keepdims=True)
    cen = conv - mu
    var = jnp.mean(cen * cen, axis=-1, keepdims=True)
    ln = cen * lax.rsqrt(var + EPS) * lng_ref[...] + lnb_ref[...]
    a_conv = _silu(ln) * cgate_scr[...]
    merged = g0_scr[...] * _dot(a_conv.astype(BF16), wco_ref[...]) + part_scr[...]
    y = x_ref[...] + _dot(merged.astype(BF16), wout_ref[...])
    out_ref[...] = _rms(y, fg_ref[...])


def _block(x, o_mla, kx, vx, ng, w_in_bf, wr, bg, cw, cb, lng, lnb, wco, wmo, wxo, wout, fg):
    b, s, d = x.shape
    ts = TS_BLK
    const = lambda shape: pl.BlockSpec(shape, lambda i, j: (0,) * len(shape),
                                       pipeline_mode=pl.Buffered(1))
    return pl.pallas_call(
        _block_kernel,
        grid=(b, s // ts),
        in_specs=[pl.BlockSpec((None, ts, d), lambda i, j: (i, j, 0)),
                  pl.BlockSpec((None, ts, MLA_WIDTH), lambda i, j: (i, j, 0)),
                  pl.BlockSpec((None, N_MEM, X_WIDTH), lambda i, j: (i, 0, 0)),
                  pl.BlockSpec((None, N_MEM, X_WIDTH), lambda i, j: (i, 0, 0)),
                  const((1, d)), const((d, _M_GATE)), const(wr.shape), const(bg.shape),
                  const(cw.shape),
                  const((1, CONV_WIDTH)), const((1, CONV_WIDTH)), const((1, CONV_WIDTH)),
                  const(wco.shape), const(wmo.shape), const(wxo.shape), const(wout.shape),
                  const((1, d))],
        out_specs=pl.BlockSpec((None, ts, d), lambda i, j: (i, j, 0)),
        out_shape=jax.ShapeDtypeStruct((b, s, d), F32),
        scratch_shapes=[pltpu.VMEM((ts, d), BF16),
                        pltpu.VMEM((HALO + ts, CONV_WIDTH), F32),
                        pltpu.VMEM((SUBLANES - 1, SHIFT_ROWS, CONV_WIDTH), F32),
                        pltpu.VMEM((ts, CONV_WIDTH), F32),
                        pltpu.VMEM((ts, d), F32),
                        pltpu.VMEM((ts, CONV_WIDTH), F32),
                        pltpu.VMEM((ts, d), F32)],
        compiler_params=pltpu.CompilerParams(dimension_semantics=("arbitrary", "arbitrary"),
                                             vmem_limit_bytes=VMEM_LIMIT),
        name="block",
    )(x, o_mla, kx, vx, ng, w_in_bf, wr, bg, cw, cb, lng, lnb, wco, wmo, wxo, wout, fg)


def _inv_freq_column():
    inv = ROPE_THETA ** (-jnp.arange(0, QK_ROPE, 2, dtype=F32) / QK_ROPE)
    return inv.reshape(ROPE_HALF, 1)


def kernel(x, mem, positions, norm_g, w_in, b_gate, conv_w, conv_b, conv_ln_g, conv_ln_b,
           w_conv_o, q_norm_g, w_uq, kv_norm_g, w_ukv, w_mla_o, mem_norm_g, w_mem_kv, w_x_o,
           w_out, final_norm_g):
    assert norm_g.shape[0] == 1, "single-layer trunk"
    b, s, d = x.shape
    row = lambda v: v.reshape(1, -1)
    edges = np.cumsum((0,) + IN_SPLITS)
    w_in_bf = w_in[0].astype(BF16)
    (w_cval, w_cglu, w_cgate, w_qd, w_kvd, w_kr, w_mg, w_xq, w_xg, w_g) = (
        w_in_bf[:, edges[i]:edges[i + 1]] for i in range(len(IN_SPLITS)))

    w_kr_pad = jnp.pad(w_kr, ((0, 0), (ROPE_LO, LANES - ROPE_HI)))
    wa = jnp.concatenate([w_qd, w_kvd, w_kr_pad], axis=1)
    assert edges[3] == _M_GATE and edges[-1] - edges[6] == W_B_COLS - _M_GATE
    wr = w_in_bf[:, edges[6]:]
    wuq = jnp.pad(w_uq[0].reshape(Q_LORA, MLA_HEADS, QK_NOPE + QK_ROPE),
                  ((0, 0), (0, 0), (0, HEAD_PAD - QK_NOPE - QK_ROPE)))
    wuq = wuq.reshape(Q_LORA, MLA_HEADS * HEAD_PAD).astype(BF16)
    wukv = w_ukv[0].reshape(KV_LORA, MLA_HEADS, QK_NOPE + V_HEAD)
    wuk = jnp.pad(wukv[:, :, :QK_NOPE], ((0, 0), (0, 0), (0, HEAD_PAD - QK_NOPE)))
    wuk = wuk.reshape(KV_LORA, MLA_HEADS * HEAD_PAD).astype(BF16)
    wuv = wukv[:, :, QK_NOPE:].reshape(KV_LORA, MLA_WIDTH).astype(BF16)

    kx, vx = _mem_kv(mem, row(mem_norm_g[0]), w_mem_kv[0].astype(BF16))
    qt, kp, vt = _qkv(x, positions.reshape(b, 1, s), row(norm_g[0]), wa, row(q_norm_g[0]), wuq,
                      row(kv_norm_g[0]), wuk, wuv, _inv_freq_column())
    o_mla = _mla_attn(qt, kp, vt)
    return _block(x, o_mla, kx, vx, row(norm_g[0]), w_in_bf, wr, row(b_gate[0]),
                  jnp.repeat(conv_w[0], SUBLANES, axis=0),
                  row(conv_b[0]), row(conv_ln_g[0]), row(conv_ln_b[0]),
                  w_conv_o[0].astype(BF16), w_mla_o[0].astype(BF16), w_x_o[0].astype(BF16),
                  w_out[0].astype(BF16), row(final_norm_g))
```

```python
import functools

import jax
import jax.numpy as jnp
import numpy as np
from jax import lax
from jax.experimental import pallas as pl
from jax.experimental.pallas import tpu as pltpu

F32 = jnp.float32
BF16 = jnp.bfloat16

D_MODEL = 1024
N_MEM = 256
CONV_WIDTH = 512
CONV_KERNEL = 31
MLA_HEADS = 8
QK_NOPE = 64
QK_ROPE = 32
V_HEAD = 64
Q_LORA = 384
KV_LORA = 256
MLA_WIDTH = MLA_HEADS * V_HEAD
X_HEADS = 4
X_HEAD_DIM = 128
X_WIDTH = X_HEADS * X_HEAD_DIM
ROPE_THETA = 10000.0
EPS = 1e-6
IN_SPLITS = (CONV_WIDTH, CONV_WIDTH, CONV_WIDTH, Q_LORA, KV_LORA, QK_ROPE, MLA_WIDTH,
             X_WIDTH, X_WIDTH, 3 * D_MODEL)

LANES = 128
HEAD_PAD = LANES
BF16_ROWS = 16
V_AUG = V_HEAD + BF16_ROWS
ROPE_LO = QK_NOPE
ROPE_HALF = QK_ROPE // 2
ROPE_HI = QK_NOPE + QK_ROPE
MLA_SCALE = (QK_NOPE + QK_ROPE) ** -0.5
X_SCALE = X_HEAD_DIM ** -0.5
LOG2E = float(np.log2(np.e))
NEG = float(np.finfo(np.float32).min)

TS_QKV = 512
TQ = 512
TK = 512
TS_BLK = 512
HALO = 32
CONV_ROWS = 32
SUBLANES = 8
SHIFT_ROWS = TS_BLK + HALO - SUBLANES
VMEM_LIMIT = 56 * 1024 * 1024


def _rms(x, g):
    return x * lax.rsqrt(jnp.mean(x * x, axis=-1, keepdims=True) + EPS) * g


def _sigmoid(x):
    return 0.5 * jnp.tanh(0.5 * x) + 0.5


def _silu(x):
    h = 0.5 * x
    return h * jnp.tanh(h) + h


def _dot(a, b):
    return jnp.dot(a, b, preferred_element_type=F32)


def _dot_nt(a, b):
    return lax.dot_general(a, b, (((1,), (1,)), ((), ())), preferred_element_type=F32)


def _mem_kv_kernel(mem_ref, g_ref, w_ref, k_ref, v_ref):
    hb = _rms(mem_ref[...], g_ref[...]).astype(BF16)
    kv = _dot(hb, w_ref[...])
    k_ref[...] = kv[:, :X_WIDTH].astype(BF16)
    v_ref[...] = kv[:, X_WIDTH:].astype(BF16)


def _mem_kv(mem, g, w):
    b, m, d = mem.shape
    return pl.pallas_call(
        _mem_kv_kernel,
        grid=(b,),
        in_specs=[pl.BlockSpec((None, m, d), lambda i: (i, 0, 0)),
                  pl.BlockSpec((1, d), lambda i: (0, 0)),
                  pl.BlockSpec((d, 2 * X_WIDTH), lambda i: (0, 0))],
        out_specs=[pl.BlockSpec((None, m, X_WIDTH), lambda i: (i, 0, 0)),
                   pl.BlockSpec((None, m, X_WIDTH), lambda i: (i, 0, 0))],
        out_shape=[jax.ShapeDtypeStruct((b, m, X_WIDTH), BF16)] * 2,
        compiler_params=pltpu.CompilerParams(dimension_semantics=("arbitrary",)),
        name="mem_kv",
    )(mem, g, w)


def _qkv_kernel(x_ref, pos_ref, ng_ref, wa_ref, qg_ref, wuq_ref, kvg_ref, wuk_ref, wuv_ref,
                invf_ref, qt_ref, k_ref, vt_ref):
    hb = _rms(x_ref[...], ng_ref[...]).astype(BF16)
    za = _dot(hb, wa_ref[...])
    cq = _rms(za[:, :Q_LORA], qg_ref[...]).astype(BF16)
    ckv = _rms(za[:, Q_LORA:Q_LORA + KV_LORA], kvg_ref[...]).astype(BF16)
    kr = za[:, Q_LORA + KV_LORA:]

    ang = invf_ref[...] * pos_ref[...].astype(F32)
    cos, sin = jnp.cos(ang), jnp.sin(ang)

    def rope_t(t):
        x1 = t[ROPE_LO:ROPE_LO + ROPE_HALF]
        x2 = t[ROPE_LO + ROPE_HALF:ROPE_HI]
        return jnp.concatenate(
            [t[:ROPE_LO], x1 * cos - x2 * sin, x1 * sin + x2 * cos, t[ROPE_HI:]], axis=0)

    kr_rot = rope_t(kr.T).T
    q = _dot(cq, wuq_ref[...])
    kn = _dot(ckv, wuk_ref[...])
    for h in range(MLA_HEADS):
        sl = slice(HEAD_PAD * h, HEAD_PAD * (h + 1))
        qt_ref[h] = (rope_t(q[:, sl].T) * (MLA_SCALE * LOG2E)).astype(BF16)
        k_ref[:, sl] = (kn[:, sl] + kr_rot).astype(BF16)
    vt = _dot(ckv, wuv_ref[...]).T.astype(BF16)
    ones_row = (lax.broadcasted_iota(jnp.int32, (V_AUG - V_HEAD, vt.shape[1]), 0) == 0).astype(BF16)
    for h in range(MLA_HEADS):
        vt_ref[h, 0:V_HEAD, :] = vt[V_HEAD * h:V_HEAD * (h + 1), :]
        vt_ref[h, V_HEAD:V_AUG, :] = ones_row


def _qkv(x, pos3, ng, wa, qg, wuq, kvg, wuk, wuv, invf):
    b, s, d = x.shape
    ts = TS_QKV
    const = lambda shape: pl.BlockSpec(shape, lambda i, j: (0,) * len(shape))
    hp = MLA_HEADS * HEAD_PAD
    return pl.pallas_call(
        _qkv_kernel,
        grid=(b, s // ts),
        in_specs=[pl.BlockSpec((None, ts, d), lambda i, j: (i, j, 0)),
                  pl.BlockSpec((None, 1, ts), lambda i, j: (i, 0, j)),
                  const((1, d)), const(wa.shape), const((1, Q_LORA)), const(wuq.shape),
                  const((1, KV_LORA)), const(wuk.shape), const(wuv.shape), const((ROPE_HALF, 1))],
        out_specs=[pl.BlockSpec((None, MLA_HEADS, HEAD_PAD, ts), lambda i, j: (i, 0, 0, j)),
                   pl.BlockSpec((None, ts, hp), lambda i, j: (i, j, 0)),
                   pl.BlockSpec((None, MLA_HEADS, V_AUG, ts), lambda i, j: (i, 0, 0, j))],
        out_shape=[jax.ShapeDtypeStruct((b, MLA_HEADS, HEAD_PAD, s), BF16),
                   jax.ShapeDtypeStruct((b, s, hp), BF16),
                   jax.ShapeDtypeStruct((b, MLA_HEADS, V_AUG, s), BF16)],
        compiler_params=pltpu.CompilerParams(dimension_semantics=("arbitrary", "arbitrary"),
                                             vmem_limit_bytes=VMEM_LIMIT),
        name="qkv",
    )(x, pos3, ng, wa, qg, wuq, kvg, wuk, wuv, invf)


ATTN_SLOTS = 4
ATTN_UNROLL = 4
ATTN_STRIP = 256
MAX_CHAINS = 4


def _attn_schedule(n_tiles):
    below = [(qi, j) for qi in range(n_tiles) for j in range(qi)]
    diag = [(qi, qi) for qi in range(n_tiles)]
    assert len(below) % ATTN_UNROLL == 0 and len(diag) % ATTN_UNROLL == 0
    table = np.array(below + diag + [(0, 0)], np.int32).T
    return table, len(below), len(diag)


def _attn_kernel(n_below, n_diag, tbl_ref, qt_ref, k_ref, vt_ref, o_ref,
                 s_scr, mblk_scr, m_scr, acc_scr):
    heads = (0, 1)
    units = [(hh, c) for hh in heads for c in range(TQ // ATTN_STRIP)]

    def key_rows(c, diag):
        return min(TK, (c + 1) * ATTN_STRIP) if diag else TK

    def scores(f, slot, hh, c, diag):
        q0 = pl.multiple_of(tbl_ref[0, f] * TQ + c * ATTN_STRIP, ATTN_STRIP)
        k0 = pl.multiple_of(tbl_ref[1, f] * TK, TK)
        nk = key_rows(c, diag)
        cols = slice(c * ATTN_STRIP, (c + 1) * ATTN_STRIP)
        k = k_ref[pl.ds(k0, nk), HEAD_PAD * hh:HEAD_PAD * (hh + 1)]
        s = _dot(k, qt_ref[hh, :, pl.ds(q0, ATTN_STRIP)])
        if diag:
            key = lax.broadcasted_iota(jnp.int32, (nk, ATTN_STRIP), 0)
            qry = lax.broadcasted_iota(jnp.int32, (nk, ATTN_STRIP), 1) + c * ATTN_STRIP
            s = jnp.where(key <= qry, s, NEG)
        s_scr[slot, hh, 0:nk, cols] = s
        tiles = s.reshape(MAX_CHAINS, nk // (MAX_CHAINS * SUBLANES), SUBLANES, ATTN_STRIP)
        mblk_scr[slot, hh, :, cols] = jnp.max(jnp.max(jnp.max(tiles, axis=1), axis=0),
                                              axis=0, keepdims=True)

    def softmax(f, slot, hh, c, diag):
        qi = tbl_ref[0, f]
        nk = key_rows(c, diag)
        cols = slice(c * ATTN_STRIP, (c + 1) * ATTN_STRIP)
        m_old = m_scr[qi, hh, :, cols]
        m_new = jnp.maximum(m_old, mblk_scr[slot, hh, :, cols])
        m_scr[qi, hh, :, cols] = m_new
        alpha = jnp.exp2(m_old - m_new)
        p = jnp.exp2((s_scr[slot, hh, 0:nk, cols] - m_new).astype(BF16))
        return alpha, p

    def values(f, hh, c, diag, alpha, p):
        qi = tbl_ref[0, f]
        k0 = pl.multiple_of(tbl_ref[1, f] * TK, TK)
        cols = slice(c * ATTN_STRIP, (c + 1) * ATTN_STRIP)
        vt = vt_ref[hh, :, pl.ds(k0, key_rows(c, diag))]
        acc_scr[qi, hh, :, cols] = alpha * acc_scr[qi, hh, :, cols] + _dot(vt, p)

    def trips(first, count, unroll, diag):
        assert count % unroll == 0 and (count == unroll or unroll % ATTN_SLOTS == 0)
        if diag and first > 0:
            for hh, c in units:
                scores(first, first % ATTN_SLOTS, hh, c, diag)

        @pl.loop(0, count // unroll)
        def _(t):
            f = first + t * unroll
            for u in range(unroll):
                slot, nxt = (first + u) % ATTN_SLOTS, (first + u + 1) % ATTN_SLOTS
                for hh, c in units:
                    alpha, p = softmax(f + u, slot, hh, c, diag)
                    scores(f + u + 1, nxt, hh, c, diag)
                    values(f + u, hh, c, diag, alpha, p)

    m_scr[...] = jnp.full(m_scr.shape, NEG, F32)
    acc_scr[...] = jnp.zeros(acc_scr.shape, F32)
    for hh, c in units:
        scores(0, 0, hh, c, n_below == 0)
    trips(0, n_below, ATTN_UNROLL, diag=False)
    trips(n_below, n_diag, ATTN_UNROLL, diag=True)

    @pl.loop(0, n_diag)
    def _(qi):
        ot = jnp.concatenate([acc_scr[qi, hh, 0:V_HEAD] / acc_scr[qi, hh, V_HEAD:V_HEAD + 1]
                              for hh in heads], axis=0)
        o_ref[pl.ds(pl.multiple_of(qi * TQ, TQ), TQ), :] = ot.T.astype(BF16)


def _mla_attn(qt, kp, vt):
    b, s, _ = kp.shape
    pairs = MLA_HEADS // 2
    n_tiles = s // TQ
    table, n_below, n_diag = _attn_schedule(n_tiles)
    grid_spec = pltpu.PrefetchScalarGridSpec(
        num_scalar_prefetch=1,
        grid=(b, pairs),
        in_specs=[pl.BlockSpec((None, 2, HEAD_PAD, s), lambda i, p, tbl: (i, p, 0, 0)),
                  pl.BlockSpec((None, s, 2 * HEAD_PAD), lambda i, p, tbl: (i, 0, p)),
                  pl.BlockSpec((None, 2, V_AUG, s), lambda i, p, tbl: (i, p, 0, 0))],
        out_specs=pl.BlockSpec((None, s, LANES), lambda i, p, tbl: (i, 0, p)),
        scratch_shapes=[pltpu.VMEM((ATTN_SLOTS, 2, TK, TQ), F32),
                        pltpu.VMEM((ATTN_SLOTS, 2, 1, TQ), F32),
                        pltpu.VMEM((n_tiles, 2, 1, TQ), F32),
                        pltpu.VMEM((n_tiles, 2, V_AUG, TQ), F32)])
    return pl.pallas_call(
        functools.partial(_attn_kernel, n_below, n_diag),
        grid_spec=grid_spec,
        out_shape=jax.ShapeDtypeStruct((b, s, MLA_WIDTH), BF16),
        compiler_params=pltpu.CompilerParams(dimension_semantics=("arbitrary", "arbitrary"),
                                             vmem_limit_bytes=VMEM_LIMIT),
        name="mla_attn",
    )(jnp.asarray(table), qt, kp, vt)


_C_VAL, _C_GLU, _C_GATE, _M_GATE, _X_Q, _X_GATE, _G0 = (
    0, CONV_WIDTH, 2 * CONV_WIDTH, 3 * CONV_WIDTH, 3 * CONV_WIDTH + MLA_WIDTH,
    3 * CONV_WIDTH + MLA_WIDTH + X_WIDTH, 3 * CONV_WIDTH + MLA_WIDTH + 2 * X_WIDTH)
W_B_COLS = _G0 + 3 * D_MODEL


def _block_kernel(x_ref, o_mla_ref, kx_ref, vx_ref, ng_ref, wc_ref, wr_ref, bg_ref, cw_ref, cb_ref,
                  lng_ref, lnb_ref, wco_ref, wmo_ref, wxo_ref, wout_ref, fg_ref,
                  out_ref, hb_scr, ubuf, ush, conv_scr, part_scr, cgate_scr, g0_scr):
    ts = TS_BLK
    hb_scr[...] = _rms(x_ref[...], ng_ref[...]).astype(BF16)

    def proj(lo, width):
        if lo < _M_GATE:
            return _dot(hb_scr[...], wc_ref[:, lo:lo + width])
        return _dot(hb_scr[...], wr_ref[:, lo - _M_GATE:lo - _M_GATE + width])

    def gate(i, lo, width):
        c0 = i * D_MODEL + lo
        return _sigmoid(proj(_G0 + c0, width) + bg_ref[:, c0:c0 + width])

    @pl.when(pl.program_id(1) == 0)
    def _():
        ubuf[0:HALO, :] = jnp.zeros((HALO, CONV_WIDTH), F32)

    ubuf[HALO:HALO + ts, :] = proj(_C_VAL, CONV_WIDTH) * _sigmoid(proj(_C_GLU, CONV_WIDTH))
    for r in range(1, SUBLANES):
        ush[r - 1] = ubuf[r:r + SHIFT_ROWS, :]
    first = HALO - (CONV_KERNEL - 1)

    def conv_chunk(i):
        r0 = i * CONV_ROWS
        groups = CONV_ROWS // SUBLANES
        acc = jnp.zeros((groups, SUBLANES, CONV_WIDTH), F32) + cb_ref[...]
        for kk in range(CONV_KERNEL):
            r = (first + kk) % SUBLANES
            row = r0 + first + kk - r
            tap = ubuf[row:row + CONV_ROWS, :] if r == 0 else ush[r - 1, row:row + CONV_ROWS, :]
            w = cw_ref[kk * SUBLANES:(kk + 1) * SUBLANES, :]
            acc = acc + tap.reshape(groups, SUBLANES, CONV_WIDTH) * w
        conv_scr[r0:r0 + CONV_ROWS, :] = acc.reshape(CONV_ROWS, CONV_WIDTH)

    st = {}
    half = D_MODEL // 2

    def mla_gate():
        st["a_mla"] = (o_mla_ref[...].astype(F32) * _silu(proj(_M_GATE, MLA_WIDTH))).astype(BF16)

    def mla_out():
        st["y_mla"] = _dot(st["a_mla"], wmo_ref[...])

    def mla_merge(lo):
        part_scr[:, lo:lo + half] = gate(1, lo, half) * st["y_mla"][:, lo:lo + half]

    def x_query():
        st["xq"] = proj(_X_Q, X_WIDTH).astype(BF16)
        st["ox"] = []

    def x_heads(h0):
        for h in (h0, h0 + 1):
            hs = slice(X_HEAD_DIM * h, X_HEAD_DIM * (h + 1))
            s = _dot_nt(st["xq"][:, hs], kx_ref[:, hs]) * X_SCALE
            p = jnp.exp(s - jnp.max(s, axis=-1, keepdims=True))
            p = p / jnp.sum(p, axis=-1, keepdims=True)
            st["ox"].append(_dot(p.astype(BF16), vx_ref[:, hs]))

    def x_gate():
        a_x = jnp.concatenate(st["ox"], axis=-1) * _silu(proj(_X_GATE, X_WIDTH))
        st["a_x"] = a_x.astype(BF16)

    def x_out():
        st["y_x"] = _dot(st["a_x"], wxo_ref[...])

    def x_merge(lo):
        part_scr[:, lo:lo + half] += gate(2, lo, half) * st["y_x"][:, lo:lo + half]

    def conv_gate():
        cgate_scr[...] = _silu(proj(_C_GATE, CONV_WIDTH))

    def conv_merge_gate(lo):
        g0_scr[:, lo:lo + half] = gate(0, lo, half)

    side = [mla_gate, mla_out, functools.partial(mla_merge, 0), functools.partial(mla_merge, half),
            x_query, functools.partial(x_heads, 0), functools.partial(x_heads, 2), x_gate, x_out,
            functools.partial(x_merge, 0), functools.partial(x_merge, half), conv_gate,
            functools.partial(conv_merge_gate, 0), functools.partial(conv_merge_gate, half)]
    n_chunks = ts // CONV_ROWS
    for i in range(n_chunks):
        conv_chunk(i)
        for j, piece in enumerate(side):
            if j * n_chunks // len(side) == i:
                piece()
    ubuf[0:HALO, :] = ubuf[ts:ts + HALO, :]

    conv = conv_scr[...]
    mu = jnp.mean(conv, axis=-1, keepdims=True)
    cen = conv - mu
    var = jnp.mean(cen * cen, axis=-1, keepdims=True)
    ln = cen * lax.rsqrt(var + EPS) * lng_ref[...] + lnb_ref[...]
    a_conv = _silu(ln) * cgate_scr[...]
    merged = g0_scr[...] * _dot(a_conv.astype(BF16), wco_ref[...]) + part_scr[...]
    y = x_ref[...] + _dot(merged.astype(BF16), wout_ref[...])
    out_ref[...] = _rms(y, fg_ref[...])


def _block(x, o_mla, kx, vx, ng, w_in_bf, wr, bg, cw, cb, lng, lnb, wco, wmo, wxo, wout, fg):
    b, s, d = x.shape
    ts = TS_BLK
    const = lambda shape: pl.BlockSpec(shape, lambda i, j: (0,) * len(shape),
                                       pipeline_mode=pl.Buffered(1))
    return pl.pallas_call(
        _block_kernel,
        grid=(b, s // ts),
        in_specs=[pl.BlockSpec((None, ts, d), lambda i, j: (i, j, 0)),
                  pl.BlockSpec((None, ts, MLA_WIDTH), lambda i, j: (i, j, 0)),
                  pl.BlockSpec((None, N_MEM, X_WIDTH), lambda i, j: (i, 0, 0)),
                  pl.BlockSpec((None, N_MEM, X_WIDTH), lambda i, j: (i, 0, 0)),
                  const((1, d)), const((d, _M_GATE)), const(wr.shape), const(bg.shape),
                  const(cw.shape),
                  const((1, CONV_WIDTH)), const((1, CONV_WIDTH)), const((1, CONV_WIDTH)),
                  const(wco.shape), const(wmo.shape), const(wxo.shape), const(wout.shape),
                  const((1, d))],
        out_specs=pl.BlockSpec((None, ts, d), lambda i, j: (i, j, 0)),
        out_shape=jax.ShapeDtypeStruct((b, s, d), F32),
        scratch_shapes=[pltpu.VMEM((ts, d), BF16),
                        pltpu.VMEM((HALO + ts, CONV_WIDTH), F32),
                        pltpu.VMEM((SUBLANES - 1, SHIFT_ROWS, CONV_WIDTH), F32),
                        pltpu.VMEM((ts, CONV_WIDTH), F32),
                        pltpu.VMEM((ts, d), F32),
                        pltpu.VMEM((ts, CONV_WIDTH), F32),
                        pltpu.VMEM((ts, d), F32)],
        compiler_params=pltpu.CompilerParams(dimension_semantics=("arbitrary", "arbitrary"),
                                             vmem_limit_bytes=VMEM_LIMIT),
        name="block",
    )(x, o_mla, kx, vx, ng, w_in_bf, wr, bg, cw, cb, lng, lnb, wco, wmo, wxo, wout, fg)


def _inv_freq_column():
    inv = ROPE_THETA ** (-jnp.arange(0, QK_ROPE, 2, dtype=F32) / QK_ROPE)
    return inv.reshape(ROPE_HALF, 1)


def kernel(x, mem, positions, norm_g, w_in, b_gate, conv_w, conv_b, conv_ln_g, conv_ln_b,
           w_conv_o, q_norm_g, w_uq, kv_norm_g, w_ukv, w_mla_o, mem_norm_g, w_mem_kv, w_x_o,
           w_out, final_norm_g):
    assert norm_g.shape[0] == 1, "single-layer trunk"
    b, s, d = x.shape
    row = lambda v: v.reshape(1, -1)
    edges = np.cumsum((0,) + IN_SPLITS)
    w_in_bf = w_in[0].astype(BF16)
    (w_cval, w_cglu, w_cgate, w_qd, w_kvd, w_kr, w_mg, w_xq, w_xg, w_g) = (
        w_in_bf[:, edges[i]:edges[i + 1]] for i in range(len(IN_SPLITS)))

    w_kr_pad = jnp.pad(w_kr, ((0, 0), (ROPE_LO, LANES - ROPE_HI)))
    wa = jnp.concatenate([w_qd, w_kvd, w_kr_pad], axis=1)
    assert edges[3] == _M_GATE and edges[-1] - edges[6] == W_B_COLS - _M_GATE
    wr = w_in_bf[:, edges[6]:]
    wuq = jnp.pad(w_uq[0].reshape(Q_LORA, MLA_HEADS, QK_NOPE + QK_ROPE),
                  ((0, 0), (0, 0), (0, HEAD_PAD - QK_NOPE - QK_ROPE)))
    wuq = wuq.reshape(Q_LORA, MLA_HEADS * HEAD_PAD).astype(BF16)
    wukv = w_ukv[0].reshape(KV_LORA, MLA_HEADS, QK_NOPE + V_HEAD)
    wuk = jnp.pad(wukv[:, :, :QK_NOPE], ((0, 0), (0, 0), (0, HEAD_PAD - QK_NOPE)))
    wuk = wuk.reshape(KV_LORA, MLA_HEADS * HEAD_PAD).astype(BF16)
    wuv = wukv[:, :, QK_NOPE:].reshape(KV_LORA, MLA_WIDTH).astype(BF16)

    kx, vx = _mem_kv(mem, row(mem_norm_g[0]), w_mem_kv[0].astype(BF16))
    qt, kp, vt = _qkv(x, positions.reshape(b, 1, s), row(norm_g[0]), wa, row(q_norm_g[0]), wuq,
                      row(kv_norm_g[0]), wuk, wuv, _inv_freq_column())
    o_mla = _mla_attn(qt, kp, vt)
    return _block(x, o_mla, kx, vx, row(norm_g[0]), w_in_bf, wr, row(b_gate[0]),
                  jnp.repeat(conv_w[0], SUBLANES, axis=0),
                  row(conv_b[0]), row(conv_ln_g[0]), row(conv_ln_b[0]),
                  w_conv_o[0].astype(BF16), w_mla_o[0].astype(BF16), w_x_o[0].astype(BF16),
                  w_out[0].astype(BF16), row(final_norm_g))
```

```python
import functools

import jax
import jax.numpy as jnp
import numpy as np
from jax import lax
from jax.experimental import pallas as pl
from jax.experimental.pallas import tpu as pltpu

F32 = jnp.float32
BF16 = jnp.bfloat16

D_MODEL = 1024
N_MEM = 256
CONV_WIDTH = 512
CONV_KERNEL = 31
MLA_HEADS = 8
QK_NOPE = 64
QK_ROPE = 32
V_HEAD = 64
Q_LORA = 384
KV_LORA = 256
MLA_WIDTH = MLA_HEADS * V_HEAD
X_HEADS = 4
X_HEAD_DIM = 128
X_WIDTH = X_HEADS * X_HEAD_DIM
ROPE_THETA = 10000.0
EPS = 1e-6
IN_SPLITS = (CONV_WIDTH, CONV_WIDTH, CONV_WIDTH, Q_LORA, KV_LORA, QK_ROPE, MLA_WIDTH,
             X_WIDTH, X_WIDTH, 3 * D_MODEL)

LANES = 128
HEAD_PAD = LANES
BF16_ROWS = 16
V_AUG = V_HEAD + BF16_ROWS
ROPE_LO = QK_NOPE
ROPE_HALF = QK_ROPE // 2
ROPE_HI = QK_NOPE + QK_ROPE
MLA_SCALE = (QK_NOPE + QK_ROPE) ** -0.5
X_SCALE = X_HEAD_DIM ** -0.5
LOG2E = float(np.log2(np.e))
NEG = float(np.finfo(np.float32).min)

TS_QKV = 512
TQ = 512
TK = 512
TS_BLK = 512
HALO = 32
CONV_ROWS = 32
SUBLANES = 8
SHIFT_ROWS = TS_BLK + HALO - SUBLANES
VMEM_LIMIT = 56 * 1024 * 1024


def _rms(x, g):
    return x * lax.rsqrt(jnp.mean(x * x, axis=-1, keepdims=True) + EPS) * g


def _twice_sigmoid_of_double(h):
    return jnp.tanh(h) + 1.0


def _silu_of_double(h):
    return h * _twice_sigmoid_of_double(h)


def _dot(a, b):
    return jnp.dot(a, b, preferred_element_type=F32)


def _dot_nt(a, b):
    return lax.dot_general(a, b, (((1,), (1,)), ((), ())), preferred_element_type=F32)


def _mem_kv_kernel(mem_ref, g_ref, w_ref, k_ref, v_ref):
    hb = _rms(mem_ref[...], g_ref[...]).astype(BF16)
    kv = _dot(hb, w_ref[...])
    k_ref[...] = kv[:, :X_WIDTH].astype(BF16)
    v_ref[...] = kv[:, X_WIDTH:].astype(BF16)


def _mem_kv(mem, g, w):
    b, m, d = mem.shape
    return pl.pallas_call(
        _mem_kv_kernel,
        grid=(b,),
        in_specs=[pl.BlockSpec((None, m, d), lambda i: (i, 0, 0)),
                  pl.BlockSpec((1, d), lambda i: (0, 0)),
                  pl.BlockSpec((d, 2 * X_WIDTH), lambda i: (0, 0))],
        out_specs=[pl.BlockSpec((None, m, X_WIDTH), lambda i: (i, 0, 0)),
                   pl.BlockSpec((None, m, X_WIDTH), lambda i: (i, 0, 0))],
        out_shape=[jax.ShapeDtypeStruct((b, m, X_WIDTH), BF16)] * 2,
        compiler_params=pltpu.CompilerParams(dimension_semantics=("arbitrary",)),
        name="mem_kv",
    )(mem, g, w)


def _qkv_kernel(x_ref, pos_ref, ng_ref, wa_ref, qg_ref, wuq_ref, kvg_ref, wuk_ref, wuv_ref,
                invf_ref, qt_ref, k_ref, vt_ref, hb_ref):
    hb = _rms(x_ref[...], ng_ref[...]).astype(BF16)
    hb_ref[...] = hb
    za = _dot(hb, wa_ref[...])
    cq = _rms(za[:, :Q_LORA], qg_ref[...]).astype(BF16)
    ckv = _rms(za[:, Q_LORA:Q_LORA + KV_LORA], kvg_ref[...]).astype(BF16)
    kr = za[:, Q_LORA + KV_LORA:]

    ang = invf_ref[...] * pos_ref[...].astype(F32)
    cos, sin = jnp.cos(ang), jnp.sin(ang)

    def rope_t(t):
        x1 = t[ROPE_LO:ROPE_LO + ROPE_HALF]
        x2 = t[ROPE_LO + ROPE_HALF:ROPE_HI]
        return jnp.concatenate(
            [t[:ROPE_LO], x1 * cos - x2 * sin, x1 * sin + x2 * cos, t[ROPE_HI:]], axis=0)

    kr_rot = rope_t(kr.T).T
    q = _dot(cq, wuq_ref[...])
    kn = _dot(ckv, wuk_ref[...])
    for h in range(MLA_HEADS):
        sl = slice(HEAD_PAD * h, HEAD_PAD * (h + 1))
        qt_ref[h] = (rope_t(q[:, sl].T) * (MLA_SCALE * LOG2E)).astype(BF16)
        k_ref[:, sl] = (kn[:, sl] + kr_rot).astype(BF16)
    vt = _dot(ckv, wuv_ref[...]).T.astype(BF16)
    ones_row = (lax.broadcasted_iota(jnp.int32, (V_AUG - V_HEAD, vt.shape[1]), 0) == 0).astype(BF16)
    for h in range(MLA_HEADS):
        vt_ref[h, 0:V_HEAD, :] = vt[V_HEAD * h:V_HEAD * (h + 1), :]
        vt_ref[h, V_HEAD:V_AUG, :] = ones_row


def _qkv(x, pos3, ng, wa, qg, wuq, kvg, wuk, wuv, invf):
    b, s, d = x.shape
    ts = TS_QKV
    const = lambda shape: pl.BlockSpec(shape, lambda i, j: (0,) * len(shape))
    hp = MLA_HEADS * HEAD_PAD
    return pl.pallas_call(
        _qkv_kernel,
        grid=(b, s // ts),
        in_specs=[pl.BlockSpec((None, ts, d), lambda i, j: (i, j, 0)),
                  pl.BlockSpec((None, 1, ts), lambda i, j: (i, 0, j)),
                  const((1, d)), const(wa.shape), const((1, Q_LORA)), const(wuq.shape),
                  const((1, KV_LORA)), const(wuk.shape), const(wuv.shape), const((ROPE_HALF, 1))],
        out_specs=[pl.BlockSpec((None, MLA_HEADS, HEAD_PAD, ts), lambda i, j: (i, 0, 0, j)),
                   pl.BlockSpec((None, ts, hp), lambda i, j: (i, j, 0)),
                   pl.BlockSpec((None, MLA_HEADS, V_AUG, ts), lambda i, j: (i, 0, 0, j)),
                   pl.BlockSpec((None, ts, d), lambda i, j: (i, j, 0))],
        out_shape=[jax.ShapeDtypeStruct((b, MLA_HEADS, HEAD_PAD, s), BF16),
                   jax.ShapeDtypeStruct((b, s, hp), BF16),
                   jax.ShapeDtypeStruct((b, MLA_HEADS, V_AUG, s), BF16),
                   jax.ShapeDtypeStruct((b, s, d), BF16)],
        compiler_params=pltpu.CompilerParams(dimension_semantics=("arbitrary", "arbitrary"),
                                             vmem_limit_bytes=VMEM_LIMIT),
        name="qkv",
    )(x, pos3, ng, wa, qg, wuq, kvg, wuk, wuv, invf)


ATTN_SLOTS = 4
ATTN_UNROLL = 4
ATTN_STRIP = 256
MAX_CHAINS = 4


def _attn_schedule(n_tiles):
    below = [(qi, j) for qi in range(n_tiles) for j in range(qi)]
    diag = [(qi, qi) for qi in range(n_tiles)]
    assert len(below) % ATTN_UNROLL == 0 and len(diag) % ATTN_UNROLL == 0
    table = np.array(below + diag + [(0, 0)], np.int32).T
    return table, len(below), len(diag)


def _attn_kernel(n_below, n_diag, tbl_ref, qt_ref, k_ref, vt_ref, o_ref,
                 s_scr, mblk_scr, m_scr, acc_scr):
    heads = (0, 1)
    units = [(hh, c) for hh in heads for c in range(TQ // ATTN_STRIP)]

    def key_rows(c, diag):
        return min(TK, (c + 1) * ATTN_STRIP) if diag else TK

    def scores(f, slot, hh, c, diag):
        q0 = pl.multiple_of(tbl_ref[0, f] * TQ + c * ATTN_STRIP, ATTN_STRIP)
        k0 = pl.multiple_of(tbl_ref[1, f] * TK, TK)
        nk = key_rows(c, diag)
        k = k_ref[pl.ds(k0, nk), HEAD_PAD * hh:HEAD_PAD * (hh + 1)]
        s = _dot(k, qt_ref[hh, :, pl.ds(q0, ATTN_STRIP)])
        if diag:
            key = lax.broadcasted_iota(jnp.int32, (nk, ATTN_STRIP), 0)
            qry = lax.broadcasted_iota(jnp.int32, (nk, ATTN_STRIP), 1) + c * ATTN_STRIP
            s = jnp.where(key <= qry, s, NEG)
        s_scr[slot, hh, c, 0:nk, :] = s
        tiles = s.reshape(MAX_CHAINS, nk // (MAX_CHAINS * SUBLANES), SUBLANES, ATTN_STRIP)
        mblk_scr[slot, hh, c] = jnp.max(jnp.max(jnp.max(tiles, axis=1), axis=0),
                                        axis=0, keepdims=True)

    def softmax(f, slot, hh, c, diag):
        qi = tbl_ref[0, f]
        nk = key_rows(c, diag)
        m_old = m_scr[qi, hh, c]
        m_new = jnp.maximum(m_old, mblk_scr[slot, hh, c])
        m_scr[qi, hh, c] = m_new
        alpha = jnp.exp2(m_old - m_new)
        p = jnp.exp2((s_scr[slot, hh, c, 0:nk, :] - m_new).astype(BF16))
        return alpha, p

    def values(f, hh, c, diag, alpha, p):
        qi = tbl_ref[0, f]
        k0 = pl.multiple_of(tbl_ref[1, f] * TK, TK)
        vt = vt_ref[hh, :, pl.ds(k0, key_rows(c, diag))]
        acc_scr[qi, hh, c] = alpha * acc_scr[qi, hh, c] + _dot(vt, p)

    def trips(first, count, unroll, diag):
        assert count % unroll == 0 and (count == unroll or unroll % ATTN_SLOTS == 0)
        if diag and first > 0:
            for hh, c in units:
                scores(first, first % ATTN_SLOTS, hh, c, diag)

        @pl.loop(0, count // unroll)
        def _(t):
            f = first + t * unroll
            for u in range(unroll):
                slot, nxt = (first + u) % ATTN_SLOTS, (first + u + 1) % ATTN_SLOTS
                for hh, c in units:
                    alpha, p = softmax(f + u, slot, hh, c, diag)
                    scores(f + u + 1, nxt, hh, c, diag)
                    values(f + u, hh, c, diag, alpha, p)

    m_scr[...] = jnp.full(m_scr.shape, NEG, F32)
    acc_scr[...] = jnp.zeros(acc_scr.shape, F32)
    for hh, c in units:
        scores(0, 0, hh, c, n_below == 0)
    trips(0, n_below, ATTN_UNROLL, diag=False)
    trips(n_below, n_diag, ATTN_UNROLL, diag=True)

    @pl.loop(0, n_diag)
    def _(qi):
        ot = jnp.concatenate(
            [jnp.concatenate([acc_scr[qi, hh, c, 0:V_HEAD] / acc_scr[qi, hh, c, V_HEAD:V_HEAD + 1]
                              for c in range(TQ // ATTN_STRIP)], axis=1) for hh in heads],
            axis=0)
        o_ref[pl.ds(pl.multiple_of(qi * TQ, TQ), TQ), :] = ot.T.astype(BF16)


def _mla_attn(qt, kp, vt):
    b, s, _ = kp.shape
    pairs = MLA_HEADS // 2
    n_tiles = s // TQ
    n_strips = TQ // ATTN_STRIP
    table, n_below, n_diag = _attn_schedule(n_tiles)
    grid_spec = pltpu.PrefetchScalarGridSpec(
        num_scalar_prefetch=1,
        grid=(b, pairs),
        in_specs=[pl.BlockSpec((None, 2, HEAD_PAD, s), lambda i, p, tbl: (i, p, 0, 0)),
                  pl.BlockSpec((None, s, 2 * HEAD_PAD), lambda i, p, tbl: (i, 0, p)),
                  pl.BlockSpec((None, 2, V_AUG, s), lambda i, p, tbl: (i, p, 0, 0))],
        out_specs=pl.BlockSpec((None, s, LANES), lambda i, p, tbl: (i, 0, p)),
        scratch_shapes=[pltpu.VMEM((ATTN_SLOTS, 2, n_strips, TK, ATTN_STRIP), F32),
                        pltpu.VMEM((ATTN_SLOTS, 2, n_strips, 1, ATTN_STRIP), F32),
                        pltpu.VMEM((n_tiles, 2, n_strips, 1, ATTN_STRIP), F32),
                        pltpu.VMEM((n_tiles, 2, n_strips, V_AUG, ATTN_STRIP), F32)])
    return pl.pallas_call(
        functools.partial(_attn_kernel, n_below, n_diag),
        grid_spec=grid_spec,
        out_shape=jax.ShapeDtypeStruct((b, s, MLA_WIDTH), BF16),
        compiler_params=pltpu.CompilerParams(dimension_semantics=("arbitrary", "arbitrary"),
                                             vmem_limit_bytes=VMEM_LIMIT),
        name="mla_attn",
    )(jnp.asarray(table), qt, kp, vt)


_C_VAL, _C_GLU, _C_GATE, _M_GATE, _X_Q, _X_GATE, _G0 = (
    0, CONV_WIDTH, 2 * CONV_WIDTH, 3 * CONV_WIDTH, 3 * CONV_WIDTH + MLA_WIDTH,
    3 * CONV_WIDTH + MLA_WIDTH + X_WIDTH, 3 * CONV_WIDTH + MLA_WIDTH + 2 * X_WIDTH)
W_B_COLS = _G0 + 3 * D_MODEL


def _block_kernel(x_ref, hb_ref, o_mla_ref, kx_ref, vx_ref, wc_ref, wr_ref, bg_ref, cw_ref, cb_ref,
                  lng_ref, lnb_ref, wco_ref, wmo_ref, wxo_ref, wout_ref, fg_ref,
                  out_ref, ubuf, ush, conv_scr, part_scr, cgate_scr, g0_scr):
    ts = TS_BLK

    def proj(lo, width):
        if lo < _M_GATE:
            return _dot(hb_ref[...], wc_ref[:, lo:lo + width])
        return _dot(hb_ref[...], wr_ref[:, lo - _M_GATE:lo - _M_GATE + width])

    def gate(i, lo, width):
        c0 = i * D_MODEL + lo
        return _twice_sigmoid_of_double(proj(_G0 + c0, width) + bg_ref[:, c0:c0 + width])

    @pl.when(pl.program_id(1) == 0)
    def _():
        ubuf[0:HALO, :] = jnp.zeros((HALO, CONV_WIDTH), F32)

    ubuf[HALO:HALO + ts, :] = (proj(_C_VAL, CONV_WIDTH)
                               * _twice_sigmoid_of_double(proj(_C_GLU, CONV_WIDTH)))
    for r in range(1, SUBLANES):
        ush[r - 1] = ubuf[r:r + SHIFT_ROWS, :]
    first = HALO - (CONV_KERNEL - 1)

    def conv_chunk(i):
        r0 = i * CONV_ROWS
        groups = CONV_ROWS // SUBLANES
        acc = jnp.zeros((groups, SUBLANES, CONV_WIDTH), F32) + cb_ref[...]
        for kk in range(CONV_KERNEL):
            r = (first + kk) % SUBLANES
            row = r0 + first + kk - r
            tap = ubuf[row:row + CONV_ROWS, :] if r == 0 else ush[r - 1, row:row + CONV_ROWS, :]
            w = cw_ref[kk * SUBLANES:(kk + 1) * SUBLANES, :]
            acc = acc + tap.reshape(groups, SUBLANES, CONV_WIDTH) * w
        conv_scr[r0:r0 + CONV_ROWS, :] = acc.reshape(CONV_ROWS, CONV_WIDTH)

    st = {}
    half = D_MODEL // 2

    def mla_gate():
        gate_m = _silu_of_double(proj(_M_GATE, MLA_WIDTH))
        st["a_mla"] = (o_mla_ref[...].astype(F32) * gate_m).astype(BF16)

    def mla_out():
        st["y_mla"] = _dot(st["a_mla"], wmo_ref[...])

    def mla_merge(lo):
        part_scr[:, lo:lo + half] = gate(1, lo, half) * st["y_mla"][:, lo:lo + half]

    def x_query():
        st["xq"] = proj(_X_Q, X_WIDTH).astype(BF16)
        st["ox"] = []

    def x_heads(h0):
        for h in (h0, h0 + 1):
            hs = slice(X_HEAD_DIM * h, X_HEAD_DIM * (h + 1))
            s = _dot_nt(st["xq"][:, hs], kx_ref[:, hs]) * X_SCALE
            p = jnp.exp(s - jnp.max(s, axis=-1, keepdims=True))
            p = p / jnp.sum(p, axis=-1, keepdims=True)
            st["ox"].append(_dot(p.astype(BF16), vx_ref[:, hs]))

    def x_gate():
        st["xg"] = _silu_of_double(proj(_X_GATE, X_WIDTH))

    def x_out():
        a_x = jnp.concatenate(st["ox"], axis=-1) * st["xg"]
        st["y_x"] = _dot(a_x.astype(BF16), wxo_ref[...])

    def x_merge(lo):
        part_scr[:, lo:lo + half] += gate(2, lo, half) * st["y_x"][:, lo:lo + half]

    def conv_gate():
        cgate_scr[...] = _silu_of_double(proj(_C_GATE, CONV_WIDTH))

    def conv_merge_gate(lo):
        g0_scr[:, lo:lo + half] = gate(0, lo, half)

    side = [mla_gate, mla_out, functools.partial(mla_merge, 0), functools.partial(mla_merge, half),
            x_query, functools.partial(x_heads, 0), functools.partial(x_heads, 2), x_gate, x_out,
            functools.partial(x_merge, 0), functools.partial(x_merge, half), conv_gate,
            functools.partial(conv_merge_gate, 0), functools.partial(conv_merge_gate, half)]
    n_chunks = ts // CONV_ROWS
    for i in range(n_chunks):
        conv_chunk(i)
        for j, piece in enumerate(side):
            if j * n_chunks // len(side) == i:
                piece()
    ubuf[0:HALO, :] = ubuf[ts:ts + HALO, :]

    conv = conv_scr[...]
    mu = jnp.mean(conv, axis=-1, keepdims=True)
    cen = conv - mu
    var = jnp.mean(cen * cen, axis=-1, keepdims=True)
    half_ln = cen * lax.rsqrt(var + EPS) * lng_ref[...] + lnb_ref[...]
    a_conv = _silu_of_double(half_ln) * cgate_scr[...]
    merged = g0_scr[...] * _dot(a_conv.astype(BF16), wco_ref[...]) + part_scr[...]
    y = x_ref[...] + _dot(merged.astype(BF16), wout_ref[...])
    out_ref[...] = _rms(y, fg_ref[...])


def _block(x, hb, o_mla, kx, vx, w_in_bf, wr, bg, cw, cb, lng, lnb, wco, wmo, wxo, wout, fg):
    b, s, d = x.shape
    ts = TS_BLK
    const = lambda shape: pl.BlockSpec(shape, lambda i, j: (0,) * len(shape),
                                       pipeline_mode=pl.Buffered(1))
    return pl.pallas_call(
        _block_kernel,
        grid=(b, s // ts),
        in_specs=[pl.BlockSpec((None, ts, d), lambda i, j: (i, j, 0)),
                  pl.BlockSpec((None, ts, d), lambda i, j: (i, j, 0)),
                  pl.BlockSpec((None, ts, MLA_WIDTH), lambda i, j: (i, j, 0)),
                  pl.BlockSpec((None, N_MEM, X_WIDTH), lambda i, j: (i, 0, 0)),
                  pl.BlockSpec((None, N_MEM, X_WIDTH), lambda i, j: (i, 0, 0)),
                  const((d, _M_GATE)), const(wr.shape), const(bg.shape),
                  const(cw.shape),
                  const((1, CONV_WIDTH)), const((1, CONV_WIDTH)), const((1, CONV_WIDTH)),
                  const(wco.shape), const(wmo.shape), const(wxo.shape), const(wout.shape),
                  const((1, d))],
        out_specs=pl.BlockSpec((None, ts, d), lambda i, j: (i, j, 0)),
        out_shape=jax.ShapeDtypeStruct((b, s, d), F32),
        scratch_shapes=[pltpu.VMEM((HALO + ts, CONV_WIDTH), F32),
                        pltpu.VMEM((SUBLANES - 1, SHIFT_ROWS, CONV_WIDTH), F32),
                        pltpu.VMEM((ts, CONV_WIDTH), F32),
                        pltpu.VMEM((ts, d), F32),
                        pltpu.VMEM((ts, CONV_WIDTH), F32),
                        pltpu.VMEM((ts, d), F32)],
        compiler_params=pltpu.CompilerParams(dimension_semantics=("arbitrary", "arbitrary"),
                                             vmem_limit_bytes=VMEM_LIMIT),
        name="block",
    )(x, hb, o_mla, kx, vx, w_in_bf, wr, bg, cw, cb, lng, lnb, wco, wmo, wxo, wout, fg)


def _inv_freq_column():
    inv = ROPE_THETA ** (-jnp.arange(0, QK_ROPE, 2, dtype=F32) / QK_ROPE)
    return inv.reshape(ROPE_HALF, 1)


def kernel(x, mem, positions, norm_g, w_in, b_gate, conv_w, conv_b, conv_ln_g, conv_ln_b,
           w_conv_o, q_norm_g, w_uq, kv_norm_g, w_ukv, w_mla_o, mem_norm_g, w_mem_kv, w_x_o,
           w_out, final_norm_g):
    assert norm_g.shape[0] == 1, "single-layer trunk"
    b, s, d = x.shape
    row = lambda v: v.reshape(1, -1)
    edges = np.cumsum((0,) + IN_SPLITS)
    halved = np.ones((1, edges[-1]), np.float32)
    for i in (0, 1, 2, 6, 8, 9):
        halved[0, edges[i]:edges[i + 1]] = 0.5
    w_in_bf = (w_in[0] * halved).astype(BF16)
    (w_cval, w_cglu, w_cgate, w_qd, w_kvd, w_kr, w_mg, w_xq, w_xg, w_g) = (
        w_in_bf[:, edges[i]:edges[i + 1]] for i in range(len(IN_SPLITS)))

    w_kr_pad = jnp.pad(w_kr, ((0, 0), (ROPE_LO, LANES - ROPE_HI)))
    wa = jnp.concatenate([w_qd, w_kvd, w_kr_pad], axis=1)
    assert edges[3] == _M_GATE and edges[-1] - edges[6] == W_B_COLS - _M_GATE
    wr = w_in_bf[:, edges[6]:]
    wuq = jnp.pad(w_uq[0].reshape(Q_LORA, MLA_HEADS, QK_NOPE + QK_ROPE),
                  ((0, 0), (0, 0), (0, HEAD_PAD - QK_NOPE - QK_ROPE)))
    wuq = wuq.reshape(Q_LORA, MLA_HEADS * HEAD_PAD).astype(BF16)
    wukv = w_ukv[0].reshape(KV_LORA, MLA_HEADS, QK_NOPE + V_HEAD)
    wuk = jnp.pad(wukv[:, :, :QK_NOPE], ((0, 0), (0, 0), (0, HEAD_PAD - QK_NOPE)))
    wuk = wuk.reshape(KV_LORA, MLA_HEADS * HEAD_PAD).astype(BF16)
    wuv = wukv[:, :, QK_NOPE:].reshape(KV_LORA, MLA_WIDTH).astype(BF16)

    kx, vx = _mem_kv(mem, row(mem_norm_g[0]), w_mem_kv[0].astype(BF16))
    qt, kp, vt, hb = _qkv(x, positions.reshape(b, 1, s), row(norm_g[0]), wa, row(q_norm_g[0]),
                          wuq, row(kv_norm_g[0]), wuk, wuv, _inv_freq_column())
    o_mla = _mla_attn(qt, kp, vt)
    half_bf = lambda w: (0.5 * w).astype(BF16)
    return _block(x, hb, o_mla, kx, vx, w_in_bf, wr, row(0.5 * b_gate[0]),
                  jnp.repeat(conv_w[0], SUBLANES, axis=0),
                  row(conv_b[0]), row(0.5 * conv_ln_g[0]), row(0.5 * conv_ln_b[0]),
                  half_bf(w_conv_o[0]), half_bf(w_mla_o[0]), half_bf(w_x_o[0]),
                  w_out[0].astype(BF16), row(final_norm_g))
```

```python
import functools

import jax
import jax.numpy as jnp
import numpy as np
from jax import lax
from jax.experimental import pallas as pl
from jax.experimental.pallas import tpu as pltpu

F32 = jnp.float32
BF16 = jnp.bfloat16

D_MODEL = 1024
N_MEM = 256
CONV_WIDTH = 512
CONV_KERNEL = 31
MLA_HEADS = 8
QK_NOPE = 64
QK_ROPE = 32
V_HEAD = 64
Q_LORA = 384
KV_LORA = 256
MLA_WIDTH = MLA_HEADS * V_HEAD
X_HEADS = 4
X_HEAD_DIM = 128
X_WIDTH = X_HEADS * X_HEAD_DIM
ROPE_THETA = 10000.0
EPS = 1e-6
IN_SPLITS = (CONV_WIDTH, CONV_WIDTH, CONV_WIDTH, Q_LORA, KV_LORA, QK_ROPE, MLA_WIDTH,
             X_WIDTH, X_WIDTH, 3 * D_MODEL)

LANES = 128
HEAD_PAD = LANES
BF16_ROWS = 16
V_AUG = V_HEAD + BF16_ROWS
ROPE_LO = QK_NOPE
ROPE_HALF = QK_ROPE // 2
ROPE_HI = QK_NOPE + QK_ROPE
MLA_SCALE = (QK_NOPE + QK_ROPE) ** -0.5
X_SCALE = X_HEAD_DIM ** -0.5
LOG2E = float(np.log2(np.e))
NEG = float(np.finfo(np.float32).min)

TS_QKV = 512
TQ = 512
TK = 512
TS_BLK = 512
HALO = 32
CONV_ROWS = 32
SUBLANES = 8
SHIFT_ROWS = TS_BLK + HALO - SUBLANES
VMEM_LIMIT = 56 * 1024 * 1024


def _rms(x, g):
    return x * lax.rsqrt(jnp.mean(x * x, axis=-1, keepdims=True) + EPS) * g


def _twice_sigmoid_of_double(h):
    return jnp.tanh(h) + 1.0


def _silu_of_double(h):
    return h * _twice_sigmoid_of_double(h)


def _dot(a, b):
    return jnp.dot(a, b, preferred_element_type=F32)


def _dot_nt(a, b):
    return lax.dot_general(a, b, (((1,), (1,)), ((), ())), preferred_element_type=F32)


def _mem_kv_kernel(mem_ref, g_ref, w_ref, k_ref, v_ref):
    hb = _rms(mem_ref[...], g_ref[...]).astype(BF16)
    kv = _dot(hb, w_ref[...])
    k_ref[...] = kv[:, :X_WIDTH].astype(BF16)
    v_ref[...] = kv[:, X_WIDTH:].astype(BF16)


def _mem_kv(mem, g, w):
    b, m, d = mem.shape
    return pl.pallas_call(
        _mem_kv_kernel,
        grid=(b,),
        in_specs=[pl.BlockSpec((None, m, d), lambda i: (i, 0, 0)),
                  pl.BlockSpec((1, d), lambda i: (0, 0)),
                  pl.BlockSpec((d, 2 * X_WIDTH), lambda i: (0, 0))],
        out_specs=[pl.BlockSpec((None, m, X_WIDTH), lambda i: (i, 0, 0)),
                   pl.BlockSpec((None, m, X_WIDTH), lambda i: (i, 0, 0))],
        out_shape=[jax.ShapeDtypeStruct((b, m, X_WIDTH), BF16)] * 2,
        compiler_params=pltpu.CompilerParams(dimension_semantics=("arbitrary",)),
        name="mem_kv",
    )(mem, g, w)


def _qkv_kernel(x_ref, pos_ref, ng_ref, wa_ref, qg_ref, wuq_ref, kvg_ref, wuk_ref, wuv_ref,
                invf_ref, qt_ref, k_ref, vt_ref, hb_ref):
    hb = _rms(x_ref[...], ng_ref[...]).astype(BF16)
    hb_ref[...] = hb
    za = _dot(hb, wa_ref[...])
    cq = _rms(za[:, :Q_LORA], qg_ref[...]).astype(BF16)
    ckv = _rms(za[:, Q_LORA:Q_LORA + KV_LORA], kvg_ref[...]).astype(BF16)
    kr = za[:, Q_LORA + KV_LORA:]

    ang = invf_ref[...] * pos_ref[...].astype(F32)
    cos, sin = jnp.cos(ang), jnp.sin(ang)

    def rope_t(t):
        x1 = t[ROPE_LO:ROPE_LO + ROPE_HALF]
        x2 = t[ROPE_LO + ROPE_HALF:ROPE_HI]
        return jnp.concatenate(
            [t[:ROPE_LO], x1 * cos - x2 * sin, x1 * sin + x2 * cos, t[ROPE_HI:]], axis=0)

    kr_rot = rope_t(kr.T).T
    q = _dot(cq, wuq_ref[...])
    kn = _dot(ckv, wuk_ref[...])
    for h in range(MLA_HEADS):
        sl = slice(HEAD_PAD * h, HEAD_PAD * (h + 1))
        qt_ref[h] = (rope_t(q[:, sl].T) * (MLA_SCALE * LOG2E)).astype(BF16)
        k_ref[:, sl] = (kn[:, sl] + kr_rot).astype(BF16)
    vt = _dot(ckv, wuv_ref[...]).T.astype(BF16)
    ones_row = (lax.broadcasted_iota(jnp.int32, (V_AUG - V_HEAD, vt.shape[1]), 0) == 0).astype(BF16)
    for h in range(MLA_HEADS):
        vt_ref[h, 0:V_HEAD, :] = vt[V_HEAD * h:V_HEAD * (h + 1), :]
        vt_ref[h, V_HEAD:V_AUG, :] = ones_row


def _qkv(x, pos3, ng, wa, qg, wuq, kvg, wuk, wuv, invf):
    b, s, d = x.shape
    ts = TS_QKV
    const = lambda shape: pl.BlockSpec(shape, lambda i, j: (0,) * len(shape))
    hp = MLA_HEADS * HEAD_PAD
    return pl.pallas_call(
        _qkv_kernel,
        grid=(b, s // ts),
        in_specs=[pl.BlockSpec((None, ts, d), lambda i, j: (i, j, 0)),
                  pl.BlockSpec((None, 1, ts), lambda i, j: (i, 0, j)),
                  const((1, d)), const(wa.shape), const((1, Q_LORA)), const(wuq.shape),
                  const((1, KV_LORA)), const(wuk.shape), const(wuv.shape), const((ROPE_HALF, 1))],
        out_specs=[pl.BlockSpec((None, MLA_HEADS, HEAD_PAD, ts), lambda i, j: (i, 0, 0, j)),
                   pl.BlockSpec((None, ts, hp), lambda i, j: (i, j, 0)),
                   pl.BlockSpec((None, MLA_HEADS, V_AUG, ts), lambda i, j: (i, 0, 0, j)),
                   pl.BlockSpec((None, ts, d), lambda i, j: (i, j, 0))],
        out_shape=[jax.ShapeDtypeStruct((b, MLA_HEADS, HEAD_PAD, s), BF16),
                   jax.ShapeDtypeStruct((b, s, hp), BF16),
                   jax.ShapeDtypeStruct((b, MLA_HEADS, V_AUG, s), BF16),
                   jax.ShapeDtypeStruct((b, s, d), BF16)],
        compiler_params=pltpu.CompilerParams(dimension_semantics=("arbitrary", "arbitrary"),
                                             vmem_limit_bytes=VMEM_LIMIT),
        name="qkv",
    )(x, pos3, ng, wa, qg, wuq, kvg, wuk, wuv, invf)


ATTN_SLOTS = 4
ATTN_UNROLL = 4
ATTN_STRIP = 256
MAX_CHAINS = 4


def _attn_schedule(n_tiles):
    below = [(qi, j) for qi in range(n_tiles) for j in range(qi)]
    diag = [(qi, qi) for qi in range(n_tiles)]
    assert len(below) % ATTN_UNROLL == 0 and len(diag) % ATTN_UNROLL == 0
    table = np.array(below + diag + [(0, 0)], np.int32).T
    return table, len(below), len(diag)


def _attn_kernel(n_below, n_diag, tbl_ref, qt_ref, k_ref, vt_ref, o_ref,
                 s_scr, mblk_scr, m_scr, acc_scr):
    heads = (0, 1)
    units = [(hh, c) for hh in heads for c in range(TQ // ATTN_STRIP)]

    def key_rows(c, diag):
        return min(TK, (c + 1) * ATTN_STRIP) if diag else TK

    def scores(f, slot, hh, c, diag):
        q0 = pl.multiple_of(tbl_ref[0, f] * TQ + c * ATTN_STRIP, ATTN_STRIP)
        k0 = pl.multiple_of(tbl_ref[1, f] * TK, TK)
        nk = key_rows(c, diag)
        k = k_ref[pl.ds(k0, nk), HEAD_PAD * hh:HEAD_PAD * (hh + 1)]
        s = _dot(k, qt_ref[hh, :, pl.ds(q0, ATTN_STRIP)])
        if diag:
            key = lax.broadcasted_iota(jnp.int32, (nk, ATTN_STRIP), 0)
            qry = lax.broadcasted_iota(jnp.int32, (nk, ATTN_STRIP), 1) + c * ATTN_STRIP
            s = jnp.where(key <= qry, s, NEG)
        s_scr[slot, hh, c, 0:nk, :] = s
        tiles = s.reshape(MAX_CHAINS, nk // (MAX_CHAINS * SUBLANES), SUBLANES, ATTN_STRIP)
        mblk_scr[slot, hh, c] = jnp.max(jnp.max(jnp.max(tiles, axis=1), axis=0),
                                        axis=0, keepdims=True)

    def softmax(f, slot, hh, c, diag):
        qi = tbl_ref[0, f]
        nk = key_rows(c, diag)
        m_old = m_scr[qi, hh, c]
        m_new = jnp.maximum(m_old, mblk_scr[slot, hh, c])
        m_scr[qi, hh, c] = m_new
        alpha = jnp.exp2(m_old - m_new)
        p = jnp.exp2((s_scr[slot, hh, c, 0:nk, :] - m_new).astype(BF16))
        return alpha, p

    def values(f, hh, c, diag, alpha, p):
        qi = tbl_ref[0, f]
        k0 = pl.multiple_of(tbl_ref[1, f] * TK, TK)
        vt = vt_ref[hh, :, pl.ds(k0, key_rows(c, diag))]
        acc_scr[qi, hh, c] = alpha * acc_scr[qi, hh, c] + _dot(vt, p)

    def trips(first, count, unroll, diag):
        assert count % unroll == 0 and (count == unroll or unroll % ATTN_SLOTS == 0)
        if diag and first > 0:
            for hh, c in units:
                scores(first, first % ATTN_SLOTS, hh, c, diag)

        @pl.loop(0, count // unroll)
        def _(t):
            f = first + t * unroll
            for u in range(unroll):
                slot, nxt = (first + u) % ATTN_SLOTS, (first + u + 1) % ATTN_SLOTS
                for hh, c in units:
                    alpha, p = softmax(f + u, slot, hh, c, diag)
                    scores(f + u + 1, nxt, hh, c, diag)
                    values(f + u, hh, c, diag, alpha, p)

    m_scr[...] = jnp.full(m_scr.shape, NEG, F32)
    acc_scr[...] = jnp.zeros(acc_scr.shape, F32)
    for hh, c in units:
        scores(0, 0, hh, c, n_below == 0)
    trips(0, n_below, ATTN_UNROLL, diag=False)
    trips(n_below, n_diag, ATTN_UNROLL, diag=True)

    @pl.loop(0, n_diag)
    def _(qi):
        ot = jnp.concatenate(
            [jnp.concatenate([acc_scr[qi, hh, c, 0:V_HEAD] / acc_scr[qi, hh, c, V_HEAD:V_HEAD + 1]
                              for c in range(TQ // ATTN_STRIP)], axis=1) for hh in heads],
            axis=0)
        o_ref[pl.ds(pl.multiple_of(qi * TQ, TQ), TQ), :] = ot.T.astype(BF16)


def _mla_attn(qt, kp, vt):
    b, s, _ = kp.shape
    pairs = MLA_HEADS // 2
    n_tiles = s // TQ
    n_strips = TQ // ATTN_STRIP
    table, n_below, n_diag = _attn_schedule(n_tiles)
    grid_spec = pltpu.PrefetchScalarGridSpec(
        num_scalar_prefetch=1,
        grid=(b, pairs),
        in_specs=[pl.BlockSpec((None, 2, HEAD_PAD, s), lambda i, p, tbl: (i, p, 0, 0)),
                  pl.BlockSpec((None, s, 2 * HEAD_PAD), lambda i, p, tbl: (i, 0, p)),
                  pl.BlockSpec((None, 2, V_AUG, s), lambda i, p, tbl: (i, p, 0, 0))],
        out_specs=pl.BlockSpec((None, s, LANES), lambda i, p, tbl: (i, 0, p)),
        scratch_shapes=[pltpu.VMEM((ATTN_SLOTS, 2, n_strips, TK, ATTN_STRIP), F32),
                        pltpu.VMEM((ATTN_SLOTS, 2, n_strips, 1, ATTN_STRIP), F32),
                        pltpu.VMEM((n_tiles, 2, n_strips, 1, ATTN_STRIP), F32),
                        pltpu.VMEM((n_tiles, 2, n_strips, V_AUG, ATTN_STRIP), F32)])
    return pl.pallas_call(
        functools.partial(_attn_kernel, n_below, n_diag),
        grid_spec=grid_spec,
        out_shape=jax.ShapeDtypeStruct((b, s, MLA_WIDTH), BF16),
        compiler_params=pltpu.CompilerParams(dimension_semantics=("arbitrary", "arbitrary"),
                                             vmem_limit_bytes=VMEM_LIMIT),
        name="mla_attn",
    )(jnp.asarray(table), qt, kp, vt)


_C_VAL, _C_GLU, _C_GATE, _M_GATE, _X_Q, _X_GATE, _G0 = (
    0, CONV_WIDTH, 2 * CONV_WIDTH, 3 * CONV_WIDTH, 3 * CONV_WIDTH + MLA_WIDTH,
    3 * CONV_WIDTH + MLA_WIDTH + X_WIDTH, 3 * CONV_WIDTH + MLA_WIDTH + 2 * X_WIDTH)
W_B_COLS = _G0 + 3 * D_MODEL


def _block_kernel(x_ref, hb_ref, o_mla_ref, kx_ref, vx_ref, wc_ref, wr_ref, bg_ref, cw_ref, cb_ref,
                  lng_ref, lnb_ref, wco_ref, wmo_ref, wxo_ref, wout_ref, fg_ref,
                  out_ref, ubuf, ush, conv_scr, part_scr, cgate_scr, g0_scr):
    ts = TS_BLK

    def proj(lo, width):
        if lo < _M_GATE:
            return _dot(hb_ref[...], wc_ref[:, lo:lo + width])
        return _dot(hb_ref[...], wr_ref[:, lo - _M_GATE:lo - _M_GATE + width])

    def gate(i, lo, width):
        c0 = i * D_MODEL + lo
        return _twice_sigmoid_of_double(proj(_G0 + c0, width) + bg_ref[:, c0:c0 + width])

    @pl.when(pl.program_id(1) == 0)
    def _():
        ubuf[0:HALO, :] = jnp.zeros((HALO, CONV_WIDTH), F32)

    ubuf[HALO:HALO + ts, :] = (proj(_C_VAL, CONV_WIDTH)
                               * _twice_sigmoid_of_double(proj(_C_GLU, CONV_WIDTH)))
    for r in range(1, SUBLANES):
        ush[r - 1] = ubuf[r:r + SHIFT_ROWS, :]
    first = HALO - (CONV_KERNEL - 1)

    def conv_chunk(i):
        r0 = i * CONV_ROWS
        groups = CONV_ROWS // SUBLANES
        acc = jnp.zeros((groups, SUBLANES, CONV_WIDTH), F32) + cb_ref[...]
        for kk in range(CONV_KERNEL):
            r = (first + kk) % SUBLANES
            row = r0 + first + kk - r
            tap = ubuf[row:row + CONV_ROWS, :] if r == 0 else ush[r - 1, row:row + CONV_ROWS, :]
            w = cw_ref[kk * SUBLANES:(kk + 1) * SUBLANES, :]
            acc = acc + tap.reshape(groups, SUBLANES, CONV_WIDTH) * w
        conv_scr[r0:r0 + CONV_ROWS, :] = acc.reshape(CONV_ROWS, CONV_WIDTH)

    st = {}
    half = D_MODEL // 2

    def mla_gate():
        gate_m = _silu_of_double(proj(_M_GATE, MLA_WIDTH))
        st["a_mla"] = (o_mla_ref[...].astype(F32) * gate_m).astype(BF16)

    def mla_out():
        st["y_mla"] = _dot(st["a_mla"], wmo_ref[...])

    def mla_merge(lo):
        part_scr[:, lo:lo + half] = gate(1, lo, half) * st["y_mla"][:, lo:lo + half]

    def x_query():
        st["xq"] = proj(_X_Q, X_WIDTH).astype(BF16)
        st["ox"] = []

    def x_heads(h0):
        for h in (h0, h0 + 1):
            hs = slice(X_HEAD_DIM * h, X_HEAD_DIM * (h + 1))
            s = _dot_nt(st["xq"][:, hs], kx_ref[:, hs]) * X_SCALE
            p = jnp.exp(s - jnp.max(s, axis=-1, keepdims=True))
            p = p / jnp.sum(p, axis=-1, keepdims=True)
            st["ox"].append(_dot(p.astype(BF16), vx_ref[:, hs]))

    def x_gate():
        st["xg"] = _silu_of_double(proj(_X_GATE, X_WIDTH))

    def x_out():
        a_x = jnp.concatenate(st["ox"], axis=-1) * st["xg"]
        st["y_x"] = _dot(a_x.astype(BF16), wxo_ref[...])

    def x_merge(lo):
        part_scr[:, lo:lo + half] += gate(2, lo, half) * st["y_x"][:, lo:lo + half]

    def conv_gate():
        cgate_scr[...] = _silu_of_double(proj(_C_GATE, CONV_WIDTH))

    def conv_merge_gate(lo):
        g0_scr[:, lo:lo + half] = gate(0, lo, half)

    side = [mla_gate, mla_out, functools.partial(mla_merge, 0), functools.partial(mla_merge, half),
            x_query, functools.partial(x_heads, 0), functools.partial(x_heads, 2), x_gate, x_out,
            functools.partial(x_merge, 0), functools.partial(x_merge, half), conv_gate,
            functools.partial(conv_merge_gate, 0), functools.partial(conv_merge_gate, half)]
    n_chunks = ts // CONV_ROWS
    for i in range(n_chunks):
        conv_chunk(i)
        for j, piece in enumerate(side):
            if j * n_chunks // len(side) == i:
                piece()
    ubuf[0:HALO, :] = ubuf[ts:ts + HALO, :]

    conv = conv_scr[...]
    mu = jnp.mean(conv, axis=-1, keepdims=True)
    cen = conv - mu
    var = jnp.mean(cen * cen, axis=-1, keepdims=True)
    half_ln = cen * lax.rsqrt(var + EPS) * lng_ref[...] + lnb_ref[...]
    a_conv = _silu_of_double(half_ln) * cgate_scr[...]
    merged = g0_scr[...] * _dot(a_conv.astype(BF16), wco_ref[...]) + part_scr[...]
    y = x_ref[...] + _dot(merged.astype(BF16), wout_ref[...])
    out_ref[...] = _rms(y, fg_ref[...])


def _block(x, hb, o_mla, kx, vx, wc, wr, bg, cw, cb, lng, lnb, wco, wmo, wxo, wout, fg):
    b, s, d = x.shape
    ts = TS_BLK
    const = lambda shape: pl.BlockSpec(shape, lambda i, j: (0,) * len(shape),
                                       pipeline_mode=pl.Buffered(1))
    return pl.pallas_call(
        _block_kernel,
        grid=(b, s // ts),
        in_specs=[pl.BlockSpec((None, ts, d), lambda i, j: (i, j, 0)),
                  pl.BlockSpec((None, ts, d), lambda i, j: (i, j, 0)),
                  pl.BlockSpec((None, ts, MLA_WIDTH), lambda i, j: (i, j, 0)),
                  pl.BlockSpec((None, N_MEM, X_WIDTH), lambda i, j: (i, 0, 0)),
                  pl.BlockSpec((None, N_MEM, X_WIDTH), lambda i, j: (i, 0, 0)),
                  const(wc.shape), const(wr.shape), const(bg.shape),
                  const(cw.shape),
                  const((1, CONV_WIDTH)), const((1, CONV_WIDTH)), const((1, CONV_WIDTH)),
                  const(wco.shape), const(wmo.shape), const(wxo.shape), const(wout.shape),
                  const((1, d))],
        out_specs=pl.BlockSpec((None, ts, d), lambda i, j: (i, j, 0)),
        out_shape=jax.ShapeDtypeStruct((b, s, d), F32),
        scratch_shapes=[pltpu.VMEM((HALO + ts, CONV_WIDTH), F32),
                        pltpu.VMEM((SUBLANES - 1, SHIFT_ROWS, CONV_WIDTH), F32),
                        pltpu.VMEM((ts, CONV_WIDTH), F32),
                        pltpu.VMEM((ts, d), F32),
                        pltpu.VMEM((ts, CONV_WIDTH), F32),
                        pltpu.VMEM((ts, d), F32)],
        compiler_params=pltpu.CompilerParams(dimension_semantics=("arbitrary", "arbitrary"),
                                             vmem_limit_bytes=VMEM_LIMIT),
        name="block",
    )(x, hb, o_mla, kx, vx, wc, wr, bg, cw, cb, lng, lnb, wco, wmo, wxo, wout, fg)


def _inv_freq_column():
    inv = ROPE_THETA ** (-jnp.arange(0, QK_ROPE, 2, dtype=F32) / QK_ROPE)
    return inv.reshape(ROPE_HALF, 1)


def kernel(x, mem, positions, norm_g, w_in, b_gate, conv_w, conv_b, conv_ln_g, conv_ln_b,
           w_conv_o, q_norm_g, w_uq, kv_norm_g, w_ukv, w_mla_o, mem_norm_g, w_mem_kv, w_x_o,
           w_out, final_norm_g):
    assert norm_g.shape[0] == 1, "single-layer trunk"
    b, s, d = x.shape
    row = lambda v: v.reshape(1, -1)
    edges = np.cumsum((0,) + IN_SPLITS)
    w_in_bf = w_in[0].astype(BF16)
    w_qd, w_kvd, w_kr = (w_in_bf[:, edges[i]:edges[i + 1]] for i in (3, 4, 5))

    w_kr_pad = jnp.pad(w_kr, ((0, 0), (ROPE_LO, LANES - ROPE_HI)))
    wa = jnp.concatenate([w_qd, w_kvd, w_kr_pad], axis=1)
    assert edges[3] == _M_GATE and edges[-1] - edges[6] == W_B_COLS - _M_GATE
    halved = np.full((1, edges[-1] - edges[6]), 0.5, np.float32)
    halved[0, edges[7] - edges[6]:edges[8] - edges[6]] = 1.0
    wc = w_in_bf[:, :edges[3]] * jnp.asarray(0.5, BF16)
    wr = w_in_bf[:, edges[6]:] * jnp.asarray(halved, BF16)
    wuq = jnp.pad(w_uq[0].reshape(Q_LORA, MLA_HEADS, QK_NOPE + QK_ROPE),
                  ((0, 0), (0, 0), (0, HEAD_PAD - QK_NOPE - QK_ROPE)))
    wuq = wuq.reshape(Q_LORA, MLA_HEADS * HEAD_PAD).astype(BF16)
    wukv = w_ukv[0].reshape(KV_LORA, MLA_HEADS, QK_NOPE + V_HEAD)
    wuk = jnp.pad(wukv[:, :, :QK_NOPE], ((0, 0), (0, 0), (0, HEAD_PAD - QK_NOPE)))
    wuk = wuk.reshape(KV_LORA, MLA_HEADS * HEAD_PAD).astype(BF16)
    wuv = wukv[:, :, QK_NOPE:].reshape(KV_LORA, MLA_WIDTH).astype(BF16)

    kx, vx = _mem_kv(mem, row(mem_norm_g[0]), w_mem_kv[0].astype(BF16))
    qt, kp, vt, hb = _qkv(x, positions.reshape(b, 1, s), row(norm_g[0]), wa, row(q_norm_g[0]),
                          wuq, row(kv_norm_g[0]), wuk, wuv, _inv_freq_column())
    o_mla = _mla_attn(qt, kp, vt)
    half_bf = lambda w: (0.5 * w).astype(BF16)
    return _block(x, hb, o_mla, kx, vx, wc, wr, row(0.5 * b_gate[0]),
                  jnp.repeat(conv_w[0], SUBLANES, axis=0),
                  row(conv_b[0]), row(0.5 * conv_ln_g[0]), row(0.5 * conv_ln_b[0]),
                  half_bf(w_conv_o[0]), half_bf(w_mla_o[0]), half_bf(w_x_o[0]),
                  w_out[0].astype(BF16), row(final_norm_g))
```

```python
import functools

import jax
import jax.numpy as jnp
import numpy as np
from jax import lax
from jax.experimental import pallas as pl
from jax.experimental.pallas import tpu as pltpu

F32 = jnp.float32
BF16 = jnp.bfloat16

D_MODEL = 1024
N_MEM = 256
CONV_WIDTH = 512
CONV_KERNEL = 31
MLA_HEADS = 8
QK_NOPE = 64
QK_ROPE = 32
V_HEAD = 64
Q_LORA = 384
KV_LORA = 256
MLA_WIDTH = MLA_HEADS * V_HEAD
X_HEADS = 4
X_HEAD_DIM = 128
X_WIDTH = X_HEADS * X_HEAD_DIM
ROPE_THETA = 10000.0
EPS = 1e-6
IN_SPLITS = (CONV_WIDTH, CONV_WIDTH, CONV_WIDTH, Q_LORA, KV_LORA, QK_ROPE, MLA_WIDTH,
             X_WIDTH, X_WIDTH, 3 * D_MODEL)

LANES = 128
HEAD_PAD = LANES
BF16_ROWS = 16
V_AUG = V_HEAD + BF16_ROWS
ROPE_LO = QK_NOPE
ROPE_HALF = QK_ROPE // 2
ROPE_HI = QK_NOPE + QK_ROPE
MLA_SCALE = (QK_NOPE + QK_ROPE) ** -0.5
X_SCALE = X_HEAD_DIM ** -0.5
LOG2E = float(np.log2(np.e))
NEG = float(np.finfo(np.float32).min)

TS_QKV = 512
TQ = 512
TK = 512
TS_BLK = 512
HALO = 32
CONV_ROWS = 32
SUBLANES = 8
SHIFT_ROWS = TS_BLK + HALO - SUBLANES
VMEM_LIMIT = 56 * 1024 * 1024


def _rms(x, g):
    return x * lax.rsqrt(jnp.mean(x * x, axis=-1, keepdims=True) + EPS) * g


def _twice_sigmoid_of_double(h):
    return jnp.tanh(h) + 1.0


def _silu_of_double(h):
    return h * _twice_sigmoid_of_double(h)


def _dot(a, b):
    return jnp.dot(a, b, preferred_element_type=F32)


def _dot_nt(a, b):
    return lax.dot_general(a, b, (((1,), (1,)), ((), ())), preferred_element_type=F32)


def _mem_kv_kernel(mem_ref, g_ref, w_ref, k_ref, v_ref):
    hb = _rms(mem_ref[...], g_ref[...]).astype(BF16)
    kv = _dot(hb, w_ref[...])
    k_ref[...] = kv[:, :X_WIDTH].astype(BF16)
    v_ref[...] = kv[:, X_WIDTH:].astype(BF16)


def _mem_kv(mem, g, w):
    b, m, d = mem.shape
    return pl.pallas_call(
        _mem_kv_kernel,
        grid=(b,),
        in_specs=[pl.BlockSpec((None, m, d), lambda i: (i, 0, 0)),
                  pl.BlockSpec((1, d), lambda i: (0, 0)),
                  pl.BlockSpec((d, 2 * X_WIDTH), lambda i: (0, 0))],
        out_specs=[pl.BlockSpec((None, m, X_WIDTH), lambda i: (i, 0, 0)),
                   pl.BlockSpec((None, m, X_WIDTH), lambda i: (i, 0, 0))],
        out_shape=[jax.ShapeDtypeStruct((b, m, X_WIDTH), BF16)] * 2,
        compiler_params=pltpu.CompilerParams(dimension_semantics=("arbitrary",)),
        name="mem_kv",
    )(mem, g, w)


def _qkv_kernel(x_ref, pos_ref, ng_ref, wa_ref, qg_ref, wuq_ref, kvg_ref, wuk_ref, wuv_ref,
                invf_ref, qt_ref, k_ref, vt_ref, hb_ref):
    hb = _rms(x_ref[...], ng_ref[...]).astype(BF16)
    hb_ref[...] = hb
    za = _dot(hb, wa_ref[...])
    cq = _rms(za[:, :Q_LORA], qg_ref[...]).astype(BF16)
    ckv = _rms(za[:, Q_LORA:Q_LORA + KV_LORA], kvg_ref[...]).astype(BF16)
    kr = za[:, Q_LORA + KV_LORA:]

    ang = invf_ref[...] * pos_ref[...].astype(F32)
    cos, sin = jnp.cos(ang), jnp.sin(ang)

    def rope_t(t):
        x1 = t[ROPE_LO:ROPE_LO + ROPE_HALF]
        x2 = t[ROPE_LO + ROPE_HALF:ROPE_HI]
        return jnp.concatenate(
            [t[:ROPE_LO], x1 * cos - x2 * sin, x1 * sin + x2 * cos, t[ROPE_HI:]], axis=0)

    kr_rot = rope_t(kr.T).T
    q = _dot(cq, wuq_ref[...])
    kn = _dot(ckv, wuk_ref[...])
    for h in range(MLA_HEADS):
        sl = slice(HEAD_PAD * h, HEAD_PAD * (h + 1))
        qt_ref[h] = (rope_t(q[:, sl].T) * (MLA_SCALE * LOG2E)).astype(BF16)
        k_ref[:, sl] = (kn[:, sl] + kr_rot).astype(BF16)
    vt = _dot(ckv, wuv_ref[...]).T.astype(BF16)
    ones_row = (lax.broadcasted_iota(jnp.int32, (V_AUG - V_HEAD, vt.shape[1]), 0) == 0).astype(BF16)
    for h in range(MLA_HEADS):
        vt_ref[h, 0:V_HEAD, :] = vt[V_HEAD * h:V_HEAD * (h + 1), :]
        vt_ref[h, V_HEAD:V_AUG, :] = ones_row


def _qkv(x, pos3, ng, wa, qg, wuq, kvg, wuk, wuv, invf):
    b, s, d = x.shape
    ts = TS_QKV
    const = lambda shape: pl.BlockSpec(shape, lambda i, j: (0,) * len(shape))
    hp = MLA_HEADS * HEAD_PAD
    return pl.pallas_call(
        _qkv_kernel,
        grid=(b, s // ts),
        in_specs=[pl.BlockSpec((None, ts, d), lambda i, j: (i, j, 0)),
                  pl.BlockSpec((None, 1, ts), lambda i, j: (i, 0, j)),
                  const((1, d)), const(wa.shape), const((1, Q_LORA)), const(wuq.shape),
                  const((1, KV_LORA)), const(wuk.shape), const(wuv.shape), const((ROPE_HALF, 1))],
        out_specs=[pl.BlockSpec((None, MLA_HEADS, HEAD_PAD, ts), lambda i, j: (i, 0, 0, j)),
                   pl.BlockSpec((None, ts, hp), lambda i, j: (i, j, 0)),
                   pl.BlockSpec((None, MLA_HEADS, V_AUG, ts), lambda i, j: (i, 0, 0, j)),
                   pl.BlockSpec((None, ts, d), lambda i, j: (i, j, 0))],
        out_shape=[jax.ShapeDtypeStruct((b, MLA_HEADS, HEAD_PAD, s), BF16),
                   jax.ShapeDtypeStruct((b, s, hp), BF16),
                   jax.ShapeDtypeStruct((b, MLA_HEADS, V_AUG, s), BF16),
                   jax.ShapeDtypeStruct((b, s, d), BF16)],
        compiler_params=pltpu.CompilerParams(dimension_semantics=("arbitrary", "arbitrary"),
                                             vmem_limit_bytes=VMEM_LIMIT),
        name="qkv",
    )(x, pos3, ng, wa, qg, wuq, kvg, wuk, wuv, invf)


ATTN_SLOTS = 4
ATTN_UNROLL = 4
ATTN_STRIP = 256
MAX_CHAINS = 4


def _attn_schedule(n_tiles):
    below = [(qi, j) for qi in range(n_tiles) for j in range(qi)]
    diag = [(qi, qi) for qi in range(n_tiles)]
    assert len(below) % ATTN_UNROLL == 0 and len(diag) % ATTN_UNROLL == 0
    table = np.array(below + diag + [(0, 0)], np.int32).T
    return table, len(below), len(diag)


def _attn_kernel(n_below, n_diag, tbl_ref, qt_ref, k_ref, vt_ref, o_ref,
                 s_scr, mblk_scr, m_scr, acc_scr):
    heads = (0, 1)
    units = [(hh, c) for hh in heads for c in range(TQ // ATTN_STRIP)]

    def key_rows(c, diag):
        return min(TK, (c + 1) * ATTN_STRIP) if diag else TK

    def scores(f, slot, hh, c, diag):
        q0 = pl.multiple_of(tbl_ref[0, f] * TQ + c * ATTN_STRIP, ATTN_STRIP)
        k0 = pl.multiple_of(tbl_ref[1, f] * TK, TK)
        nk = key_rows(c, diag)
        k = k_ref[pl.ds(k0, nk), HEAD_PAD * hh:HEAD_PAD * (hh + 1)]
        s = _dot(k, qt_ref[hh, :, pl.ds(q0, ATTN_STRIP)])
        if diag:
            key = lax.broadcasted_iota(jnp.int32, (nk, ATTN_STRIP), 0)
            qry = lax.broadcasted_iota(jnp.int32, (nk, ATTN_STRIP), 1) + c * ATTN_STRIP
            s = jnp.where(key <= qry, s, NEG)
        s_scr[slot, hh, c, 0:nk, :] = s
        tiles = s.reshape(MAX_CHAINS, nk // (MAX_CHAINS * SUBLANES), SUBLANES, ATTN_STRIP)
        mblk_scr[slot, hh, c] = jnp.max(jnp.max(jnp.max(tiles, axis=1), axis=0),
                                        axis=0, keepdims=True)

    def softmax(f, slot, hh, c, diag):
        qi = tbl_ref[0, f]
        nk = key_rows(c, diag)
        m_old = m_scr[qi, hh, c]
        m_new = jnp.maximum(m_old, mblk_scr[slot, hh, c])
        m_scr[qi, hh, c] = m_new
        alpha = jnp.exp2(m_old - m_new)
        p = jnp.exp2((s_scr[slot, hh, c, 0:nk, :] - m_new).astype(BF16))
        return alpha, p

    def values(f, hh, c, diag, alpha, p):
        qi = tbl_ref[0, f]
        k0 = pl.multiple_of(tbl_ref[1, f] * TK, TK)
        vt = vt_ref[hh, :, pl.ds(k0, key_rows(c, diag))]
        acc_scr[qi, hh, c] = alpha * acc_scr[qi, hh, c] + _dot(vt, p)

    def trip(f, first, unroll, diag, last_next_diag):
        for u in range(unroll):
            slot, nxt = (first + u) % ATTN_SLOTS, (first + u + 1) % ATTN_SLOTS
            next_diag = last_next_diag if u == unroll - 1 else diag
            for hh, c in units:
                alpha, p = softmax(f + u, slot, hh, c, diag)
                scores(f + u + 1, nxt, hh, c, next_diag)
                values(f + u, hh, c, diag, alpha, p)

    def trips(first, count, unroll, diag, then_diag):
        assert count % unroll == 0 and (count == unroll or unroll % ATTN_SLOTS == 0)
        n_trips = count // unroll
        if n_trips > 1:
            pl.loop(0, n_trips - 1)(lambda t: trip(first + t * unroll, first, unroll, diag, diag))
        if n_trips > 0:
            trip(first + (n_trips - 1) * unroll, first, unroll, diag, then_diag)

    m_scr[...] = jnp.full(m_scr.shape, NEG, F32)
    acc_scr[...] = jnp.zeros(acc_scr.shape, F32)
    for hh, c in units:
        scores(0, 0, hh, c, n_below == 0)
    trips(0, n_below, ATTN_UNROLL, diag=False, then_diag=True)
    trips(n_below, n_diag, ATTN_UNROLL, diag=True, then_diag=True)

    @pl.loop(0, n_diag)
    def _(qi):
        ot = jnp.concatenate(
            [jnp.concatenate([acc_scr[qi, hh, c, 0:V_HEAD] / acc_scr[qi, hh, c, V_HEAD:V_HEAD + 1]
                              for c in range(TQ // ATTN_STRIP)], axis=1) for hh in heads],
            axis=0)
        o_ref[pl.ds(pl.multiple_of(qi * TQ, TQ), TQ), :] = ot.T.astype(BF16)


def _mla_attn(qt, kp, vt):
    b, s, _ = kp.shape
    pairs = MLA_HEADS // 2
    n_tiles = s // TQ
    n_strips = TQ // ATTN_STRIP
    table, n_below, n_diag = _attn_schedule(n_tiles)
    grid_spec = pltpu.PrefetchScalarGridSpec(
        num_scalar_prefetch=1,
        grid=(b, pairs),
        in_specs=[pl.BlockSpec((None, 2, HEAD_PAD, s), lambda i, p, tbl: (i, p, 0, 0)),
                  pl.BlockSpec((None, s, 2 * HEAD_PAD), lambda i, p, tbl: (i, 0, p)),
                  pl.BlockSpec((None, 2, V_AUG, s), lambda i, p, tbl: (i, p, 0, 0))],
        out_specs=pl.BlockSpec((None, s, LANES), lambda i, p, tbl: (i, 0, p)),
        scratch_shapes=[pltpu.VMEM((ATTN_SLOTS, 2, n_strips, TK, ATTN_STRIP), F32),
                        pltpu.VMEM((ATTN_SLOTS, 2, n_strips, 1, ATTN_STRIP), F32),
                        pltpu.VMEM((n_tiles, 2, n_strips, 1, ATTN_STRIP), F32),
                        pltpu.VMEM((n_tiles, 2, n_strips, V_AUG, ATTN_STRIP), F32)])
    return pl.pallas_call(
        functools.partial(_attn_kernel, n_below, n_diag),
        grid_spec=grid_spec,
        out_shape=jax.ShapeDtypeStruct((b, s, MLA_WIDTH), BF16),
        compiler_params=pltpu.CompilerParams(dimension_semantics=("arbitrary", "arbitrary"),
                                             vmem_limit_bytes=VMEM_LIMIT),
        name="mla_attn",
    )(jnp.asarray(table), qt, kp, vt)


_C_VAL, _C_GLU, _C_GATE, _M_GATE, _X_Q, _X_GATE, _G0 = (
    0, CONV_WIDTH, 2 * CONV_WIDTH, 3 * CONV_WIDTH, 3 * CONV_WIDTH + MLA_WIDTH,
    3 * CONV_WIDTH + MLA_WIDTH + X_WIDTH, 3 * CONV_WIDTH + MLA_WIDTH + 2 * X_WIDTH)
W_B_COLS = _G0 + 3 * D_MODEL


def _block_kernel(x_ref, hb_ref, o_mla_ref, kx_ref, vx_ref, wc_ref, wr_ref, bg_ref, cw_ref, cb_ref,
                  lng_ref, lnb_ref, wco_ref, wmo_ref, wxo_ref, wout_ref, fg_ref,
                  out_ref, ubuf, ush, conv_scr, part_scr, cgate_scr, g0_scr):
    ts = TS_BLK

    def proj(lo, width):
        if lo < _M_GATE:
            return _dot(hb_ref[...], wc_ref[:, lo:lo + width])
        return _dot(hb_ref[...], wr_ref[:, lo - _M_GATE:lo - _M_GATE + width])

    def gate(i, lo, width):
        c0 = i * D_MODEL + lo
        return _twice_sigmoid_of_double(proj(_G0 + c0, width) + bg_ref[:, c0:c0 + width])

    @pl.when(pl.program_id(1) == 0)
    def _():
        ubuf[0:HALO, :] = jnp.zeros((HALO, CONV_WIDTH), F32)

    ubuf[HALO:HALO + ts, :] = (proj(_C_VAL, CONV_WIDTH)
                               * _twice_sigmoid_of_double(proj(_C_GLU, CONV_WIDTH)))
    groups = SHIFT_ROWS // SUBLANES
    this_tile = ubuf[0:SHIFT_ROWS, :].reshape(groups, SUBLANES, CONV_WIDTH)
    next_tile = ubuf[SUBLANES:SUBLANES + SHIFT_ROWS, :].reshape(groups, SUBLANES, CONV_WIDTH)
    sub = lax.broadcasted_iota(jnp.int32, (1, SUBLANES, CONV_WIDTH), 1)
    for r in range(1, SUBLANES):
        picked = jnp.where(sub >= r, this_tile, next_tile)
        ush[r - 1] = pltpu.roll(picked, SUBLANES - r, 1).reshape(SHIFT_ROWS, CONV_WIDTH)
    first = HALO - (CONV_KERNEL - 1)

    def conv_chunk(i):
        r0 = i * CONV_ROWS
        groups = CONV_ROWS // SUBLANES
        acc = jnp.zeros((groups, SUBLANES, CONV_WIDTH), F32) + cb_ref[...]
        for kk in range(CONV_KERNEL):
            r = (first + kk) % SUBLANES
            row = r0 + first + kk - r
            tap = ubuf[row:row + CONV_ROWS, :] if r == 0 else ush[r - 1, row:row + CONV_ROWS, :]
            w = cw_ref[kk * SUBLANES:(kk + 1) * SUBLANES, :]
            acc = acc + tap.reshape(groups, SUBLANES, CONV_WIDTH) * w
        conv_scr[r0:r0 + CONV_ROWS, :] = acc.reshape(CONV_ROWS, CONV_WIDTH)

    st = {}
    half = D_MODEL // 2

    def mla_gate():
        gate_m = _silu_of_double(proj(_M_GATE, MLA_WIDTH))
        st["a_mla"] = (o_mla_ref[...].astype(F32) * gate_m).astype(BF16)

    def mla_out():
        st["y_mla"] = _dot(st["a_mla"], wmo_ref[...])

    def mla_merge(lo):
        part_scr[:, lo:lo + half] = gate(1, lo, half) * st["y_mla"][:, lo:lo + half]

    def x_query():
        st["xq"] = proj(_X_Q, X_WIDTH).astype(BF16)
        st["ox"] = []

    def x_heads(h0):
        for h in (h0, h0 + 1):
            hs = slice(X_HEAD_DIM * h, X_HEAD_DIM * (h + 1))
            s = _dot_nt(st["xq"][:, hs], kx_ref[:, hs]) * X_SCALE
            p = jnp.exp(s - jnp.max(s, axis=-1, keepdims=True))
            p = p / jnp.sum(p, axis=-1, keepdims=True)
            st["ox"].append(_dot(p.astype(BF16), vx_ref[:, hs]))

    def x_gate():
        st["xg"] = _silu_of_double(proj(_X_GATE, X_WIDTH))

    def x_out():
        a_x = jnp.concatenate(st["ox"], axis=-1) * st["xg"]
        st["y_x"] = _dot(a_x.astype(BF16), wxo_ref[...])

    def x_merge(lo):
        part_scr[:, lo:lo + half] += gate(2, lo, half) * st["y_x"][:, lo:lo + half]

    def conv_gate():
        cgate_scr[...] = _silu_of_double(proj(_C_GATE, CONV_WIDTH))

    def conv_merge_gate(lo):
        g0_scr[:, lo:lo + half] = gate(0, lo, half)

    side = [mla_gate, mla_out, functools.partial(mla_merge, 0), functools.partial(mla_merge, half),
            x_query, functools.partial(x_heads, 0), functools.partial(x_heads, 2), x_gate, x_out,
            functools.partial(x_merge, 0), functools.partial(x_merge, half), conv_gate,
            functools.partial(conv_merge_gate, 0), functools.partial(conv_merge_gate, half)]
    n_chunks = ts // CONV_ROWS
    for i in range(n_chunks):
        conv_chunk(i)
        for j, piece in enumerate(side):
            if j * n_chunks // len(side) == i:
                piece()
    ubuf[0:HALO, :] = ubuf[ts:ts + HALO, :]

    conv = conv_scr[...]
    mu = jnp.mean(conv, axis=-1, keepdims=True)
    cen = conv - mu
    var = jnp.mean(cen * cen, axis=-1, keepdims=True)
    half_ln = cen * lax.rsqrt(var + EPS) * lng_ref[...] + lnb_ref[...]
    a_conv = _silu_of_double(half_ln) * cgate_scr[...]
    merged = g0_scr[...] * _dot(a_conv.astype(BF16), wco_ref[...]) + part_scr[...]
    y = x_ref[...] + _dot(merged.astype(BF16), wout_ref[...])
    out_ref[...] = _rms(y, fg_ref[...])


def _block(x, hb, o_mla, kx, vx, wc, wr, bg, cw, cb, lng, lnb, wco, wmo, wxo, wout, fg):
    b, s, d = x.shape
    ts = TS_BLK
    const = lambda shape: pl.BlockSpec(shape, lambda i, j: (0,) * len(shape),
                                       pipeline_mode=pl.Buffered(1))
    return pl.pallas_call(
        _block_kernel,
        grid=(b, s // ts),
        in_specs=[pl.BlockSpec((None, ts, d), lambda i, j: (i, j, 0)),
                  pl.BlockSpec((None, ts, d), lambda i, j: (i, j, 0)),
                  pl.BlockSpec((None, ts, MLA_WIDTH), lambda i, j: (i, j, 0)),
                  pl.BlockSpec((None, N_MEM, X_WIDTH), lambda i, j: (i, 0, 0)),
                  pl.BlockSpec((None, N_MEM, X_WIDTH), lambda i, j: (i, 0, 0)),
                  const(wc.shape), const(wr.shape), const(bg.shape),
                  const(cw.shape),
                  const((1, CONV_WIDTH)), const((1, CONV_WIDTH)), const((1, CONV_WIDTH)),
                  const(wco.shape), const(wmo.shape), const(wxo.shape), const(wout.shape),
                  const((1, d))],
        out_specs=pl.BlockSpec((None, ts, d), lambda i, j: (i, j, 0)),
        out_shape=jax.ShapeDtypeStruct((b, s, d), F32),
        scratch_shapes=[pltpu.VMEM((HALO + ts, CONV_WIDTH), F32),
                        pltpu.VMEM((SUBLANES - 1, SHIFT_ROWS, CONV_WIDTH), F32),
                        pltpu.VMEM((ts, CONV_WIDTH), F32),
                        pltpu.VMEM((ts, d), F32),
                        pltpu.VMEM((ts, CONV_WIDTH), F32),
                        pltpu.VMEM((ts, d), F32)],
        compiler_params=pltpu.CompilerParams(dimension_semantics=("arbitrary", "arbitrary"),
                                             vmem_limit_bytes=VMEM_LIMIT),
        name="block",
    )(x, hb, o_mla, kx, vx, wc, wr, bg, cw, cb, lng, lnb, wco, wmo, wxo, wout, fg)


def _inv_freq_column():
    inv = ROPE_THETA ** (-jnp.arange(0, QK_ROPE, 2, dtype=F32) / QK_ROPE)
    return inv.reshape(ROPE_HALF, 1)


def kernel(x, mem, positions, norm_g, w_in, b_gate, conv_w, conv_b, conv_ln_g, conv_ln_b,
           w_conv_o, q_norm_g, w_uq, kv_norm_g, w_ukv, w_mla_o, mem_norm_g, w_mem_kv, w_x_o,
           w_out, final_norm_g):
    assert norm_g.shape[0] == 1, "single-layer trunk"
    b, s, d = x.shape
    row = lambda v: v.reshape(1, -1)
    edges = np.cumsum((0,) + IN_SPLITS)
    w_in_bf = w_in[0].astype(BF16)
    w_qd, w_kvd, w_kr = (w_in_bf[:, edges[i]:edges[i + 1]] for i in (3, 4, 5))

    w_kr_pad = jnp.pad(w_kr, ((0, 0), (ROPE_LO, LANES - ROPE_HI)))
    wa = jnp.concatenate([w_qd, w_kvd, w_kr_pad], axis=1)
    assert edges[3] == _M_GATE and edges[-1] - edges[6] == W_B_COLS - _M_GATE
    halved = np.full((1, edges[-1] - edges[6]), 0.5, np.float32)
    halved[0, edges[7] - edges[6]:edges[8] - edges[6]] = 1.0
    wc = w_in_bf[:, :edges[3]] * jnp.asarray(0.5, BF16)
    wr = w_in_bf[:, edges[6]:] * jnp.asarray(halved, BF16)
    wuq = jnp.pad(w_uq[0].reshape(Q_LORA, MLA_HEADS, QK_NOPE + QK_ROPE),
                  ((0, 0), (0, 0), (0, HEAD_PAD - QK_NOPE - QK_ROPE)))
    wuq = wuq.reshape(Q_LORA, MLA_HEADS * HEAD_PAD).astype(BF16)
    wukv = w_ukv[0].reshape(KV_LORA, MLA_HEADS, QK_NOPE + V_HEAD)
    wuk = jnp.pad(wukv[:, :, :QK_NOPE], ((0, 0), (0, 0), (0, HEAD_PAD - QK_NOPE)))
    wuk = wuk.reshape(KV_LORA, MLA_HEADS * HEAD_PAD).astype(BF16)
    wuv = wukv[:, :, QK_NOPE:].reshape(KV_LORA, MLA_WIDTH).astype(BF16)

    kx, vx = _mem_kv(mem, row(mem_norm_g[0]), w_mem_kv[0].astype(BF16))
    qt, kp, vt, hb = _qkv(x, positions.reshape(b, 1, s), row(norm_g[0]), wa, row(q_norm_g[0]),
                          wuq, row(kv_norm_g[0]), wuk, wuv, _inv_freq_column())
    o_mla = _mla_attn(qt, kp, vt)
    half_bf = lambda w: (0.5 * w).astype(BF16)
    return _block(x, hb, o_mla, kx, vx, wc, wr, row(0.5 * b_gate[0]),
                  jnp.repeat(conv_w[0], SUBLANES, axis=0),
                  row(conv_b[0]), row(0.5 * conv_ln_g[0]), row(0.5 * conv_ln_b[0]),
                  half_bf(w_conv_o[0]), half_bf(w_mla_o[0]), half_bf(w_x_o[0]),
                  w_out[0].astype(BF16), row(final_norm_g))
```

```python
import functools

import jax
import jax.numpy as jnp
import numpy as np
from jax import lax
from jax.experimental import pallas as pl
from jax.experimental.pallas import tpu as pltpu

F32 = jnp.float32
BF16 = jnp.bfloat16

D_MODEL = 1024
N_MEM = 256
CONV_WIDTH = 512
CONV_KERNEL = 31
MLA_HEADS = 8
QK_NOPE = 64
QK_ROPE = 32
V_HEAD = 64
Q_LORA = 384
KV_LORA = 256
MLA_WIDTH = MLA_HEADS * V_HEAD
X_HEADS = 4
X_HEAD_DIM = 128
X_WIDTH = X_HEADS * X_HEAD_DIM
ROPE_THETA = 10000.0
EPS = 1e-6
IN_SPLITS = (CONV_WIDTH, CONV_WIDTH, CONV_WIDTH, Q_LORA, KV_LORA, QK_ROPE, MLA_WIDTH,
             X_WIDTH, X_WIDTH, 3 * D_MODEL)

LANES = 128
HEAD_PAD = LANES
BF16_ROWS = 16
V_AUG = V_HEAD + BF16_ROWS
ROPE_LO = QK_NOPE
ROPE_HALF = QK_ROPE // 2
ROPE_HI = QK_NOPE + QK_ROPE
MLA_SCALE = (QK_NOPE + QK_ROPE) ** -0.5
X_SCALE = X_HEAD_DIM ** -0.5
LOG2E = float(np.log2(np.e))
NEG = float(np.finfo(np.float32).min)

TS_QKV = 512
TQ = 512
TK = 512
TS_BLK = 512
HALO = 32
CONV_ROWS = 32
CONV_SPLIT = 2
OUT_SPLIT = 2
SUBLANES = 8
SHIFT_ROWS = TS_BLK + HALO - SUBLANES
VMEM_LIMIT = 56 * 1024 * 1024


def _rms(x, g):
    return x * lax.rsqrt(jnp.mean(x * x, axis=-1, keepdims=True) + EPS) * g


def _twice_sigmoid_of_double(h):
    return jnp.tanh(h) + 1.0


def _silu_of_double(h):
    return h * _twice_sigmoid_of_double(h)


def _dot(a, b):
    return jnp.dot(a, b, preferred_element_type=F32)


def _dot_nt(a, b):
    return lax.dot_general(a, b, (((1,), (1,)), ((), ())), preferred_element_type=F32)


def _mem_kv_kernel(mem_ref, g_ref, w_ref, k_ref, v_ref):
    hb = _rms(mem_ref[...], g_ref[...]).astype(BF16)
    kv = _dot(hb, w_ref[...])
    k_ref[...] = kv[:, :X_WIDTH].astype(BF16)
    v_ref[...] = kv[:, X_WIDTH:].astype(BF16)


def _mem_kv(mem, g, w):
    b, m, d = mem.shape
    return pl.pallas_call(
        _mem_kv_kernel,
        grid=(b,),
        in_specs=[pl.BlockSpec((None, m, d), lambda i: (i, 0, 0)),
                  pl.BlockSpec((1, d), lambda i: (0, 0)),
                  pl.BlockSpec((d, 2 * X_WIDTH), lambda i: (0, 0))],
        out_specs=[pl.BlockSpec((None, m, X_WIDTH), lambda i: (i, 0, 0)),
                   pl.BlockSpec((None, m, X_WIDTH), lambda i: (i, 0, 0))],
        out_shape=[jax.ShapeDtypeStruct((b, m, X_WIDTH), BF16)] * 2,
        compiler_params=pltpu.CompilerParams(dimension_semantics=("arbitrary",)),
        name="mem_kv",
    )(mem, g, w)


def _qkv_kernel(x_ref, pos_ref, ng_ref, wa_ref, qg_ref, wuq_ref, kvg_ref, wuk_ref, wuv_ref,
                invf_ref, qt_ref, k_ref, vt_ref, hb_ref):
    hb = _rms(x_ref[...], ng_ref[...]).astype(BF16)
    hb_ref[...] = hb
    za = _dot(hb, wa_ref[...])
    cq = _rms(za[:, :Q_LORA], qg_ref[...]).astype(BF16)
    ckv = _rms(za[:, Q_LORA:Q_LORA + KV_LORA], kvg_ref[...]).astype(BF16)
    kr = za[:, Q_LORA + KV_LORA:]

    ang = invf_ref[...] * pos_ref[...].astype(F32)
    cos, sin = jnp.cos(ang), jnp.sin(ang)

    def rope_t(t):
        x1 = t[ROPE_LO:ROPE_LO + ROPE_HALF]
        x2 = t[ROPE_LO + ROPE_HALF:ROPE_HI]
        return jnp.concatenate(
            [t[:ROPE_LO], x1 * cos - x2 * sin, x1 * sin + x2 * cos, t[ROPE_HI:]], axis=0)

    kr_rot = rope_t(kr.T).T
    q = _dot(cq, wuq_ref[...])
    kn = _dot(ckv, wuk_ref[...])
    for h in range(MLA_HEADS):
        sl = slice(HEAD_PAD * h, HEAD_PAD * (h + 1))
        qt_ref[h] = (rope_t(q[:, sl].T) * (MLA_SCALE * LOG2E)).astype(BF16)
        k_ref[:, sl] = (kn[:, sl] + kr_rot).astype(BF16)
    vt = _dot(ckv, wuv_ref[...]).T.astype(BF16)
    ones_row = (lax.broadcasted_iota(jnp.int32, (V_AUG - V_HEAD, vt.shape[1]), 0) == 0).astype(BF16)
    for h in range(MLA_HEADS):
        vt_ref[h, 0:V_HEAD, :] = vt[V_HEAD * h:V_HEAD * (h + 1), :]
        vt_ref[h, V_HEAD:V_AUG, :] = ones_row


def _qkv(x, pos3, ng, wa, qg, wuq, kvg, wuk, wuv, invf):
    b, s, d = x.shape
    ts = TS_QKV
    const = lambda shape: pl.BlockSpec(shape, lambda i, j: (0,) * len(shape))
    hp = MLA_HEADS * HEAD_PAD
    return pl.pallas_call(
        _qkv_kernel,
        grid=(b, s // ts),
        in_specs=[pl.BlockSpec((None, ts, d), lambda i, j: (i, j, 0)),
                  pl.BlockSpec((None, 1, ts), lambda i, j: (i, 0, j)),
                  const((1, d)), const(wa.shape), const((1, Q_LORA)), const(wuq.shape),
                  const((1, KV_LORA)), const(wuk.shape), const(wuv.shape), const((ROPE_HALF, 1))],
        out_specs=[pl.BlockSpec((None, MLA_HEADS, HEAD_PAD, ts), lambda i, j: (i, 0, 0, j)),
                   pl.BlockSpec((None, ts, hp), lambda i, j: (i, j, 0)),
                   pl.BlockSpec((None, MLA_HEADS, V_AUG, ts), lambda i, j: (i, 0, 0, j)),
                   pl.BlockSpec((None, ts, d), lambda i, j: (i, j, 0))],
        out_shape=[jax.ShapeDtypeStruct((b, MLA_HEADS, HEAD_PAD, s), BF16),
                   jax.ShapeDtypeStruct((b, s, hp), BF16),
                   jax.ShapeDtypeStruct((b, MLA_HEADS, V_AUG, s), BF16),
                   jax.ShapeDtypeStruct((b, s, d), BF16)],
        compiler_params=pltpu.CompilerParams(dimension_semantics=("arbitrary", "arbitrary"),
                                             vmem_limit_bytes=VMEM_LIMIT),
        name="qkv",
    )(x, pos3, ng, wa, qg, wuq, kvg, wuk, wuv, invf)


ATTN_SLOTS = 4
ATTN_UNROLL = 4
ATTN_STRIP = 256
MAX_CHAINS = 4


def _attn_schedule(n_tiles):
    below = [(qi, j) for qi in range(n_tiles) for j in range(qi)]
    diag = [(qi, qi) for qi in range(n_tiles)]
    assert len(below) % ATTN_UNROLL == 0 and len(diag) % ATTN_UNROLL == 0
    table = np.array(below + diag + [(0, 0)], np.int32).T
    return table, len(below), len(diag)


def _attn_kernel(n_below, n_diag, tbl_ref, qt_ref, k_ref, vt_ref, o_ref,
                 s_scr, mblk_scr, m_scr, acc_scr):
    heads = (0, 1)
    units = [(hh, c) for hh in heads for c in range(TQ // ATTN_STRIP)]

    def key_rows(c, diag):
        return min(TK, (c + 1) * ATTN_STRIP) if diag else TK

    def scores(f, slot, hh, c, diag):
        q0 = pl.multiple_of(tbl_ref[0, f] * TQ + c * ATTN_STRIP, ATTN_STRIP)
        k0 = pl.multiple_of(tbl_ref[1, f] * TK, TK)
        nk = key_rows(c, diag)
        k = k_ref[pl.ds(k0, nk), HEAD_PAD * hh:HEAD_PAD * (hh + 1)]
        s = _dot(k, qt_ref[hh, :, pl.ds(q0, ATTN_STRIP)])
        if diag:
            key = lax.broadcasted_iota(jnp.int32, (nk, ATTN_STRIP), 0)
            qry = lax.broadcasted_iota(jnp.int32, (nk, ATTN_STRIP), 1) + c * ATTN_STRIP
            s = jnp.where(key <= qry, s, NEG)
        s_scr[slot, hh, c, 0:nk, :] = s
        tiles = s.reshape(MAX_CHAINS, nk // (MAX_CHAINS * SUBLANES), SUBLANES, ATTN_STRIP)
        mblk_scr[slot, hh, c] = jnp.max(jnp.max(jnp.max(tiles, axis=1), axis=0),
                                        axis=0, keepdims=True)

    def softmax(f, slot, hh, c, diag):
        qi = tbl_ref[0, f]
        nk = key_rows(c, diag)
        m_old = m_scr[qi, hh, c]
        m_new = jnp.maximum(m_old, mblk_scr[slot, hh, c])
        m_scr[qi, hh, c] = m_new
        alpha = jnp.exp2(m_old - m_new)
        p = jnp.exp2((s_scr[slot, hh, c, 0:nk, :] - m_new).astype(BF16))
        return alpha, p

    def values(f, hh, c, diag, alpha, p):
        qi = tbl_ref[0, f]
        k0 = pl.multiple_of(tbl_ref[1, f] * TK, TK)
        vt = vt_ref[hh, :, pl.ds(k0, key_rows(c, diag))]
        acc_scr[qi, hh, c] = alpha * acc_scr[qi, hh, c] + _dot(vt, p)

    def trip(f, first, unroll, diag, last_next_diag):
        for u in range(unroll):
            slot, nxt = (first + u) % ATTN_SLOTS, (first + u + 1) % ATTN_SLOTS
            next_diag = last_next_diag if u == unroll - 1 else diag
            for hh, c in units:
                alpha, p = softmax(f + u, slot, hh, c, diag)
                scores(f + u + 1, nxt, hh, c, next_diag)
                values(f + u, hh, c, diag, alpha, p)

    def trips(first, count, unroll, diag, then_diag):
        assert count % unroll == 0 and (count == unroll or unroll % ATTN_SLOTS == 0)
        n_trips = count // unroll
        if n_trips > 1:
            pl.loop(0, n_trips - 1)(lambda t: trip(first + t * unroll, first, unroll, diag, diag))
        if n_trips > 0:
            trip(first + (n_trips - 1) * unroll, first, unroll, diag, then_diag)

    m_scr[...] = jnp.full(m_scr.shape, NEG, F32)
    acc_scr[...] = jnp.zeros(acc_scr.shape, F32)
    for hh, c in units:
        scores(0, 0, hh, c, n_below == 0)
    trips(0, n_below, ATTN_UNROLL, diag=False, then_diag=True)
    trips(n_below, n_diag, ATTN_UNROLL, diag=True, then_diag=True)

    @pl.loop(0, n_diag)
    def _(qi):
        ot = jnp.concatenate(
            [jnp.concatenate([acc_scr[qi, hh, c, 0:V_HEAD] / acc_scr[qi, hh, c, V_HEAD:V_HEAD + 1]
                              for c in range(TQ // ATTN_STRIP)], axis=1) for hh in heads],
            axis=0)
        o_ref[pl.ds(pl.multiple_of(qi * TQ, TQ), TQ), :] = ot.T.astype(BF16)


def _mla_attn(qt, kp, vt):
    b, s, _ = kp.shape
    pairs = MLA_HEADS // 2
    n_tiles = s // TQ
    n_strips = TQ // ATTN_STRIP
    table, n_below, n_diag = _attn_schedule(n_tiles)
    grid_spec = pltpu.PrefetchScalarGridSpec(
        num_scalar_prefetch=1,
        grid=(b, pairs),
        in_specs=[pl.BlockSpec((None, 2, HEAD_PAD, s), lambda i, p, tbl: (i, p, 0, 0)),
                  pl.BlockSpec((None, s, 2 * HEAD_PAD), lambda i, p, tbl: (i, 0, p)),
                  pl.BlockSpec((None, 2, V_AUG, s), lambda i, p, tbl: (i, p, 0, 0))],
        out_specs=pl.BlockSpec((None, s, LANES), lambda i, p, tbl: (i, 0, p)),
        scratch_shapes=[pltpu.VMEM((ATTN_SLOTS, 2, n_strips, TK, ATTN_STRIP), F32),
                        pltpu.VMEM((ATTN_SLOTS, 2, n_strips, 1, ATTN_STRIP), F32),
                        pltpu.VMEM((n_tiles, 2, n_strips, 1, ATTN_STRIP), F32),
                        pltpu.VMEM((n_tiles, 2, n_strips, V_AUG, ATTN_STRIP), F32)])
    return pl.pallas_call(
        functools.partial(_attn_kernel, n_below, n_diag),
        grid_spec=grid_spec,
        out_shape=jax.ShapeDtypeStruct((b, s, MLA_WIDTH), BF16),
        compiler_params=pltpu.CompilerParams(dimension_semantics=("arbitrary", "arbitrary"),
                                             vmem_limit_bytes=VMEM_LIMIT),
        name="mla_attn",
    )(jnp.asarray(table), qt, kp, vt)


_C_VAL, _C_GLU, _C_GATE, _M_GATE, _X_Q, _X_GATE, _G0 = (
    0, CONV_WIDTH, 2 * CONV_WIDTH, 3 * CONV_WIDTH, 3 * CONV_WIDTH + MLA_WIDTH,
    3 * CONV_WIDTH + MLA_WIDTH + X_WIDTH, 3 * CONV_WIDTH + MLA_WIDTH + 2 * X_WIDTH)
W_B_COLS = _G0 + 3 * D_MODEL


def _block_kernel(x_ref, hb_ref, o_mla_ref, kx_ref, vx_ref, wc_ref, wr_ref, bg_ref, cw_ref, cb_ref,
                  lng_ref, lnb_ref, wco_ref, wmo_ref, wxo_ref, wout_ref, fg_ref,
                  out_ref, ubuf, ush, conv_scr, part_scr, cgate_scr, g0_scr):
    ts = TS_BLK

    def proj(lo, width):
        if lo < _M_GATE:
            return _dot(hb_ref[...], wc_ref[:, lo:lo + width])
        return _dot(hb_ref[...], wr_ref[:, lo - _M_GATE:lo - _M_GATE + width])

    def gate(i, lo, width):
        c0 = i * D_MODEL + lo
        return _twice_sigmoid_of_double(proj(_G0 + c0, width) + bg_ref[:, c0:c0 + width])

    @pl.when(pl.program_id(1) == 0)
    def _():
        ubuf[0:HALO, :] = jnp.zeros((HALO, CONV_WIDTH), F32)

    cw_ = CONV_WIDTH // CONV_SPLIT
    first = HALO - (CONV_KERNEL - 1)

    def glu(g):
        cs = slice(g * cw_, (g + 1) * cw_)
        ubuf[HALO:HALO + ts, cs] = (proj(_C_VAL + g * cw_, cw_)
                                    * _twice_sigmoid_of_double(proj(_C_GLU + g * cw_, cw_)))
        groups = SHIFT_ROWS // SUBLANES
        this_tile = ubuf[0:SHIFT_ROWS, cs].reshape(groups, SUBLANES, cw_)
        next_tile = ubuf[SUBLANES:SUBLANES + SHIFT_ROWS, cs].reshape(groups, SUBLANES, cw_)
        sub = lax.broadcasted_iota(jnp.int32, (1, SUBLANES, cw_), 1)
        for r in range(1, SUBLANES):
            picked = jnp.where(sub >= r, this_tile, next_tile)
            ush[r - 1, :, cs] = pltpu.roll(picked, SUBLANES - r, 1).reshape(SHIFT_ROWS, cw_)

    def conv_chunk(g, i):
        cs = slice(g * cw_, (g + 1) * cw_)
        r0 = i * CONV_ROWS
        groups = CONV_ROWS // SUBLANES
        acc = jnp.zeros((groups, SUBLANES, cw_), F32) + cb_ref[:, cs]
        for kk in range(CONV_KERNEL):
            r = (first + kk) % SUBLANES
            row = r0 + first + kk - r
            tap = ubuf[row:row + CONV_ROWS, cs] if r == 0 else ush[r - 1, row:row + CONV_ROWS, cs]
            w = cw_ref[kk * SUBLANES:(kk + 1) * SUBLANES, cs]
            acc = acc + tap.reshape(groups, SUBLANES, cw_) * w
        conv_scr[r0:r0 + CONV_ROWS, cs] = acc.reshape(CONV_ROWS, cw_)

    st = {}
    half = D_MODEL // 2

    def mla_gate():
        gate_m = _silu_of_double(proj(_M_GATE, MLA_WIDTH))
        st["a_mla"] = (o_mla_ref[...].astype(F32) * gate_m).astype(BF16)

    def mla_out():
        st["y_mla"] = _dot(st["a_mla"], wmo_ref[...])

    def mla_merge(lo):
        part_scr[:, lo:lo + half] = gate(1, lo, half) * st["y_mla"][:, lo:lo + half]

    def x_query():
        st["xq"] = proj(_X_Q, X_WIDTH).astype(BF16)
        st["ox"] = []

    def x_heads(h0):
        for h in (h0, h0 + 1):
            hs = slice(X_HEAD_DIM * h, X_HEAD_DIM * (h + 1))
            s = _dot_nt(st["xq"][:, hs], kx_ref[:, hs]) * X_SCALE
            p = jnp.exp(s - jnp.max(s, axis=-1, keepdims=True))
            p = p / jnp.sum(p, axis=-1, keepdims=True)
            st["ox"].append(_dot(p.astype(BF16), vx_ref[:, hs]))

    def x_gate():
        st["xg"] = _silu_of_double(proj(_X_GATE, X_WIDTH))

    def x_out():
        a_x = jnp.concatenate(st["ox"], axis=-1) * st["xg"]
        st["y_x"] = _dot(a_x.astype(BF16), wxo_ref[...])

    def x_merge(lo):
        part_scr[:, lo:lo + half] += gate(2, lo, half) * st["y_x"][:, lo:lo + half]

    def conv_gate():
        cgate_scr[...] = _silu_of_double(proj(_C_GATE, CONV_WIDTH))

    def conv_merge_gate(lo):
        g0_scr[:, lo:lo + half] = gate(0, lo, half)

    side = [mla_gate, mla_out, functools.partial(mla_merge, 0), functools.partial(mla_merge, half),
            x_query, functools.partial(x_heads, 0), functools.partial(x_heads, 2), x_gate, x_out,
            functools.partial(x_merge, 0), functools.partial(x_merge, half), conv_gate,
            functools.partial(conv_merge_gate, 0), functools.partial(conv_merge_gate, half)]
    side = [functools.partial(glu, g) for g in range(1, CONV_SPLIT)] + side
    chunks = [(g, i) for g in range(CONV_SPLIT) for i in range(ts // CONV_ROWS)]
    glu(0)
    for n, (g, i) in enumerate(chunks):
        conv_chunk(g, i)
        for j, piece in enumerate(side):
            if j * len(chunks) // len(side) == n:
                piece()
    ubuf[0:HALO, :] = ubuf[ts:ts + HALO, :]

    conv = conv_scr[...]
    mu = jnp.mean(conv, axis=-1, keepdims=True)
    cen = conv - mu
    var = jnp.mean(cen * cen, axis=-1, keepdims=True)
    half_ln = cen * lax.rsqrt(var + EPS) * lng_ref[...] + lnb_ref[...]
    a_conv = _silu_of_double(half_ln) * cgate_scr[...]
    y_conv = _dot(a_conv.astype(BF16), wco_ref[...])
    for r0 in range(0, ts, ts // OUT_SPLIT):
        rows = slice(r0, r0 + ts // OUT_SPLIT)
        merged = g0_scr[rows, :] * y_conv[rows, :] + part_scr[rows, :]
        y = x_ref[rows, :] + _dot(merged.astype(BF16), wout_ref[...])
        out_ref[rows, :] = _rms(y, fg_ref[...])


def _block(x, hb, o_mla, kx, vx, wc, wr, bg, cw, cb, lng, lnb, wco, wmo, wxo, wout, fg):
    b, s, d = x.shape
    ts = TS_BLK
    const = lambda shape: pl.BlockSpec(shape, lambda i, j: (0,) * len(shape),
                                       pipeline_mode=pl.Buffered(1))
    return pl.pallas_call(
        _block_kernel,
        grid=(b, s // ts),
        in_specs=[pl.BlockSpec((None, ts, d), lambda i, j: (i, j, 0)),
                  pl.BlockSpec((None, ts, d), lambda i, j: (i, j, 0)),
                  pl.BlockSpec((None, ts, MLA_WIDTH), lambda i, j: (i, j, 0)),
                  pl.BlockSpec((None, N_MEM, X_WIDTH), lambda i, j: (i, 0, 0)),
                  pl.BlockSpec((None, N_MEM, X_WIDTH), lambda i, j: (i, 0, 0)),
                  const(wc.shape), const(wr.shape), const(bg.shape),
                  const(cw.shape),
                  const((1, CONV_WIDTH)), const((1, CONV_WIDTH)), const((1, CONV_WIDTH)),
                  const(wco.shape), const(wmo.shape), const(wxo.shape), const(wout.shape),
                  const((1, d))],
        out_specs=pl.BlockSpec((None, ts, d), lambda i, j: (i, j, 0)),
        out_shape=jax.ShapeDtypeStruct((b, s, d), F32),
        scratch_shapes=[pltpu.VMEM((HALO + ts, CONV_WIDTH), F32),
                        pltpu.VMEM((SUBLANES - 1, SHIFT_ROWS, CONV_WIDTH), F32),
                        pltpu.VMEM((ts, CONV_WIDTH), F32),
                        pltpu.VMEM((ts, d), F32),
                        pltpu.VMEM((ts, CONV_WIDTH), F32),
                        pltpu.VMEM((ts, d), F32)],
        compiler_params=pltpu.CompilerParams(dimension_semantics=("arbitrary", "arbitrary"),
                                             vmem_limit_bytes=VMEM_LIMIT),
        name="block",
    )(x, hb, o_mla, kx, vx, wc, wr, bg, cw, cb, lng, lnb, wco, wmo, wxo, wout, fg)


def _inv_freq_column():
    inv = ROPE_THETA ** (-jnp.arange(0, QK_ROPE, 2, dtype=F32) / QK_ROPE)
    return inv.reshape(ROPE_HALF, 1)


def kernel(x, mem, positions, norm_g, w_in, b_gate, conv_w, conv_b, conv_ln_g, conv_ln_b,
           w_conv_o, q_norm_g, w_uq, kv_norm_g, w_ukv, w_mla_o, mem_norm_g, w_mem_kv, w_x_o,
           w_out, final_norm_g):
    assert norm_g.shape[0] == 1, "single-layer trunk"
    b, s, d = x.shape
    row = lambda v: v.reshape(1, -1)
    edges = np.cumsum((0,) + IN_SPLITS)
    w_in_bf = w_in[0].astype(BF16)
    w_qd, w_kvd, w_kr = (w_in_bf[:, edges[i]:edges[i + 1]] for i in (3, 4, 5))

    w_kr_pad = jnp.pad(w_kr, ((0, 0), (ROPE_LO, LANES - ROPE_HI)))
    wa = jnp.concatenate([w_qd, w_kvd, w_kr_pad], axis=1)
    assert edges[3] == _M_GATE and edges[-1] - edges[6] == W_B_COLS - _M_GATE
    halved = np.full((1, edges[-1] - edges[6]), 0.5, np.float32)
    halved[0, edges[7] - edges[6]:edges[8] - edges[6]] = 1.0
    wc = w_in_bf[:, :edges[3]] * jnp.asarray(0.5, BF16)
    wr = w_in_bf[:, edges[6]:] * jnp.asarray(halved, BF16)
    wuq = jnp.pad(w_uq[0].reshape(Q_LORA, MLA_HEADS, QK_NOPE + QK_ROPE),
                  ((0, 0), (0, 0), (0, HEAD_PAD - QK_NOPE - QK_ROPE)))
    wuq = wuq.reshape(Q_LORA, MLA_HEADS * HEAD_PAD).astype(BF16)
    wukv = w_ukv[0].reshape(KV_LORA, MLA_HEADS, QK_NOPE + V_HEAD)
    wuk = jnp.pad(wukv[:, :, :QK_NOPE], ((0, 0), (0, 0), (0, HEAD_PAD - QK_NOPE)))
    wuk = wuk.reshape(KV_LORA, MLA_HEADS * HEAD_PAD).astype(BF16)
    wuv = wukv[:, :, QK_NOPE:].reshape(KV_LORA, MLA_WIDTH).astype(BF16)

    kx, vx = _mem_kv(mem, row(mem_norm_g[0]), w_mem_kv[0].astype(BF16))
    qt, kp, vt, hb = _qkv(x, positions.reshape(b, 1, s), row(norm_g[0]), wa, row(q_norm_g[0]),
                          wuq, row(kv_norm_g[0]), wuk, wuv, _inv_freq_column())
    o_mla = _mla_attn(qt, kp, vt)
    half_bf = lambda w: (0.5 * w).astype(BF16)
    return _block(x, hb, o_mla, kx, vx, wc, wr, row(0.5 * b_gate[0]),
                  jnp.repeat(conv_w[0], SUBLANES, axis=0),
                  row(conv_b[0]), row(0.5 * conv_ln_g[0]), row(0.5 * conv_ln_b[0]),
                  half_bf(w_conv_o[0]), half_bf(w_mla_o[0]), half_bf(w_x_o[0]),
                  w_out[0].astype(BF16), row(final_norm_g))
```

```python
import functools

import jax
import jax.numpy as jnp
import numpy as np
from jax import lax
from jax.experimental import pallas as pl
from jax.experimental.pallas import tpu as pltpu

F32 = jnp.float32
BF16 = jnp.bfloat16

D_MODEL = 1024
N_MEM = 256
CONV_WIDTH = 512
CONV_KERNEL = 31
MLA_HEADS = 8
QK_NOPE = 64
QK_ROPE = 32
V_HEAD = 64
Q_LORA = 384
KV_LORA = 256
MLA_WIDTH = MLA_HEADS * V_HEAD
X_HEADS = 4
X_HEAD_DIM = 128
X_WIDTH = X_HEADS * X_HEAD_DIM
ROPE_THETA = 10000.0
EPS = 1e-6
IN_SPLITS = (CONV_WIDTH, CONV_WIDTH, CONV_WIDTH, Q_LORA, KV_LORA, QK_ROPE, MLA_WIDTH,
             X_WIDTH, X_WIDTH, 3 * D_MODEL)

LANES = 128
HEAD_PAD = LANES
BF16_ROWS = 16
V_AUG = V_HEAD + BF16_ROWS
ROPE_LO = QK_NOPE
ROPE_HALF = QK_ROPE // 2
ROPE_HI = QK_NOPE + QK_ROPE
MLA_SCALE = (QK_NOPE + QK_ROPE) ** -0.5
X_SCALE = X_HEAD_DIM ** -0.5
LOG2E = float(np.log2(np.e))
NEG = float(np.finfo(np.float32).min)

TS_QKV = 512
TQ = 512
TK = 512
TS_BLK = 512
HALO = 32
CONV_ROWS = 32
CONV_SPLIT = 2
OUT_SPLIT = 2
SUBLANES = 8
SHIFT_ROWS = TS_BLK + HALO - SUBLANES
VMEM_LIMIT = 56 * 1024 * 1024


def _rms(x, g):
    return x * lax.rsqrt(jnp.mean(x * x, axis=-1, keepdims=True) + EPS) * g


def _twice_sigmoid_of_double(h):
    return jnp.tanh(h) + 1.0


def _silu_of_double(h):
    return h * _twice_sigmoid_of_double(h)


def _dot(a, b):
    return jnp.dot(a, b, preferred_element_type=F32)


def _dot_nt(a, b):
    return lax.dot_general(a, b, (((1,), (1,)), ((), ())), preferred_element_type=F32)


def _mem_kv_kernel(mem_ref, g_ref, w_ref, k_ref, v_ref):
    hb = _rms(mem_ref[...], g_ref[...]).astype(BF16)
    kv = _dot(hb, w_ref[...])
    k_ref[...] = kv[:, :X_WIDTH].astype(BF16)
    v_ref[...] = kv[:, X_WIDTH:].astype(BF16)


def _mem_kv(mem, g, w):
    b, m, d = mem.shape
    return pl.pallas_call(
        _mem_kv_kernel,
        grid=(b,),
        in_specs=[pl.BlockSpec((None, m, d), lambda i: (i, 0, 0)),
                  pl.BlockSpec((1, d), lambda i: (0, 0)),
                  pl.BlockSpec((d, 2 * X_WIDTH), lambda i: (0, 0))],
        out_specs=[pl.BlockSpec((None, m, X_WIDTH), lambda i: (i, 0, 0)),
                   pl.BlockSpec((None, m, X_WIDTH), lambda i: (i, 0, 0))],
        out_shape=[jax.ShapeDtypeStruct((b, m, X_WIDTH), BF16)] * 2,
        compiler_params=pltpu.CompilerParams(dimension_semantics=("arbitrary",)),
        name="mem_kv",
    )(mem, g, w)


def _qkv_kernel(x_ref, pos_ref, ng_ref, wa_ref, qg_ref, wuq_ref, kvg_ref, wuk_ref, wuv_ref,
                invf_ref, qt_ref, k_ref, vt_ref, hb_ref):
    hb = _rms(x_ref[...], ng_ref[...]).astype(BF16)
    hb_ref[...] = hb
    za = _dot_nt(hb, wa_ref[...])
    cq = _rms(za[:, :Q_LORA], qg_ref[...]).astype(BF16)
    ckv = _rms(za[:, Q_LORA:Q_LORA + KV_LORA], kvg_ref[...]).astype(BF16)
    kr = za[:, Q_LORA + KV_LORA:]

    ang = invf_ref[...] * pos_ref[...].astype(F32)
    cos, sin = jnp.cos(ang), jnp.sin(ang)

    def rope_t(t):
        x1 = t[ROPE_LO:ROPE_LO + ROPE_HALF]
        x2 = t[ROPE_LO + ROPE_HALF:ROPE_HI]
        return jnp.concatenate(
            [t[:ROPE_LO], x1 * cos - x2 * sin, x1 * sin + x2 * cos, t[ROPE_HI:]], axis=0)

    kr_rot = rope_t(kr.T).T
    q = _dot(cq, wuq_ref[...])
    kn = _dot(ckv, wuk_ref[...])
    for h in range(MLA_HEADS):
        sl = slice(HEAD_PAD * h, HEAD_PAD * (h + 1))
        qt_ref[h] = (rope_t(q[:, sl].T) * (MLA_SCALE * LOG2E)).astype(BF16)
        k_ref[:, sl] = (kn[:, sl] + kr_rot).astype(BF16)
    vt = _dot(ckv, wuv_ref[...]).T.astype(BF16)
    ones_row = (lax.broadcasted_iota(jnp.int32, (V_AUG - V_HEAD, vt.shape[1]), 0) == 0).astype(BF16)
    for h in range(MLA_HEADS):
        vt_ref[h, 0:V_HEAD, :] = vt[V_HEAD * h:V_HEAD * (h + 1), :]
        vt_ref[h, V_HEAD:V_AUG, :] = ones_row


def _qkv(x, pos3, ng, wa, qg, wuq, kvg, wuk, wuv, invf):
    b, s, d = x.shape
    ts = TS_QKV
    const = lambda shape: pl.BlockSpec(shape, lambda i, j: (0,) * len(shape))
    hp = MLA_HEADS * HEAD_PAD
    return pl.pallas_call(
        _qkv_kernel,
        grid=(b, s // ts),
        in_specs=[pl.BlockSpec((None, ts, d), lambda i, j: (i, j, 0)),
                  pl.BlockSpec((None, 1, ts), lambda i, j: (i, 0, j)),
                  const((1, d)), const(wa.shape), const((1, Q_LORA)), const(wuq.shape),
                  const((1, KV_LORA)), const(wuk.shape), const(wuv.shape), const((ROPE_HALF, 1))],
        out_specs=[pl.BlockSpec((None, MLA_HEADS, HEAD_PAD, ts), lambda i, j: (i, 0, 0, j)),
                   pl.BlockSpec((None, ts, hp), lambda i, j: (i, j, 0)),
                   pl.BlockSpec((None, MLA_HEADS, V_AUG, ts), lambda i, j: (i, 0, 0, j)),
                   pl.BlockSpec((None, ts, d), lambda i, j: (i, j, 0))],
        out_shape=[jax.ShapeDtypeStruct((b, MLA_HEADS, HEAD_PAD, s), BF16),
                   jax.ShapeDtypeStruct((b, s, hp), BF16),
                   jax.ShapeDtypeStruct((b, MLA_HEADS, V_AUG, s), BF16),
                   jax.ShapeDtypeStruct((b, s, d), BF16)],
        compiler_params=pltpu.CompilerParams(dimension_semantics=("arbitrary", "arbitrary"),
                                             vmem_limit_bytes=VMEM_LIMIT),
        name="qkv",
    )(x, pos3, ng, wa, qg, wuq, kvg, wuk, wuv, invf)


ATTN_SLOTS = 4
ATTN_UNROLL = 4
ATTN_STRIP = 256
MAX_CHAINS = 4


def _attn_schedule(n_tiles):
    below = [(qi, j) for qi in range(n_tiles) for j in range(qi)]
    diag = [(qi, qi) for qi in range(n_tiles)]
    assert len(below) % ATTN_UNROLL == 0 and len(diag) % ATTN_UNROLL == 0
    table = np.array(below + diag + [(0, 0)], np.int32).T
    return table, len(below), len(diag)


def _attn_kernel(n_below, n_diag, tbl_ref, qt_ref, k_ref, vt_ref, o_ref,
                 s_scr, mblk_scr, m_scr, acc_scr):
    heads = (0, 1)
    units = [(hh, c) for hh in heads for c in range(TQ // ATTN_STRIP)]

    def key_rows(c, diag):
        return min(TK, (c + 1) * ATTN_STRIP) if diag else TK

    def scores(f, slot, hh, c, diag):
        q0 = pl.multiple_of(tbl_ref[0, f] * TQ + c * ATTN_STRIP, ATTN_STRIP)
        k0 = pl.multiple_of(tbl_ref[1, f] * TK, TK)
        nk = key_rows(c, diag)
        k = k_ref[pl.ds(k0, nk), HEAD_PAD * hh:HEAD_PAD * (hh + 1)]
        s = _dot(k, qt_ref[hh, :, pl.ds(q0, ATTN_STRIP)])
        if diag:
            key = lax.broadcasted_iota(jnp.int32, (nk, ATTN_STRIP), 0)
            qry = lax.broadcasted_iota(jnp.int32, (nk, ATTN_STRIP), 1) + c * ATTN_STRIP
            s = jnp.where(key <= qry, s, NEG)
        s_scr[slot, hh, c, 0:nk, :] = s
        tiles = s.reshape(MAX_CHAINS, nk // (MAX_CHAINS * SUBLANES), SUBLANES, ATTN_STRIP)
        mblk_scr[slot, hh, c] = jnp.max(jnp.max(jnp.max(tiles, axis=1), axis=0),
                                        axis=0, keepdims=True)

    def softmax(f, slot, hh, c, diag):
        qi = tbl_ref[0, f]
        nk = key_rows(c, diag)
        m_old = m_scr[qi, hh, c]
        m_new = jnp.maximum(m_old, mblk_scr[slot, hh, c])
        m_scr[qi, hh, c] = m_new
        alpha = jnp.exp2(m_old - m_new)
        p = jnp.exp2((s_scr[slot, hh, c, 0:nk, :] - m_new).astype(BF16))
        return alpha, p

    def values(f, hh, c, diag, alpha, p):
        qi = tbl_ref[0, f]
        k0 = pl.multiple_of(tbl_ref[1, f] * TK, TK)
        vt = vt_ref[hh, :, pl.ds(k0, key_rows(c, diag))]
        acc_scr[qi, hh, c] = alpha * acc_scr[qi, hh, c] + _dot(vt, p)

    def trip(f, first, unroll, diag, last_next_diag):
        for u in range(unroll):
            slot, nxt = (first + u) % ATTN_SLOTS, (first + u + 1) % ATTN_SLOTS
            next_diag = last_next_diag if u == unroll - 1 else diag
            for hh, c in units:
                alpha, p = softmax(f + u, slot, hh, c, diag)
                scores(f + u + 1, nxt, hh, c, next_diag)
                values(f + u, hh, c, diag, alpha, p)

    def trips(first, count, unroll, diag, then_diag):
        assert count % unroll == 0 and (count == unroll or unroll % ATTN_SLOTS == 0)
        n_trips = count // unroll
        if n_trips > 1:
            pl.loop(0, n_trips - 1)(lambda t: trip(first + t * unroll, first, unroll, diag, diag))
        if n_trips > 0:
            trip(first + (n_trips - 1) * unroll, first, unroll, diag, then_diag)

    m_scr[...] = jnp.full(m_scr.shape, NEG, F32)
    acc_scr[...] = jnp.zeros(acc_scr.shape, F32)
    for hh, c in units:
        scores(0, 0, hh, c, n_below == 0)
    trips(0, n_below, ATTN_UNROLL, diag=False, then_diag=True)
    trips(n_below, n_diag, ATTN_UNROLL, diag=True, then_diag=True)

    @pl.loop(0, n_diag)
    def _(qi):
        ot = jnp.concatenate(
            [jnp.concatenate([acc_scr[qi, hh, c, 0:V_HEAD] / acc_scr[qi, hh, c, V_HEAD:V_HEAD + 1]
                              for c in range(TQ // ATTN_STRIP)], axis=1) for hh in heads],
            axis=0)
        o_ref[pl.ds(pl.multiple_of(qi * TQ, TQ), TQ), :] = ot.T.astype(BF16)


def _mla_attn(qt, kp, vt):
    b, s, _ = kp.shape
    pairs = MLA_HEADS // 2
    n_tiles = s // TQ
    n_strips = TQ // ATTN_STRIP
    table, n_below, n_diag = _attn_schedule(n_tiles)
    grid_spec = pltpu.PrefetchScalarGridSpec(
        num_scalar_prefetch=1,
        grid=(b, pairs),
        in_specs=[pl.BlockSpec((None, 2, HEAD_PAD, s), lambda i, p, tbl: (i, p, 0, 0)),
                  pl.BlockSpec((None, s, 2 * HEAD_PAD), lambda i, p, tbl: (i, 0, p)),
                  pl.BlockSpec((None, 2, V_AUG, s), lambda i, p, tbl: (i, p, 0, 0))],
        out_specs=pl.BlockSpec((None, s, LANES), lambda i, p, tbl: (i, 0, p)),
        scratch_shapes=[pltpu.VMEM((ATTN_SLOTS, 2, n_strips, TK, ATTN_STRIP), F32),
                        pltpu.VMEM((ATTN_SLOTS, 2, n_strips, 1, ATTN_STRIP), F32),
                        pltpu.VMEM((n_tiles, 2, n_strips, 1, ATTN_STRIP), F32),
                        pltpu.VMEM((n_tiles, 2, n_strips, V_AUG, ATTN_STRIP), F32)])
    return pl.pallas_call(
        functools.partial(_attn_kernel, n_below, n_diag),
        grid_spec=grid_spec,
        out_shape=jax.ShapeDtypeStruct((b, s, MLA_WIDTH), BF16),
        compiler_params=pltpu.CompilerParams(dimension_semantics=("arbitrary", "arbitrary"),
                                             vmem_limit_bytes=VMEM_LIMIT),
        name="mla_attn",
    )(jnp.asarray(table), qt, kp, vt)


_EDGES = tuple(int(e) for e in np.cumsum((0,) + IN_SPLITS))
assert all(e % BF16_ROWS == 0 for e in _EDGES)
(_C_VAL, _C_GLU, _C_GATE, _Q_DOWN, _KV_DOWN, _K_ROPE, _M_GATE, _X_Q, _X_GATE, _G0, D_IN) = _EDGES


def _block_kernel(x_ref, hb_ref, o_mla_ref, kx_ref, vx_ref, wt_ref, bg_ref, cw_ref, cb_ref,
                  lng_ref, lnb_ref, wco_ref, wmo_ref, wxo_ref, wout_ref, fg_ref,
                  out_ref, ubuf, ush, conv_scr, part_scr, cgate_scr, g0_scr):
    ts = TS_BLK

    def proj(lo, width):
        return _dot_nt(hb_ref[...], wt_ref[lo:lo + width, :])

    def gate(i, lo, width):
        c0 = i * D_MODEL + lo
        return _twice_sigmoid_of_double(proj(_G0 + c0, width) + bg_ref[:, c0:c0 + width])

    @pl.when(pl.program_id(1) == 0)
    def _():
        ubuf[0:HALO, :] = jnp.zeros((HALO, CONV_WIDTH), F32)

    cw_ = CONV_WIDTH // CONV_SPLIT
    first = HALO - (CONV_KERNEL - 1)

    def glu(g):
        cs = slice(g * cw_, (g + 1) * cw_)
        ubuf[HALO:HALO + ts, cs] = (proj(_C_VAL + g * cw_, cw_)
                                    * _twice_sigmoid_of_double(proj(_C_GLU + g * cw_, cw_)))
        groups = SHIFT_ROWS // SUBLANES
        this_tile = ubuf[0:SHIFT_ROWS, cs].reshape(groups, SUBLANES, cw_)
        next_tile = ubuf[SUBLANES:SUBLANES + SHIFT_ROWS, cs].reshape(groups, SUBLANES, cw_)
        sub = lax.broadcasted_iota(jnp.int32, (1, SUBLANES, cw_), 1)
        for r in range(1, SUBLANES):
            picked = jnp.where(sub >= r, this_tile, next_tile)
            ush[r - 1, :, cs] = pltpu.roll(picked, SUBLANES - r, 1).reshape(SHIFT_ROWS, cw_)

    def conv_chunk(g, i):
        cs = slice(g * cw_, (g + 1) * cw_)
        r0 = i * CONV_ROWS
        groups = CONV_ROWS // SUBLANES
        acc = jnp.zeros((groups, SUBLANES, cw_), F32) + cb_ref[:, cs]
        for kk in range(CONV_KERNEL):
            r = (first + kk) % SUBLANES
            row = r0 + first + kk - r
            tap = ubuf[row:row + CONV_ROWS, cs] if r == 0 else ush[r - 1, row:row + CONV_ROWS, cs]
            w = cw_ref[kk * SUBLANES:(kk + 1) * SUBLANES, cs]
            acc = acc + tap.reshape(groups, SUBLANES, cw_) * w
        conv_scr[r0:r0 + CONV_ROWS, cs] = acc.reshape(CONV_ROWS, cw_)

    st = {}
    half = D_MODEL // 2

    def mla_gate():
        gate_m = _silu_of_double(proj(_M_GATE, MLA_WIDTH))
        st["a_mla"] = (o_mla_ref[...].astype(F32) * gate_m).astype(BF16)

    def mla_out():
        st["y_mla"] = _dot(st["a_mla"], wmo_ref[...])

    def mla_merge(lo):
        part_scr[:, lo:lo + half] = gate(1, lo, half) * st["y_mla"][:, lo:lo + half]

    def x_query():
        st["xq"] = proj(_X_Q, X_WIDTH).astype(BF16)
        st["ox"] = []

    def x_heads(h0):
        for h in (h0, h0 + 1):
            hs = slice(X_HEAD_DIM * h, X_HEAD_DIM * (h + 1))
            s = _dot_nt(st["xq"][:, hs], kx_ref[:, hs]) * X_SCALE
            p = jnp.exp(s - jnp.max(s, axis=-1, keepdims=True))
            p = p / jnp.sum(p, axis=-1, keepdims=True)
            st["ox"].append(_dot(p.astype(BF16), vx_ref[:, hs]))

    def x_gate():
        st["xg"] = _silu_of_double(proj(_X_GATE, X_WIDTH))

    def x_out():
        a_x = jnp.concatenate(st["ox"], axis=-1) * st["xg"]
        st["y_x"] = _dot(a_x.astype(BF16), wxo_ref[...])

    def x_merge(lo):
        part_scr[:, lo:lo + half] += gate(2, lo, half) * st["y_x"][:, lo:lo + half]

    def conv_gate():
        cgate_scr[...] = _silu_of_double(proj(_C_GATE, CONV_WIDTH))

    def conv_merge_gate(lo):
        g0_scr[:, lo:lo + half] = gate(0, lo, half)

    side = [mla_gate, mla_out, functools.partial(mla_merge, 0), functools.partial(mla_merge, half),
            x_query, functools.partial(x_heads, 0), functools.partial(x_heads, 2), x_gate, x_out,
            functools.partial(x_merge, 0), functools.partial(x_merge, half), conv_gate,
            functools.partial(conv_merge_gate, 0), functools.partial(conv_merge_gate, half)]
    side = [functools.partial(glu, g) for g in range(1, CONV_SPLIT)] + side
    chunks = [(g, i) for g in range(CONV_SPLIT) for i in range(ts // CONV_ROWS)]
    glu(0)
    for n, (g, i) in enumerate(chunks):
        conv_chunk(g, i)
        for j, piece in enumerate(side):
            if j * len(chunks) // len(side) == n:
                piece()
    ubuf[0:HALO, :] = ubuf[ts:ts + HALO, :]

    conv = conv_scr[...]
    mu = jnp.mean(conv, axis=-1, keepdims=True)
    cen = conv - mu
    var = jnp.mean(cen * cen, axis=-1, keepdims=True)
    half_ln = cen * lax.rsqrt(var + EPS) * lng_ref[...] + lnb_ref[...]
    a_conv = _silu_of_double(half_ln) * cgate_scr[...]
    y_conv = _dot(a_conv.astype(BF16), wco_ref[...])
    for r0 in range(0, ts, ts // OUT_SPLIT):
        rows = slice(r0, r0 + ts // OUT_SPLIT)
        merged = g0_scr[rows, :] * y_conv[rows, :] + part_scr[rows, :]
        y = x_ref[rows, :] + _dot(merged.astype(BF16), wout_ref[...])
        out_ref[rows, :] = _rms(y, fg_ref[...])


def _block(x, hb, o_mla, kx, vx, wt, bg, cw, cb, lng, lnb, wco, wmo, wxo, wout, fg):
    b, s, d = x.shape
    ts = TS_BLK
    const = lambda shape: pl.BlockSpec(shape, lambda i, j: (0,) * len(shape),
                                       pipeline_mode=pl.Buffered(1))
    return pl.pallas_call(
        _block_kernel,
        grid=(b, s // ts),
        in_specs=[pl.BlockSpec((None, ts, d), lambda i, j: (i, j, 0)),
                  pl.BlockSpec((None, ts, d), lambda i, j: (i, j, 0)),
                  pl.BlockSpec((None, ts, MLA_WIDTH), lambda i, j: (i, j, 0)),
                  pl.BlockSpec((None, N_MEM, X_WIDTH), lambda i, j: (i, 0, 0)),
                  pl.BlockSpec((None, N_MEM, X_WIDTH), lambda i, j: (i, 0, 0)),
                  const(wt.shape), const(bg.shape),
                  const(cw.shape),
                  const((1, CONV_WIDTH)), const((1, CONV_WIDTH)), const((1, CONV_WIDTH)),
                  const(wco.shape), const(wmo.shape), const(wxo.shape), const(wout.shape),
                  const((1, d))],
        out_specs=pl.BlockSpec((None, ts, d), lambda i, j: (i, j, 0)),
        out_shape=jax.ShapeDtypeStruct((b, s, d), F32),
        scratch_shapes=[pltpu.VMEM((HALO + ts, CONV_WIDTH), F32),
                        pltpu.VMEM((SUBLANES - 1, SHIFT_ROWS, CONV_WIDTH), F32),
                        pltpu.VMEM((ts, CONV_WIDTH), F32),
                        pltpu.VMEM((ts, d), F32),
                        pltpu.VMEM((ts, CONV_WIDTH), F32),
                        pltpu.VMEM((ts, d), F32)],
        compiler_params=pltpu.CompilerParams(dimension_semantics=("arbitrary", "arbitrary"),
                                             vmem_limit_bytes=VMEM_LIMIT),
        name="block",
    )(x, hb, o_mla, kx, vx, wt, bg, cw, cb, lng, lnb, wco, wmo, wxo, wout, fg)


def _inv_freq_column():
    inv = ROPE_THETA ** (-jnp.arange(0, QK_ROPE, 2, dtype=F32) / QK_ROPE)
    return inv.reshape(ROPE_HALF, 1)


def kernel(x, mem, positions, norm_g, w_in, b_gate, conv_w, conv_b, conv_ln_g, conv_ln_b,
           w_conv_o, q_norm_g, w_uq, kv_norm_g, w_ukv, w_mla_o, mem_norm_g, w_mem_kv, w_x_o,
           w_out, final_norm_g):
    assert norm_g.shape[0] == 1, "single-layer trunk"
    b, s, d = x.shape
    row = lambda v: v.reshape(1, -1)
    halved = np.ones((D_IN, 1), np.float32)
    for lo, hi in ((_C_VAL, _Q_DOWN), (_M_GATE, _X_Q), (_X_GATE, D_IN)):
        halved[lo:hi] = 0.5
    wt = (w_in[0].T * halved).astype(BF16)
    wa = jnp.concatenate([wt[_Q_DOWN:_K_ROPE],
                          jnp.pad(wt[_K_ROPE:_M_GATE], ((ROPE_LO, LANES - ROPE_HI), (0, 0)))], axis=0)
    wuq = jnp.pad(w_uq[0].reshape(Q_LORA, MLA_HEADS, QK_NOPE + QK_ROPE),
                  ((0, 0), (0, 0), (0, HEAD_PAD - QK_NOPE - QK_ROPE)))
    wuq = wuq.reshape(Q_LORA, MLA_HEADS * HEAD_PAD).astype(BF16)
    wukv = w_ukv[0].reshape(KV_LORA, MLA_HEADS, QK_NOPE + V_HEAD)
    wuk = jnp.pad(wukv[:, :, :QK_NOPE], ((0, 0), (0, 0), (0, HEAD_PAD - QK_NOPE)))
    wuk = wuk.reshape(KV_LORA, MLA_HEADS * HEAD_PAD).astype(BF16)
    wuv = wukv[:, :, QK_NOPE:].reshape(KV_LORA, MLA_WIDTH).astype(BF16)

    kx, vx = _mem_kv(mem, row(mem_norm_g[0]), w_mem_kv[0].astype(BF16))
    qt, kp, vt, hb = _qkv(x, positions.reshape(b, 1, s), row(norm_g[0]), wa, row(q_norm_g[0]),
                          wuq, row(kv_norm_g[0]), wuk, wuv, _inv_freq_column())
    o_mla = _mla_attn(qt, kp, vt)
    half_bf = lambda w: (0.5 * w).astype(BF16)
    return _block(x, hb, o_mla, kx, vx, wt, row(0.5 * b_gate[0]),
                  jnp.repeat(conv_w[0], SUBLANES, axis=0),
                  row(conv_b[0]), row(0.5 * conv_ln_g[0]), row(0.5 * conv_ln_b[0]),
                  half_bf(w_conv_o[0]), half_bf(w_mla_o[0]), half_bf(w_x_o[0]),
                  w_out[0].astype(BF16), row(final_norm_g))
```

```python
import functools

import jax
import jax.numpy as jnp
import numpy as np
from jax import lax
from jax.experimental import pallas as pl
from jax.experimental.pallas import tpu as pltpu

F32 = jnp.float32
BF16 = jnp.bfloat16

D_MODEL = 1024
N_MEM = 256
CONV_WIDTH = 512
CONV_KERNEL = 31
MLA_HEADS = 8
QK_NOPE = 64
QK_ROPE = 32
V_HEAD = 64
Q_LORA = 384
KV_LORA = 256
MLA_WIDTH = MLA_HEADS * V_HEAD
X_HEADS = 4
X_HEAD_DIM = 128
X_WIDTH = X_HEADS * X_HEAD_DIM
ROPE_THETA = 10000.0
EPS = 1e-6
IN_SPLITS = (CONV_WIDTH, CONV_WIDTH, CONV_WIDTH, Q_LORA, KV_LORA, QK_ROPE, MLA_WIDTH,
             X_WIDTH, X_WIDTH, 3 * D_MODEL)

LANES = 128
HEAD_PAD = LANES
BF16_ROWS = 16
V_AUG = V_HEAD + BF16_ROWS
ROPE_LO = QK_NOPE
ROPE_HALF = QK_ROPE // 2
ROPE_HI = QK_NOPE + QK_ROPE
MLA_SCALE = (QK_NOPE + QK_ROPE) ** -0.5
X_SCALE = X_HEAD_DIM ** -0.5
LOG2E = float(np.log2(np.e))
NEG = float(np.finfo(np.float32).min)

TS_QKV = 512
TQ = 512
TK = 512
TS_BLK = 512
HALO = 32
CONV_ROWS = 32
CONV_SPLIT = 2
OUT_SPLIT = 2
SUBLANES = 8
SHIFT_ROWS = TS_BLK + HALO - SUBLANES
VMEM_LIMIT = 56 * 1024 * 1024


def _rms(x, g):
    return x * lax.rsqrt(jnp.mean(x * x, axis=-1, keepdims=True) + EPS) * g


def _twice_sigmoid_of_double(h):
    return jnp.tanh(h) + 1.0


def _silu_of_double(h):
    return h * _twice_sigmoid_of_double(h)


def _dot(a, b):
    return jnp.dot(a, b, preferred_element_type=F32)


def _dot_nt(a, b):
    return lax.dot_general(a, b, (((1,), (1,)), ((), ())), preferred_element_type=F32)


def _mem_kv_kernel(mem_ref, g_ref, w_ref, k_ref, v_ref):
    hb = _rms(mem_ref[...], g_ref[...]).astype(BF16)
    kv = _dot(hb, w_ref[...])
    k_ref[...] = kv[:, :X_WIDTH].astype(BF16)
    v_ref[...] = kv[:, X_WIDTH:].astype(BF16)


def _mem_kv(mem, g, w):
    b, m, d = mem.shape
    return pl.pallas_call(
        _mem_kv_kernel,
        grid=(b,),
        in_specs=[pl.BlockSpec((None, m, d), lambda i: (i, 0, 0)),
                  pl.BlockSpec((1, d), lambda i: (0, 0)),
                  pl.BlockSpec((d, 2 * X_WIDTH), lambda i: (0, 0))],
        out_specs=[pl.BlockSpec((None, m, X_WIDTH), lambda i: (i, 0, 0)),
                   pl.BlockSpec((None, m, X_WIDTH), lambda i: (i, 0, 0))],
        out_shape=[jax.ShapeDtypeStruct((b, m, X_WIDTH), BF16)] * 2,
        compiler_params=pltpu.CompilerParams(dimension_semantics=("arbitrary",)),
        name="mem_kv",
    )(mem, g, w)


def _qkv_kernel(x_ref, pos_ref, ng_ref, wa_ref, qg_ref, wuq_ref, kvg_ref, wuk_ref, wuv_ref,
                invf_ref, qt_ref, k_ref, vt_ref, hb_ref):
    hb = _rms(x_ref[...], ng_ref[...]).astype(BF16)
    hb_ref[...] = hb
    za = _dot_nt(hb, wa_ref[...])
    cq = _rms(za[:, :Q_LORA], qg_ref[...]).astype(BF16)
    ckv = _rms(za[:, Q_LORA:Q_LORA + KV_LORA], kvg_ref[...]).astype(BF16)
    kr = za[:, Q_LORA + KV_LORA:]

    ang = invf_ref[...] * pos_ref[...].astype(F32)
    cos, sin = jnp.cos(ang), jnp.sin(ang)

    def rope_t(t):
        x1 = t[ROPE_LO:ROPE_LO + ROPE_HALF]
        x2 = t[ROPE_LO + ROPE_HALF:ROPE_HI]
        return jnp.concatenate(
            [t[:ROPE_LO], x1 * cos - x2 * sin, x1 * sin + x2 * cos, t[ROPE_HI:]], axis=0)

    kr_rot = rope_t(kr.T).T
    q = _dot(cq, wuq_ref[...])
    kn = _dot(ckv, wuk_ref[...])
    for h in range(MLA_HEADS):
        sl = slice(HEAD_PAD * h, HEAD_PAD * (h + 1))
        qt_ref[h] = (rope_t(q[:, sl].T) * (MLA_SCALE * LOG2E)).astype(BF16)
        k_ref[:, sl] = (kn[:, sl] + kr_rot).astype(BF16)
    vt = _dot(ckv, wuv_ref[...]).T.astype(BF16)
    ones_row = (lax.broadcasted_iota(jnp.int32, (V_AUG - V_HEAD, vt.shape[1]), 0) == 0).astype(BF16)
    for h in range(MLA_HEADS):
        vt_ref[h, 0:V_HEAD, :] = vt[V_HEAD * h:V_HEAD * (h + 1), :]
        vt_ref[h, V_HEAD:V_AUG, :] = ones_row


def _qkv(x, pos3, ng, wa, qg, wuq, kvg, wuk, wuv, invf):
    b, s, d = x.shape
    ts = TS_QKV
    const = lambda shape: pl.BlockSpec(shape, lambda i, j: (0,) * len(shape))
    hp = MLA_HEADS * HEAD_PAD
    return pl.pallas_call(
        _qkv_kernel,
        grid=(b, s // ts),
        in_specs=[pl.BlockSpec((None, ts, d), lambda i, j: (i, j, 0)),
                  pl.BlockSpec((None, 1, ts), lambda i, j: (i, 0, j)),
                  const((1, d)), const(wa.shape), const((1, Q_LORA)), const(wuq.shape),
                  const((1, KV_LORA)), const(wuk.shape), const(wuv.shape), const((ROPE_HALF, 1))],
        out_specs=[pl.BlockSpec((None, MLA_HEADS, HEAD_PAD, ts), lambda i, j: (i, 0, 0, j)),
                   pl.BlockSpec((None, ts, hp), lambda i, j: (i, j, 0)),
                   pl.BlockSpec((None, MLA_HEADS, V_AUG, ts), lambda i, j: (i, 0, 0, j)),
                   pl.BlockSpec((None, ts, d), lambda i, j: (i, j, 0))],
        out_shape=[jax.ShapeDtypeStruct((b, MLA_HEADS, HEAD_PAD, s), BF16),
                   jax.ShapeDtypeStruct((b, s, hp), BF16),
                   jax.ShapeDtypeStruct((b, MLA_HEADS, V_AUG, s), BF16),
                   jax.ShapeDtypeStruct((b, s, d), BF16)],
        compiler_params=pltpu.CompilerParams(dimension_semantics=("arbitrary", "arbitrary"),
                                             vmem_limit_bytes=VMEM_LIMIT),
        name="qkv",
    )(x, pos3, ng, wa, qg, wuq, kvg, wuk, wuv, invf)


ATTN_SLOTS = 4
ATTN_UNROLL = 4
ATTN_STRIP = 256
MAX_CHAINS = 4


def _attn_schedule(n_tiles):
    below = [(qi, j) for qi in range(n_tiles) for j in range(qi)]
    diag = [(qi, qi) for qi in range(n_tiles)]
    assert len(below) % ATTN_UNROLL == 0 and len(diag) % ATTN_UNROLL == 0
    table = np.array(below + diag + [(0, 0)], np.int32).T
    return table, len(below), len(diag)


def _attn_kernel(n_below, n_diag, tbl_ref, qt_ref, k_ref, vt_ref, o_ref,
                 s_scr, mblk_scr, m_scr, acc_scr):
    heads = (0, 1)
    units = [(hh, c) for hh in heads for c in range(TQ // ATTN_STRIP)]

    def key_rows(c, diag):
        return min(TK, (c + 1) * ATTN_STRIP) if diag else TK

    def scores(f, slot, hh, c, diag):
        q0 = pl.multiple_of(tbl_ref[0, f] * TQ + c * ATTN_STRIP, ATTN_STRIP)
        k0 = pl.multiple_of(tbl_ref[1, f] * TK, TK)
        nk = key_rows(c, diag)
        k = k_ref[pl.ds(k0, nk), HEAD_PAD * hh:HEAD_PAD * (hh + 1)]
        s = _dot(k, qt_ref[hh, :, pl.ds(q0, ATTN_STRIP)])
        if diag:
            key = lax.broadcasted_iota(jnp.int32, (nk, ATTN_STRIP), 0)
            qry = lax.broadcasted_iota(jnp.int32, (nk, ATTN_STRIP), 1) + c * ATTN_STRIP
            s = jnp.where(key <= qry, s, NEG)
        s_scr[slot, hh, c, 0:nk, :] = s
        tiles = s.reshape(MAX_CHAINS, nk // (MAX_CHAINS * SUBLANES), SUBLANES, ATTN_STRIP)
        mblk_scr[slot, hh, c] = jnp.max(jnp.max(jnp.max(tiles, axis=1), axis=0),
                                        axis=0, keepdims=True)

    def softmax(f, slot, hh, c, diag):
        qi = tbl_ref[0, f]
        nk = key_rows(c, diag)
        m_old = m_scr[qi, hh, c]
        m_new = jnp.maximum(m_old, mblk_scr[slot, hh, c])
        m_scr[qi, hh, c] = m_new
        alpha = jnp.exp2(m_old - m_new)
        p = jnp.exp2((s_scr[slot, hh, c, 0:nk, :] - m_new).astype(BF16))
        return alpha, p

    def values(f, hh, c, diag, alpha, p):
        qi = tbl_ref[0, f]
        k0 = pl.multiple_of(tbl_ref[1, f] * TK, TK)
        vt = vt_ref[hh, :, pl.ds(k0, key_rows(c, diag))]
        acc_scr[qi, hh, c] = alpha * acc_scr[qi, hh, c] + _dot(vt, p)

    def trip(f, first, unroll, diag, last_next_diag):
        for u in range(unroll):
            slot, nxt = (first + u) % ATTN_SLOTS, (first + u + 1) % ATTN_SLOTS
            next_diag = last_next_diag if u == unroll - 1 else diag
            for hh, c in units:
                alpha, p = softmax(f + u, slot, hh, c, diag)
                scores(f + u + 1, nxt, hh, c, next_diag)
                values(f + u, hh, c, diag, alpha, p)

    def trips(first, count, unroll, diag, then_diag):
        assert count % unroll == 0 and (count == unroll or unroll % ATTN_SLOTS == 0)
        n_trips = count // unroll
        if n_trips > 1:
            pl.loop(0, n_trips - 1)(lambda t: trip(first + t * unroll, first, unroll, diag, diag))
        if n_trips > 0:
            trip(first + (n_trips - 1) * unroll, first, unroll, diag, then_diag)

    m_scr[...] = jnp.full(m_scr.shape, NEG, F32)
    acc_scr[...] = jnp.zeros(acc_scr.shape, F32)
    for hh, c in units:
        scores(0, 0, hh, c, n_below == 0)
    trips(0, n_below, ATTN_UNROLL, diag=False, then_diag=True)
    trips(n_below, n_diag, ATTN_UNROLL, diag=True, then_diag=True)

    @pl.loop(0, n_diag)
    def _(qi):
        ot = jnp.concatenate(
            [jnp.concatenate([acc_scr[qi, hh, c, 0:V_HEAD] / acc_scr[qi, hh, c, V_HEAD:V_HEAD + 1]
                              for c in range(TQ // ATTN_STRIP)], axis=1) for hh in heads],
            axis=0)
        o_ref[pl.ds(pl.multiple_of(qi * TQ, TQ), TQ), :] = ot.T.astype(BF16)


def _mla_attn(qt, kp, vt):
    b, s, _ = kp.shape
    pairs = MLA_HEADS // 2
    n_tiles = s // TQ
    n_strips = TQ // ATTN_STRIP
    table, n_below, n_diag = _attn_schedule(n_tiles)
    grid_spec = pltpu.PrefetchScalarGridSpec(
        num_scalar_prefetch=1,
        grid=(b, pairs),
        in_specs=[pl.BlockSpec((None, 2, HEAD_PAD, s), lambda i, p, tbl: (i, p, 0, 0)),
                  pl.BlockSpec((None, s, 2 * HEAD_PAD), lambda i, p, tbl: (i, 0, p)),
                  pl.BlockSpec((None, 2, V_AUG, s), lambda i, p, tbl: (i, p, 0, 0))],
        out_specs=pl.BlockSpec((None, s, LANES), lambda i, p, tbl: (i, 0, p)),
        scratch_shapes=[pltpu.VMEM((ATTN_SLOTS, 2, n_strips, TK, ATTN_STRIP), F32),
                        pltpu.VMEM((ATTN_SLOTS, 2, n_strips, 1, ATTN_STRIP), F32),
                        pltpu.VMEM((n_tiles, 2, n_strips, 1, ATTN_STRIP), F32),
                        pltpu.VMEM((n_tiles, 2, n_strips, V_AUG, ATTN_STRIP), F32)])
    return pl.pallas_call(
        functools.partial(_attn_kernel, n_below, n_diag),
        grid_spec=grid_spec,
        out_shape=jax.ShapeDtypeStruct((b, s, MLA_WIDTH), BF16),
        compiler_params=pltpu.CompilerParams(dimension_semantics=("arbitrary", "arbitrary"),
                                             vmem_limit_bytes=VMEM_LIMIT),
        name="mla_attn",
    )(jnp.asarray(table), qt, kp, vt)


_EDGES = tuple(int(e) for e in np.cumsum((0,) + IN_SPLITS))
assert all(e % BF16_ROWS == 0 for e in _EDGES)
(_C_VAL, _C_GLU, _C_GATE, _Q_DOWN, _KV_DOWN, _K_ROPE, _M_GATE, _X_Q, _X_GATE, _G0, D_IN) = _EDGES


def _block_kernel(x_ref, hb_ref, o_mla_ref, kx_ref, vx_ref, wt_ref, bg_ref, cw_ref, cb_ref,
                  lng_ref, lnb_ref, wco_ref, wmo_ref, wxo_ref, wout_ref, fg_ref,
                  out_ref, ubuf, ush, conv_scr, part_scr, cgate_scr, g0_scr):
    ts = TS_BLK

    def proj(lo, width):
        return _dot_nt(hb_ref[...], wt_ref[lo:lo + width, :])

    def gate(i, lo, width):
        c0 = i * D_MODEL + lo
        return _twice_sigmoid_of_double(proj(_G0 + c0, width) + bg_ref[:, c0:c0 + width])

    @pl.when(pl.program_id(1) == 0)
    def _():
        ubuf[0:HALO, :] = jnp.zeros((HALO, CONV_WIDTH), F32)

    cw_ = CONV_WIDTH // CONV_SPLIT
    first = HALO - (CONV_KERNEL - 1)

    def glu(g):
        cs = slice(g * cw_, (g + 1) * cw_)
        ubuf[HALO:HALO + ts, cs] = (proj(_C_VAL + g * cw_, cw_)
                                    * _twice_sigmoid_of_double(proj(_C_GLU + g * cw_, cw_)))
        groups = SHIFT_ROWS // SUBLANES
        this_tile = ubuf[0:SHIFT_ROWS, cs].reshape(groups, SUBLANES, cw_)
        next_tile = ubuf[SUBLANES:SUBLANES + SHIFT_ROWS, cs].reshape(groups, SUBLANES, cw_)
        sub = lax.broadcasted_iota(jnp.int32, (1, SUBLANES, cw_), 1)
        for r in range(1, SUBLANES):
            picked = jnp.where(sub >= r, this_tile, next_tile)
            ush[r - 1, :, cs] = pltpu.roll(picked, SUBLANES - r, 1).reshape(SHIFT_ROWS, cw_)

    def conv_chunk(g, i):
        cs = slice(g * cw_, (g + 1) * cw_)
        r0 = i * CONV_ROWS
        groups = CONV_ROWS // SUBLANES
        acc = jnp.zeros((groups, SUBLANES, cw_), F32) + cb_ref[:, cs]
        for kk in range(CONV_KERNEL):
            r = (first + kk) % SUBLANES
            row = r0 + first + kk - r
            tap = ubuf[row:row + CONV_ROWS, cs] if r == 0 else ush[r - 1, row:row + CONV_ROWS, cs]
            w = cw_ref[kk * SUBLANES:(kk + 1) * SUBLANES, cs]
            acc = acc + tap.reshape(groups, SUBLANES, cw_) * w
        conv_scr[r0:r0 + CONV_ROWS, cs] = acc.reshape(CONV_ROWS, cw_)

    st = {}
    half = D_MODEL // 2

    def mla_gate():
        gate_m = _silu_of_double(proj(_M_GATE, MLA_WIDTH))
        st["a_mla"] = (o_mla_ref[...].astype(F32) * gate_m).astype(BF16)

    def mla_out():
        st["y_mla"] = _dot(st["a_mla"], wmo_ref[...])

    def mla_merge(lo):
        part_scr[:, lo:lo + half] = gate(1, lo, half) * st["y_mla"][:, lo:lo + half]

    def x_query():
        st["xq"] = proj(_X_Q, X_WIDTH).astype(BF16)
        st["ox"] = []

    def x_heads(h0):
        for h in (h0, h0 + 1):
            hs = slice(X_HEAD_DIM * h, X_HEAD_DIM * (h + 1))
            s = _dot_nt(st["xq"][:, hs], kx_ref[:, hs])
            p = jnp.exp2((s - jnp.max(s, axis=-1, keepdims=True)) * (X_SCALE * LOG2E))
            inv = 1.0 / jnp.sum(p, axis=-1, keepdims=True)
            st["ox"].append(_dot(p.astype(BF16), vx_ref[:, hs]) * inv)

    def x_gate():
        st["xg"] = _silu_of_double(proj(_X_GATE, X_WIDTH))

    def x_out():
        a_x = jnp.concatenate(st["ox"], axis=-1) * st["xg"]
        st["y_x"] = _dot(a_x.astype(BF16), wxo_ref[...])

    def x_merge(lo):
        part_scr[:, lo:lo + half] += gate(2, lo, half) * st["y_x"][:, lo:lo + half]

    def conv_gate():
        cgate_scr[...] = _silu_of_double(proj(_C_GATE, CONV_WIDTH))

    def conv_merge_gate(lo):
        g0_scr[:, lo:lo + half] = gate(0, lo, half)

    side = [mla_gate, mla_out, functools.partial(mla_merge, 0), functools.partial(mla_merge, half),
            x_query, functools.partial(x_heads, 0), functools.partial(x_heads, 2), x_gate, x_out,
            functools.partial(x_merge, 0), functools.partial(x_merge, half), conv_gate,
            functools.partial(conv_merge_gate, 0), functools.partial(conv_merge_gate, half)]
    side = [functools.partial(glu, g) for g in range(1, CONV_SPLIT)] + side
    chunks = [(g, i) for g in range(CONV_SPLIT) for i in range(ts // CONV_ROWS)]
    glu(0)
    for n, (g, i) in enumerate(chunks):
        conv_chunk(g, i)
        for j, piece in enumerate(side):
            if j * len(chunks) // len(side) == n:
                piece()
    ubuf[0:HALO, :] = ubuf[ts:ts + HALO, :]

    conv = conv_scr[...]
    mu = jnp.mean(conv, axis=-1, keepdims=True)
    cen = conv - mu
    var = jnp.mean(cen * cen, axis=-1, keepdims=True)
    half_ln = cen * lax.rsqrt(var + EPS) * lng_ref[...] + lnb_ref[...]
    a_conv = _silu_of_double(half_ln) * cgate_scr[...]
    y_conv = _dot(a_conv.astype(BF16), wco_ref[...])
    for r0 in range(0, ts, ts // OUT_SPLIT):
        rows = slice(r0, r0 + ts // OUT_SPLIT)
        merged = g0_scr[rows, :] * y_conv[rows, :] + part_scr[rows, :]
        y = x_ref[rows, :] + _dot(merged.astype(BF16), wout_ref[...])
        out_ref[rows, :] = _rms(y, fg_ref[...])


def _block(x, hb, o_mla, kx, vx, wt, bg, cw, cb, lng, lnb, wco, wmo, wxo, wout, fg):
    b, s, d = x.shape
    ts = TS_BLK
    const = lambda shape: pl.BlockSpec(shape, lambda i, j: (0,) * len(shape),
                                       pipeline_mode=pl.Buffered(1))
    return pl.pallas_call(
        _block_kernel,
        grid=(b, s // ts),
        in_specs=[pl.BlockSpec((None, ts, d), lambda i, j: (i, j, 0)),
                  pl.BlockSpec((None, ts, d), lambda i, j: (i, j, 0)),
                  pl.BlockSpec((None, ts, MLA_WIDTH), lambda i, j: (i, j, 0)),
                  pl.BlockSpec((None, N_MEM, X_WIDTH), lambda i, j: (i, 0, 0)),
                  pl.BlockSpec((None, N_MEM, X_WIDTH), lambda i, j: (i, 0, 0)),
                  const(wt.shape), const(bg.shape),
                  const(cw.shape),
                  const((1, CONV_WIDTH)), const((1, CONV_WIDTH)), const((1, CONV_WIDTH)),
                  const(wco.shape), const(wmo.shape), const(wxo.shape), const(wout.shape),
                  const((1, d))],
        out_specs=pl.BlockSpec((None, ts, d), lambda i, j: (i, j, 0)),
        out_shape=jax.ShapeDtypeStruct((b, s, d), F32),
        scratch_shapes=[pltpu.VMEM((HALO + ts, CONV_WIDTH), F32),
                        pltpu.VMEM((SUBLANES - 1, SHIFT_ROWS, CONV_WIDTH), F32),
                        pltpu.VMEM((ts, CONV_WIDTH), F32),
                        pltpu.VMEM((ts, d), F32),
                        pltpu.VMEM((ts, CONV_WIDTH), F32),
                        pltpu.VMEM((ts, d), F32)],
        compiler_params=pltpu.CompilerParams(dimension_semantics=("arbitrary", "arbitrary"),
                                             vmem_limit_bytes=VMEM_LIMIT),
        name="block",
    )(x, hb, o_mla, kx, vx, wt, bg, cw, cb, lng, lnb, wco, wmo, wxo, wout, fg)


def _inv_freq_column():
    inv = ROPE_THETA ** (-jnp.arange(0, QK_ROPE, 2, dtype=F32) / QK_ROPE)
    return inv.reshape(ROPE_HALF, 1)


def kernel(x, mem, positions, norm_g, w_in, b_gate, conv_w, conv_b, conv_ln_g, conv_ln_b,
           w_conv_o, q_norm_g, w_uq, kv_norm_g, w_ukv, w_mla_o, mem_norm_g, w_mem_kv, w_x_o,
           w_out, final_norm_g):
    assert norm_g.shape[0] == 1, "single-layer trunk"
    b, s, d = x.shape
    row = lambda v: v.reshape(1, -1)
    halved = np.ones((D_IN, 1), np.float32)
    for lo, hi in ((_C_VAL, _Q_DOWN), (_M_GATE, _X_Q), (_X_GATE, D_IN)):
        halved[lo:hi] = 0.5
    wt = (w_in[0].T * halved).astype(BF16)
    wa = jnp.concatenate([wt[_Q_DOWN:_K_ROPE],
                          jnp.pad(wt[_K_ROPE:_M_GATE], ((ROPE_LO, LANES - ROPE_HI), (0, 0)))], axis=0)
    wuq = jnp.pad(w_uq[0].reshape(Q_LORA, MLA_HEADS, QK_NOPE + QK_ROPE),
                  ((0, 0), (0, 0), (0, HEAD_PAD - QK_NOPE - QK_ROPE)))
    wuq = wuq.reshape(Q_LORA, MLA_HEADS * HEAD_PAD).astype(BF16)
    wukv = w_ukv[0].reshape(KV_LORA, MLA_HEADS, QK_NOPE + V_HEAD)
    wuk = jnp.pad(wukv[:, :, :QK_NOPE], ((0, 0), (0, 0), (0, HEAD_PAD - QK_NOPE)))
    wuk = wuk.reshape(KV_LORA, MLA_HEADS * HEAD_PAD).astype(BF16)
    wuv = wukv[:, :, QK_NOPE:].reshape(KV_LORA, MLA_WIDTH).astype(BF16)

    kx, vx = _mem_kv(mem, row(mem_norm_g[0]), w_mem_kv[0].astype(BF16))
    qt, kp, vt, hb = _qkv(x, positions.reshape(b, 1, s), row(norm_g[0]), wa, row(q_norm_g[0]),
                          wuq, row(kv_norm_g[0]), wuk, wuv, _inv_freq_column())
    o_mla = _mla_attn(qt, kp, vt)
    half_bf = lambda w: (0.5 * w).astype(BF16)
    return _block(x, hb, o_mla, kx, vx, wt, row(0.5 * b_gate[0]),
                  jnp.repeat(conv_w[0], SUBLANES, axis=0),
                  row(conv_b[0]), row(0.5 * conv_ln_g[0]), row(0.5 * conv_ln_b[0]),
                  half_bf(w_conv_o[0]), half_bf(w_mla_o[0]), half_bf(w_x_o[0]),
                  w_out[0].astype(BF16), row(final_norm_g))
```

```python
import functools

import jax
import jax.numpy as jnp
import numpy as np
from jax import lax
from jax.experimental import pallas as pl
from jax.experimental.pallas import tpu as pltpu

F32 = jnp.float32
BF16 = jnp.bfloat16

D_MODEL = 1024
N_MEM = 256
CONV_WIDTH = 512
CONV_KERNEL = 31
MLA_HEADS = 8
QK_NOPE = 64
QK_ROPE = 32
V_HEAD = 64
Q_LORA = 384
KV_LORA = 256
MLA_WIDTH = MLA_HEADS * V_HEAD
X_HEADS = 4
X_HEAD_DIM = 128
X_WIDTH = X_HEADS * X_HEAD_DIM
ROPE_THETA = 10000.0
EPS = 1e-6
IN_SPLITS = (CONV_WIDTH, CONV_WIDTH, CONV_WIDTH, Q_LORA, KV_LORA, QK_ROPE, MLA_WIDTH,
             X_WIDTH, X_WIDTH, 3 * D_MODEL)

LANES = 128
HEAD_PAD = LANES
BF16_ROWS = 16
V_AUG = V_HEAD + BF16_ROWS
ROPE_LO = QK_NOPE
ROPE_HALF = QK_ROPE // 2
ROPE_HI = QK_NOPE + QK_ROPE
MLA_SCALE = (QK_NOPE + QK_ROPE) ** -0.5
X_SCALE = X_HEAD_DIM ** -0.5
LOG2E = float(np.log2(np.e))
NEG = float(np.finfo(np.float32).min)

TS_QKV = 1024
TQ = 512
TK = 512
TS_BLK = 512
HALO = 32
CONV_ROWS = 32
CONV_SPLIT = 2
OUT_SPLIT = 2
SUBLANES = 8
SHIFT_ROWS = TS_BLK + HALO - SUBLANES
VMEM_LIMIT = 56 * 1024 * 1024


def _rms(x, g):
    return x * lax.rsqrt(jnp.mean(x * x, axis=-1, keepdims=True) + EPS) * g


def _twice_sigmoid_of_double(h):
    return jnp.tanh(h) + 1.0


def _silu_of_double(h):
    return h * _twice_sigmoid_of_double(h)


def _dot(a, b):
    return jnp.dot(a, b, preferred_element_type=F32)


def _dot_nt(a, b):
    return lax.dot_general(a, b, (((1,), (1,)), ((), ())), preferred_element_type=F32)


def _mem_kv_kernel(mem_ref, g_ref, w_ref, k_ref, v_ref):
    hb = _rms(mem_ref[...], g_ref[...]).astype(BF16)
    kv = _dot(hb, w_ref[...])
    k_ref[...] = kv[:, :X_WIDTH].astype(BF16)
    v_ref[...] = kv[:, X_WIDTH:].astype(BF16)


def _mem_kv(mem, g, w):
    b, m, d = mem.shape
    return pl.pallas_call(
        _mem_kv_kernel,
        grid=(b,),
        in_specs=[pl.BlockSpec((None, m, d), lambda i: (i, 0, 0)),
                  pl.BlockSpec((1, d), lambda i: (0, 0)),
                  pl.BlockSpec((d, 2 * X_WIDTH), lambda i: (0, 0))],
        out_specs=[pl.BlockSpec((None, m, X_WIDTH), lambda i: (i, 0, 0)),
                   pl.BlockSpec((None, m, X_WIDTH), lambda i: (i, 0, 0))],
        out_shape=[jax.ShapeDtypeStruct((b, m, X_WIDTH), BF16)] * 2,
        compiler_params=pltpu.CompilerParams(dimension_semantics=("arbitrary",)),
        name="mem_kv",
    )(mem, g, w)


def _qkv_kernel(x_ref, pos_ref, ng_ref, wa_ref, qg_ref, wuq_ref, kvg_ref, wuk_ref, wuv_ref,
                invf_ref, qt_ref, k_ref, vt_ref, hb_ref):
    hb = _rms(x_ref[...], ng_ref[...]).astype(BF16)
    hb_ref[...] = hb
    za = _dot_nt(hb, wa_ref[...])
    cq = _rms(za[:, :Q_LORA], qg_ref[...]).astype(BF16)
    ckv = _rms(za[:, Q_LORA:Q_LORA + KV_LORA], kvg_ref[...]).astype(BF16)
    kr = za[:, Q_LORA + KV_LORA:]

    ang = invf_ref[...] * pos_ref[...].astype(F32)
    cos, sin = jnp.cos(ang), jnp.sin(ang)

    def rope_t(t):
        x1 = t[ROPE_LO:ROPE_LO + ROPE_HALF]
        x2 = t[ROPE_LO + ROPE_HALF:ROPE_HI]
        return jnp.concatenate(
            [t[:ROPE_LO], x1 * cos - x2 * sin, x1 * sin + x2 * cos, t[ROPE_HI:]], axis=0)

    kr_rot = rope_t(kr.T).T
    q = _dot(cq, wuq_ref[...])
    kn = _dot(ckv, wuk_ref[...])
    for h in range(MLA_HEADS):
        sl = slice(HEAD_PAD * h, HEAD_PAD * (h + 1))
        qt_ref[h] = (rope_t(q[:, sl].T) * (MLA_SCALE * LOG2E)).astype(BF16)
        k_ref[:, sl] = (kn[:, sl] + kr_rot).astype(BF16)
    vt = _dot(ckv, wuv_ref[...]).T.astype(BF16)
    ones_row = (lax.broadcasted_iota(jnp.int32, (V_AUG - V_HEAD, vt.shape[1]), 0) == 0).astype(BF16)
    for h in range(MLA_HEADS):
        vt_ref[h, 0:V_HEAD, :] = vt[V_HEAD * h:V_HEAD * (h + 1), :]
        vt_ref[h, V_HEAD:V_AUG, :] = ones_row


def _qkv(x, pos3, ng, wa, qg, wuq, kvg, wuk, wuv, invf):
    b, s, d = x.shape
    ts = TS_QKV
    const = lambda shape: pl.BlockSpec(shape, lambda i, j: (0,) * len(shape))
    hp = MLA_HEADS * HEAD_PAD
    return pl.pallas_call(
        _qkv_kernel,
        grid=(b, s // ts),
        in_specs=[pl.BlockSpec((None, ts, d), lambda i, j: (i, j, 0)),
                  pl.BlockSpec((None, 1, ts), lambda i, j: (i, 0, j)),
                  const((1, d)), const(wa.shape), const((1, Q_LORA)), const(wuq.shape),
                  const((1, KV_LORA)), const(wuk.shape), const(wuv.shape), const((ROPE_HALF, 1))],
        out_specs=[pl.BlockSpec((None, MLA_HEADS, HEAD_PAD, ts), lambda i, j: (i, 0, 0, j)),
                   pl.BlockSpec((None, ts, hp), lambda i, j: (i, j, 0)),
                   pl.BlockSpec((None, MLA_HEADS, V_AUG, ts), lambda i, j: (i, 0, 0, j)),
                   pl.BlockSpec((None, ts, d), lambda i, j: (i, j, 0))],
        out_shape=[jax.ShapeDtypeStruct((b, MLA_HEADS, HEAD_PAD, s), BF16),
                   jax.ShapeDtypeStruct((b, s, hp), BF16),
                   jax.ShapeDtypeStruct((b, MLA_HEADS, V_AUG, s), BF16),
                   jax.ShapeDtypeStruct((b, s, d), BF16)],
        compiler_params=pltpu.CompilerParams(dimension_semantics=("arbitrary", "arbitrary"),
                                             vmem_limit_bytes=VMEM_LIMIT),
        name="qkv",
    )(x, pos3, ng, wa, qg, wuq, kvg, wuk, wuv, invf)


ATTN_SLOTS = 4
ATTN_UNROLL = 4
ATTN_STRIP = 256
MAX_CHAINS = 4


def _attn_schedule(n_tiles):
    below = [(qi, j) for qi in range(n_tiles) for j in range(qi)]
    diag = [(qi, qi) for qi in range(n_tiles)]
    assert len(below) % ATTN_UNROLL == 0 and len(diag) % ATTN_UNROLL == 0
    table = np.array(below + diag + [(0, 0)], np.int32).T
    return table, len(below), len(diag)


def _attn_kernel(n_below, n_diag, tbl_ref, qt_ref, k_ref, vt_ref, o_ref,
                 s_scr, mblk_scr, m_scr, acc_scr):
    heads = (0, 1)
    units = [(hh, c) for hh in heads for c in range(TQ // ATTN_STRIP)]

    def key_rows(c, diag):
        return min(TK, (c + 1) * ATTN_STRIP) if diag else TK

    def scores(f, slot, hh, c, diag):
        q0 = pl.multiple_of(tbl_ref[0, f] * TQ + c * ATTN_STRIP, ATTN_STRIP)
        k0 = pl.multiple_of(tbl_ref[1, f] * TK, TK)
        nk = key_rows(c, diag)
        k = k_ref[pl.ds(k0, nk), HEAD_PAD * hh:HEAD_PAD * (hh + 1)]
        s = _dot(k, qt_ref[hh, :, pl.ds(q0, ATTN_STRIP)])
        if diag:
            key = lax.broadcasted_iota(jnp.int32, (nk, ATTN_STRIP), 0)
            qry = lax.broadcasted_iota(jnp.int32, (nk, ATTN_STRIP), 1) + c * ATTN_STRIP
            s = jnp.where(key <= qry, s, NEG)
        s_scr[slot, hh, c, 0:nk, :] = s
        tiles = s.reshape(MAX_CHAINS, nk // (MAX_CHAINS * SUBLANES), SUBLANES, ATTN_STRIP)
        mblk_scr[slot, hh, c] = jnp.max(jnp.max(jnp.max(tiles, axis=1), axis=0),
                                        axis=0, keepdims=True)

    def softmax(f, slot, hh, c, diag):
        qi = tbl_ref[0, f]
        nk = key_rows(c, diag)
        m_old = m_scr[qi, hh, c]
        m_new = jnp.maximum(m_old, mblk_scr[slot, hh, c])
        m_scr[qi, hh, c] = m_new
        alpha = jnp.exp2(m_old - m_new)
        p = jnp.exp2((s_scr[slot, hh, c, 0:nk, :] - m_new).astype(BF16))
        return alpha, p

    def values(f, hh, c, diag, alpha, p):
        qi = tbl_ref[0, f]
        k0 = pl.multiple_of(tbl_ref[1, f] * TK, TK)
        vt = vt_ref[hh, :, pl.ds(k0, key_rows(c, diag))]
        acc_scr[qi, hh, c] = alpha * acc_scr[qi, hh, c] + _dot(vt, p)

    def trip(f, first, unroll, diag, last_next_diag):
        for u in range(unroll):
            slot, nxt = (first + u) % ATTN_SLOTS, (first + u + 1) % ATTN_SLOTS
            next_diag = last_next_diag if u == unroll - 1 else diag
            for hh, c in units:
                alpha, p = softmax(f + u, slot, hh, c, diag)
                scores(f + u + 1, nxt, hh, c, next_diag)
                values(f + u, hh, c, diag, alpha, p)

    def trips(first, count, unroll, diag, then_diag):
        assert count % unroll == 0 and (count == unroll or unroll % ATTN_SLOTS == 0)
        n_trips = count // unroll
        if n_trips > 1:
            pl.loop(0, n_trips - 1)(lambda t: trip(first + t * unroll, first, unroll, diag, diag))
        if n_trips > 0:
            trip(first + (n_trips - 1) * unroll, first, unroll, diag, then_diag)

    m_scr[...] = jnp.full(m_scr.shape, NEG, F32)
    acc_scr[...] = jnp.zeros(acc_scr.shape, F32)
    for hh, c in units:
        scores(0, 0, hh, c, n_below == 0)
    trips(0, n_below, ATTN_UNROLL, diag=False, then_diag=True)
    trips(n_below, n_diag, ATTN_UNROLL, diag=True, then_diag=True)

    @pl.loop(0, n_diag)
    def _(qi):
        ot = jnp.concatenate(
            [jnp.concatenate([acc_scr[qi, hh, c, 0:V_HEAD] / acc_scr[qi, hh, c, V_HEAD:V_HEAD + 1]
                              for c in range(TQ // ATTN_STRIP)], axis=1) for hh in heads],
            axis=0)
        o_ref[pl.ds(pl.multiple_of(qi * TQ, TQ), TQ), :] = ot.T.astype(BF16)


def _mla_attn(qt, kp, vt):
    b, s, _ = kp.shape
    pairs = MLA_HEADS // 2
    n_tiles = s // TQ
    n_strips = TQ // ATTN_STRIP
    table, n_below, n_diag = _attn_schedule(n_tiles)
    grid_spec = pltpu.PrefetchScalarGridSpec(
        num_scalar_prefetch=1,
        grid=(b, pairs),
        in_specs=[pl.BlockSpec((None, 2, HEAD_PAD, s), lambda i, p, tbl: (i, p, 0, 0)),
                  pl.BlockSpec((None, s, 2 * HEAD_PAD), lambda i, p, tbl: (i, 0, p)),
                  pl.BlockSpec((None, 2, V_AUG, s), lambda i, p, tbl: (i, p, 0, 0))],
        out_specs=pl.BlockSpec((None, s, LANES), lambda i, p, tbl: (i, 0, p)),
        scratch_shapes=[pltpu.VMEM((ATTN_SLOTS, 2, n_strips, TK, ATTN_STRIP), F32),
                        pltpu.VMEM((ATTN_SLOTS, 2, n_strips, 1, ATTN_STRIP), F32),
                        pltpu.VMEM((n_tiles, 2, n_strips, 1, ATTN_STRIP), F32),
                        pltpu.VMEM((n_tiles, 2, n_strips, V_AUG, ATTN_STRIP), F32)])
    return pl.pallas_call(
        functools.partial(_attn_kernel, n_below, n_diag),
        grid_spec=grid_spec,
        out_shape=jax.ShapeDtypeStruct((b, s, MLA_WIDTH), BF16),
        compiler_params=pltpu.CompilerParams(dimension_semantics=("arbitrary", "arbitrary"),
                                             vmem_limit_bytes=VMEM_LIMIT),
        name="mla_attn",
    )(jnp.asarray(table), qt, kp, vt)


_EDGES = tuple(int(e) for e in np.cumsum((0,) + IN_SPLITS))
assert all(e % BF16_ROWS == 0 for e in _EDGES)
(_C_VAL, _C_GLU, _C_GATE, _Q_DOWN, _KV_DOWN, _K_ROPE, _M_GATE, _X_Q, _X_GATE, _G0, D_IN) = _EDGES


def _block_kernel(x_ref, hb_ref, o_mla_ref, kx_ref, vx_ref, wt_ref, bg_ref, cw_ref, cb_ref,
                  lng_ref, lnb_ref, wco_ref, wmo_ref, wxo_ref, wout_ref, fg_ref,
                  out_ref, ubuf, ush, conv_scr, part_scr, cgate_scr, g0_scr):
    ts = TS_BLK

    def proj(lo, width):
        return _dot_nt(hb_ref[...], wt_ref[lo:lo + width, :])

    def gate(i, lo, width):
        c0 = i * D_MODEL + lo
        return _twice_sigmoid_of_double(proj(_G0 + c0, width) + bg_ref[:, c0:c0 + width])

    @pl.when(pl.program_id(1) == 0)
    def _():
        ubuf[0:HALO, :] = jnp.zeros((HALO, CONV_WIDTH), F32)

    cw_ = CONV_WIDTH // CONV_SPLIT
    first = HALO - (CONV_KERNEL - 1)

    def glu(g):
        cs = slice(g * cw_, (g + 1) * cw_)
        ubuf[HALO:HALO + ts, cs] = (proj(_C_VAL + g * cw_, cw_)
                                    * _twice_sigmoid_of_double(proj(_C_GLU + g * cw_, cw_)))
        groups = SHIFT_ROWS // SUBLANES
        this_tile = ubuf[0:SHIFT_ROWS, cs].reshape(groups, SUBLANES, cw_)
        next_tile = ubuf[SUBLANES:SUBLANES + SHIFT_ROWS, cs].reshape(groups, SUBLANES, cw_)
        sub = lax.broadcasted_iota(jnp.int32, (1, SUBLANES, cw_), 1)
        for r in range(1, SUBLANES):
            picked = jnp.where(sub >= r, this_tile, next_tile)
            ush[r - 1, :, cs] = pltpu.roll(picked, SUBLANES - r, 1).reshape(SHIFT_ROWS, cw_)

    def conv_chunk(g, i):
        cs = slice(g * cw_, (g + 1) * cw_)
        r0 = i * CONV_ROWS
        groups = CONV_ROWS // SUBLANES
        acc = jnp.zeros((groups, SUBLANES, cw_), F32) + cb_ref[:, cs]
        for kk in range(CONV_KERNEL):
            r = (first + kk) % SUBLANES
            row = r0 + first + kk - r
            tap = ubuf[row:row + CONV_ROWS, cs] if r == 0 else ush[r - 1, row:row + CONV_ROWS, cs]
            w = cw_ref[kk * SUBLANES:(kk + 1) * SUBLANES, cs]
            acc = acc + tap.reshape(groups, SUBLANES, cw_) * w
        conv_scr[r0:r0 + CONV_ROWS, cs] = acc.reshape(CONV_ROWS, cw_)

    st = {}
    half = D_MODEL // 2

    def mla_gate():
        gate_m = _silu_of_double(proj(_M_GATE, MLA_WIDTH))
        st["a_mla"] = (o_mla_ref[...].astype(F32) * gate_m).astype(BF16)

    def mla_out():
        st["y_mla"] = _dot(st["a_mla"], wmo_ref[...])

    def mla_merge(lo):
        part_scr[:, lo:lo + half] = gate(1, lo, half) * st["y_mla"][:, lo:lo + half]

    def x_query():
        st["xq"] = proj(_X_Q, X_WIDTH).astype(BF16)
        st["ox"] = []

    def x_heads(h0):
        for h in (h0, h0 + 1):
            hs = slice(X_HEAD_DIM * h, X_HEAD_DIM * (h + 1))
            s = _dot_nt(st["xq"][:, hs], kx_ref[:, hs])
            p = jnp.exp2((s - jnp.max(s, axis=-1, keepdims=True)) * (X_SCALE * LOG2E))
            inv = 1.0 / jnp.sum(p, axis=-1, keepdims=True)
            st["ox"].append(_dot(p.astype(BF16), vx_ref[:, hs]) * inv)

    def x_gate():
        st["xg"] = _silu_of_double(proj(_X_GATE, X_WIDTH))

    def x_out():
        a_x = jnp.concatenate(st["ox"], axis=-1) * st["xg"]
        st["y_x"] = _dot(a_x.astype(BF16), wxo_ref[...])

    def x_merge(lo):
        part_scr[:, lo:lo + half] += gate(2, lo, half) * st["y_x"][:, lo:lo + half]

    def conv_gate():
        cgate_scr[...] = _silu_of_double(proj(_C_GATE, CONV_WIDTH))

    def conv_merge_gate(lo):
        g0_scr[:, lo:lo + half] = gate(0, lo, half)

    side = [mla_gate, mla_out, functools.partial(mla_merge, 0), functools.partial(mla_merge, half),
            x_query, functools.partial(x_heads, 0), functools.partial(x_heads, 2), x_gate, x_out,
            functools.partial(x_merge, 0), functools.partial(x_merge, half), conv_gate,
            functools.partial(conv_merge_gate, 0), functools.partial(conv_merge_gate, half)]
    side = [functools.partial(glu, g) for g in range(1, CONV_SPLIT)] + side
    chunks = [(g, i) for g in range(CONV_SPLIT) for i in range(ts // CONV_ROWS)]
    glu(0)
    for n, (g, i) in enumerate(chunks):
        conv_chunk(g, i)
        for j, piece in enumerate(side):
            if j * len(chunks) // len(side) == n:
                piece()
    ubuf[0:HALO, :] = ubuf[ts:ts + HALO, :]

    conv = conv_scr[...]
    mu = jnp.mean(conv, axis=-1, keepdims=True)
    cen = conv - mu
    var = jnp.mean(cen * cen, axis=-1, keepdims=True)
    half_ln = cen * lax.rsqrt(var + EPS) * lng_ref[...] + lnb_ref[...]
    a_conv = _silu_of_double(half_ln) * cgate_scr[...]
    y_conv = _dot(a_conv.astype(BF16), wco_ref[...])
    for r0 in range(0, ts, ts // OUT_SPLIT):
        rows = slice(r0, r0 + ts // OUT_SPLIT)
        merged = g0_scr[rows, :] * y_conv[rows, :] + part_scr[rows, :]
        y = x_ref[rows, :] + _dot(merged.astype(BF16), wout_ref[...])
        out_ref[rows, :] = _rms(y, fg_ref[...])


def _block(x, hb, o_mla, kx, vx, wt, bg, cw, cb, lng, lnb, wco, wmo, wxo, wout, fg):
    b, s, d = x.shape
    ts = TS_BLK
    const = lambda shape: pl.BlockSpec(shape, lambda i, j: (0,) * len(shape),
                                       pipeline_mode=pl.Buffered(1))
    return pl.pallas_call(
        _block_kernel,
        grid=(b, s // ts),
        in_specs=[pl.BlockSpec((None, ts, d), lambda i, j: (i, j, 0)),
                  pl.BlockSpec((None, ts, d), lambda i, j: (i, j, 0)),
                  pl.BlockSpec((None, ts, MLA_WIDTH), lambda i, j: (i, j, 0)),
                  pl.BlockSpec((None, N_MEM, X_WIDTH), lambda i, j: (i, 0, 0)),
                  pl.BlockSpec((None, N_MEM, X_WIDTH), lambda i, j: (i, 0, 0)),
                  const(wt.shape), const(bg.shape),
                  const(cw.shape),
                  const((1, CONV_WIDTH)), const((1, CONV_WIDTH)), const((1, CONV_WIDTH)),
                  const(wco.shape), const(wmo.shape), const(wxo.shape), const(wout.shape),
                  const((1, d))],
        out_specs=pl.BlockSpec((None, ts, d), lambda i, j: (i, j, 0)),
        out_shape=jax.ShapeDtypeStruct((b, s, d), F32),
        scratch_shapes=[pltpu.VMEM((HALO + ts, CONV_WIDTH), F32),
                        pltpu.VMEM((SUBLANES - 1, SHIFT_ROWS, CONV_WIDTH), F32),
                        pltpu.VMEM((ts, CONV_WIDTH), F32),
                        pltpu.VMEM((ts, d), F32),
                        pltpu.VMEM((ts, CONV_WIDTH), F32),
                        pltpu.VMEM((ts, d), F32)],
        compiler_params=pltpu.CompilerParams(dimension_semantics=("arbitrary", "arbitrary"),
                                             vmem_limit_bytes=VMEM_LIMIT),
        name="block",
    )(x, hb, o_mla, kx, vx, wt, bg, cw, cb, lng, lnb, wco, wmo, wxo, wout, fg)


def _inv_freq_column():
    inv = ROPE_THETA ** (-jnp.arange(0, QK_ROPE, 2, dtype=F32) / QK_ROPE)
    return inv.reshape(ROPE_HALF, 1)


def kernel(x, mem, positions, norm_g, w_in, b_gate, conv_w, conv_b, conv_ln_g, conv_ln_b,
           w_conv_o, q_norm_g, w_uq, kv_norm_g, w_ukv, w_mla_o, mem_norm_g, w_mem_kv, w_x_o,
           w_out, final_norm_g):
    assert norm_g.shape[0] == 1, "single-layer trunk"
    b, s, d = x.shape
    row = lambda v: v.reshape(1, -1)
    halved = np.ones((D_IN, 1), np.float32)
    for lo, hi in ((_C_VAL, _Q_DOWN), (_M_GATE, _X_Q), (_X_GATE, D_IN)):
        halved[lo:hi] = 0.5
    wt = (w_in[0].T * halved).astype(BF16)
    wa = jnp.concatenate([wt[_Q_DOWN:_K_ROPE],
                          jnp.pad(wt[_K_ROPE:_M_GATE], ((ROPE_LO, LANES - ROPE_HI), (0, 0)))], axis=0)
    wuq = jnp.pad(w_uq[0].reshape(Q_LORA, MLA_HEADS, QK_NOPE + QK_ROPE),
                  ((0, 0), (0, 0), (0, HEAD_PAD - QK_NOPE - QK_ROPE)))
    wuq = wuq.reshape(Q_LORA, MLA_HEADS * HEAD_PAD).astype(BF16)
    wukv = w_ukv[0].reshape(KV_LORA, MLA_HEADS, QK_NOPE + V_HEAD)
    wuk = jnp.pad(wukv[:, :, :QK_NOPE], ((0, 0), (0, 0), (0, HEAD_PAD - QK_NOPE)))
    wuk = wuk.reshape(KV_LORA, MLA_HEADS * HEAD_PAD).astype(BF16)
    wuv = wukv[:, :, QK_NOPE:].reshape(KV_LORA, MLA_WIDTH).astype(BF16)

    kx, vx = _mem_kv(mem, row(mem_norm_g[0]), w_mem_kv[0].astype(BF16))
    qt, kp, vt, hb = _qkv(x, positions.reshape(b, 1, s), row(norm_g[0]), wa, row(q_norm_g[0]),
                          wuq, row(kv_norm_g[0]), wuk, wuv, _inv_freq_column())
    o_mla = _mla_attn(qt, kp, vt)
    half_bf = lambda w: (0.5 * w).astype(BF16)
    return _block(x, hb, o_mla, kx, vx, wt, row(0.5 * b_gate[0]),
                  jnp.repeat(conv_w[0], SUBLANES, axis=0),
                  row(conv_b[0]), row(0.5 * conv_ln_g[0]), row(0.5 * conv_ln_b[0]),
                  half_bf(w_conv_o[0]), half_bf(w_mla_o[0]), half_bf(w_x_o[0]),
                  w_out[0].astype(BF16), row(final_norm_g))
```

```python
import functools

import jax
import jax.numpy as jnp
import numpy as np
from jax import lax
from jax.experimental import pallas as pl
from jax.experimental.pallas import tpu as pltpu

F32 = jnp.float32
BF16 = jnp.bfloat16

D_MODEL = 1024
N_MEM = 256
CONV_WIDTH = 512
CONV_KERNEL = 31
MLA_HEADS = 8
QK_NOPE = 64
QK_ROPE = 32
V_HEAD = 64
Q_LORA = 384
KV_LORA = 256
MLA_WIDTH = MLA_HEADS * V_HEAD
X_HEADS = 4
X_HEAD_DIM = 128
X_WIDTH = X_HEADS * X_HEAD_DIM
ROPE_THETA = 10000.0
EPS = 1e-6
IN_SPLITS = (CONV_WIDTH, CONV_WIDTH, CONV_WIDTH, Q_LORA, KV_LORA, QK_ROPE, MLA_WIDTH,
             X_WIDTH, X_WIDTH, 3 * D_MODEL)

LANES = 128
HEAD_PAD = LANES
BF16_ROWS = 16
V_AUG = V_HEAD + BF16_ROWS
ROPE_LO = QK_NOPE
ROPE_HALF = QK_ROPE // 2
ROPE_HI = QK_NOPE + QK_ROPE
MLA_SCALE = (QK_NOPE + QK_ROPE) ** -0.5
X_SCALE = X_HEAD_DIM ** -0.5
LOG2E = float(np.log2(np.e))
NEG = float(np.finfo(np.float32).min)

TS_QKV = 1024
TQ = 512
TK = 512
TS_BLK = 512
HALO = 32
CONV_ROWS = 32
CONV_SPLIT = 2
OUT_SPLIT = 2
SUBLANES = 8
SHIFT_ROWS = TS_BLK + HALO - SUBLANES
VMEM_LIMIT = 56 * 1024 * 1024


def _rms(x, g):
    return x * lax.rsqrt(jnp.mean(x * x, axis=-1, keepdims=True) + EPS) * g


def _twice_sigmoid_of_double(h):
    return jnp.tanh(h) + 1.0


def _silu_of_double(h):
    return h * _twice_sigmoid_of_double(h)


def _dot(a, b):
    return jnp.dot(a, b, preferred_element_type=F32)


def _dot_nt(a, b):
    return lax.dot_general(a, b, (((1,), (1,)), ((), ())), preferred_element_type=F32)


def _mem_kv_kernel(mem_ref, g_ref, w_ref, k_ref, v_ref):
    hb = _rms(mem_ref[...], g_ref[...]).astype(BF16)
    kv = _dot(hb, w_ref[...])
    k_ref[...] = kv[:, :X_WIDTH].astype(BF16)
    v_ref[...] = kv[:, X_WIDTH:].astype(BF16)


def _mem_kv(mem, g, w):
    b, m, d = mem.shape
    return pl.pallas_call(
        _mem_kv_kernel,
        grid=(b,),
        in_specs=[pl.BlockSpec((None, m, d), lambda i: (i, 0, 0)),
                  pl.BlockSpec((1, d), lambda i: (0, 0)),
                  pl.BlockSpec((d, 2 * X_WIDTH), lambda i: (0, 0))],
        out_specs=[pl.BlockSpec((None, m, X_WIDTH), lambda i: (i, 0, 0)),
                   pl.BlockSpec((None, m, X_WIDTH), lambda i: (i, 0, 0))],
        out_shape=[jax.ShapeDtypeStruct((b, m, X_WIDTH), BF16)] * 2,
        compiler_params=pltpu.CompilerParams(dimension_semantics=("arbitrary",)),
        name="mem_kv",
    )(mem, g, w)


def _qkv_kernel(x_ref, pos_ref, ng_ref, wa_ref, qg_ref, wuq_ref, kvg_ref, wuk_ref, wuv_ref,
                invf_ref, qt_ref, k_ref, vt_ref, hb_ref):
    hb = _rms(x_ref[...], ng_ref[...]).astype(BF16)
    hb_ref[...] = hb
    za = _dot_nt(hb, wa_ref[...])
    cq = _rms(za[:, :Q_LORA], qg_ref[...]).astype(BF16)
    ckv = _rms(za[:, Q_LORA:Q_LORA + KV_LORA], kvg_ref[...]).astype(BF16)
    kr = za[:, Q_LORA + KV_LORA:]

    ang = invf_ref[...] * pos_ref[...].astype(F32)
    cos, sin = jnp.cos(ang), jnp.sin(ang)

    def rope_t(t):
        x1 = t[ROPE_LO:ROPE_LO + ROPE_HALF]
        x2 = t[ROPE_LO + ROPE_HALF:ROPE_HI]
        return jnp.concatenate(
            [t[:ROPE_LO], x1 * cos - x2 * sin, x1 * sin + x2 * cos, t[ROPE_HI:]], axis=0)

    kr_rot = rope_t(kr.T).T
    q = _dot(cq, wuq_ref[...])
    kn = _dot(ckv, wuk_ref[...])
    for h in range(MLA_HEADS):
        sl = slice(HEAD_PAD * h, HEAD_PAD * (h + 1))
        qt_ref[h] = (rope_t(q[:, sl].T) * (MLA_SCALE * LOG2E)).astype(BF16)
        k_ref[:, sl] = (kn[:, sl] + kr_rot).astype(BF16)
    vt = _dot(ckv, wuv_ref[...]).T.astype(BF16)
    ones_row = (lax.broadcasted_iota(jnp.int32, (V_AUG - V_HEAD, vt.shape[1]), 0) == 0).astype(BF16)
    for h in range(MLA_HEADS):
        vt_ref[h, 0:V_HEAD, :] = vt[V_HEAD * h:V_HEAD * (h + 1), :]
        vt_ref[h, V_HEAD:V_AUG, :] = ones_row


def _qkv(x, pos3, ng, wa, qg, wuq, kvg, wuk, wuv, invf):
    b, s, d = x.shape
    ts = TS_QKV
    const = lambda shape: pl.BlockSpec(shape, lambda i, j: (0,) * len(shape))
    hp = MLA_HEADS * HEAD_PAD
    return pl.pallas_call(
        _qkv_kernel,
        grid=(b, s // ts),
        in_specs=[pl.BlockSpec((None, ts, d), lambda i, j: (i, j, 0)),
                  pl.BlockSpec((None, 1, ts), lambda i, j: (i, 0, j)),
                  const((1, d)), const(wa.shape), const((1, Q_LORA)), const(wuq.shape),
                  const((1, KV_LORA)), const(wuk.shape), const(wuv.shape), const((ROPE_HALF, 1))],
        out_specs=[pl.BlockSpec((None, MLA_HEADS, HEAD_PAD, ts), lambda i, j: (i, 0, 0, j)),
                   pl.BlockSpec((None, ts, hp), lambda i, j: (i, j, 0)),
                   pl.BlockSpec((None, MLA_HEADS, V_AUG, ts), lambda i, j: (i, 0, 0, j)),
                   pl.BlockSpec((None, ts, d), lambda i, j: (i, j, 0))],
        out_shape=[jax.ShapeDtypeStruct((b, MLA_HEADS, HEAD_PAD, s), BF16),
                   jax.ShapeDtypeStruct((b, s, hp), BF16),
                   jax.ShapeDtypeStruct((b, MLA_HEADS, V_AUG, s), BF16),
                   jax.ShapeDtypeStruct((b, s, d), BF16)],
        compiler_params=pltpu.CompilerParams(dimension_semantics=("arbitrary", "arbitrary"),
                                             vmem_limit_bytes=VMEM_LIMIT),
        name="qkv",
    )(x, pos3, ng, wa, qg, wuq, kvg, wuk, wuv, invf)


ATTN_SLOTS = 4
ATTN_UNROLL = 4
ATTN_STRIP = 256
MAX_CHAINS = 2


def _attn_schedule(n_tiles):
    below = [(qi, j) for qi in range(n_tiles) for j in range(qi)]
    diag = [(qi, qi) for qi in range(n_tiles)]
    assert len(below) % ATTN_UNROLL == 0 and len(diag) % ATTN_UNROLL == 0
    table = np.array(below + diag + [(0, 0)], np.int32).T
    return table, len(below), len(diag)


def _attn_kernel(n_below, n_diag, tbl_ref, qt_ref, k_ref, vt_ref, o_ref,
                 s_scr, mblk_scr, m_scr, acc_scr):
    heads = (0, 1)
    units = [(hh, c) for c in range(TQ // ATTN_STRIP) for hh in heads]

    def key_rows(c, diag):
        return min(TK, (c + 1) * ATTN_STRIP) if diag else TK

    def scores(f, slot, hh, c, diag):
        q0 = pl.multiple_of(tbl_ref[0, f] * TQ + c * ATTN_STRIP, ATTN_STRIP)
        k0 = pl.multiple_of(tbl_ref[1, f] * TK, TK)
        nk = key_rows(c, diag)
        k = k_ref[pl.ds(k0, nk), HEAD_PAD * hh:HEAD_PAD * (hh + 1)]
        s = _dot(k, qt_ref[hh, :, pl.ds(q0, ATTN_STRIP)])
        if diag:
            key = lax.broadcasted_iota(jnp.int32, (nk, ATTN_STRIP), 0)
            qry = lax.broadcasted_iota(jnp.int32, (nk, ATTN_STRIP), 1) + c * ATTN_STRIP
            s = jnp.where(key <= qry, s, NEG)
        s_scr[slot, hh, c, 0:nk, :] = s
        tiles = s.reshape(MAX_CHAINS, nk // (MAX_CHAINS * SUBLANES), SUBLANES, ATTN_STRIP)
        mblk_scr[slot, hh, c] = jnp.max(jnp.max(jnp.max(tiles, axis=1), axis=0),
                                        axis=0, keepdims=True)

    def softmax(f, slot, hh, c, diag):
        qi = tbl_ref[0, f]
        nk = key_rows(c, diag)
        m_old = m_scr[qi, hh, c]
        m_new = jnp.maximum(m_old, mblk_scr[slot, hh, c])
        m_scr[qi, hh, c] = m_new
        alpha = jnp.exp2(m_old - m_new)
        p = jnp.exp2((s_scr[slot, hh, c, 0:nk, :] - m_new).astype(BF16))
        return alpha, p

    def values(f, hh, c, diag, alpha, p):
        qi = tbl_ref[0, f]
        k0 = pl.multiple_of(tbl_ref[1, f] * TK, TK)
        vt = vt_ref[hh, :, pl.ds(k0, key_rows(c, diag))]
        acc_scr[qi, hh, c] = alpha * acc_scr[qi, hh, c] + _dot(vt, p)

    def trip(f, first, unroll, diag, last_next_diag):
        for u in range(unroll):
            slot, nxt = (first + u) % ATTN_SLOTS, (first + u + 1) % ATTN_SLOTS
            next_diag = last_next_diag if u == unroll - 1 else diag
            for hh, c in units:
                alpha, p = softmax(f + u, slot, hh, c, diag)
                scores(f + u + 1, nxt, hh, c, next_diag)
                values(f + u, hh, c, diag, alpha, p)

    def trips(first, count, unroll, diag, then_diag):
        assert count % unroll == 0 and (count == unroll or unroll % ATTN_SLOTS == 0)
        n_trips = count // unroll
        if n_trips > 1:
            pl.loop(0, n_trips - 1)(lambda t: trip(first + t * unroll, first, unroll, diag, diag))
        if n_trips > 0:
            trip(first + (n_trips - 1) * unroll, first, unroll, diag, then_diag)

    m_scr[...] = jnp.full(m_scr.shape, NEG, F32)
    acc_scr[...] = jnp.zeros(acc_scr.shape, F32)
    for hh, c in units:
        scores(0, 0, hh, c, n_below == 0)
    trips(0, n_below, ATTN_UNROLL, diag=False, then_diag=True)
    trips(n_below, n_diag, ATTN_UNROLL, diag=True, then_diag=True)

    @pl.loop(0, n_diag)
    def _(qi):
        ot = jnp.concatenate(
            [jnp.concatenate([acc_scr[qi, hh, c, 0:V_HEAD] / acc_scr[qi, hh, c, V_HEAD:V_HEAD + 1]
                              for c in range(TQ // ATTN_STRIP)], axis=1) for hh in heads],
            axis=0)
        o_ref[pl.ds(pl.multiple_of(qi * TQ, TQ), TQ), :] = ot.T.astype(BF16)


def _mla_attn(qt, kp, vt):
    b, s, _ = kp.shape
    pairs = MLA_HEADS // 2
    n_tiles = s // TQ
    n_strips = TQ // ATTN_STRIP
    table, n_below, n_diag = _attn_schedule(n_tiles)
    grid_spec = pltpu.PrefetchScalarGridSpec(
        num_scalar_prefetch=1,
        grid=(b, pairs),
        in_specs=[pl.BlockSpec((None, 2, HEAD_PAD, s), lambda i, p, tbl: (i, p, 0, 0)),
                  pl.BlockSpec((None, s, 2 * HEAD_PAD), lambda i, p, tbl: (i, 0, p)),
                  pl.BlockSpec((None, 2, V_AUG, s), lambda i, p, tbl: (i, p, 0, 0))],
        out_specs=pl.BlockSpec((None, s, LANES), lambda i, p, tbl: (i, 0, p)),
        scratch_shapes=[pltpu.VMEM((ATTN_SLOTS, 2, n_strips, TK, ATTN_STRIP), F32),
                        pltpu.VMEM((ATTN_SLOTS, 2, n_strips, 1, ATTN_STRIP), F32),
                        pltpu.VMEM((n_tiles, 2, n_strips, 1, ATTN_STRIP), F32),
                        pltpu.VMEM((n_tiles, 2, n_strips, V_AUG, ATTN_STRIP), F32)])
    return pl.pallas_call(
        functools.partial(_attn_kernel, n_below, n_diag),
        grid_spec=grid_spec,
        out_shape=jax.ShapeDtypeStruct((b, s, MLA_WIDTH), BF16),
        compiler_params=pltpu.CompilerParams(dimension_semantics=("arbitrary", "arbitrary"),
                                             vmem_limit_bytes=VMEM_LIMIT),
        name="mla_attn",
    )(jnp.asarray(table), qt, kp, vt)


_EDGES = tuple(int(e) for e in np.cumsum((0,) + IN_SPLITS))
assert all(e % BF16_ROWS == 0 for e in _EDGES)
(_C_VAL, _C_GLU, _C_GATE, _Q_DOWN, _KV_DOWN, _K_ROPE, _M_GATE, _X_Q, _X_GATE, _G0, D_IN) = _EDGES


def _block_kernel(x_ref, hb_ref, o_mla_ref, kx_ref, vx_ref, wt_ref, bg_ref, cw_ref, cb_ref,
                  lng_ref, lnb_ref, wco_ref, wmo_ref, wxo_ref, wout_ref, fg_ref,
                  out_ref, ubuf, ush, conv_scr, part_scr, cgate_scr, g0_scr):
    ts = TS_BLK

    def proj(lo, width):
        return _dot_nt(hb_ref[...], wt_ref[lo:lo + width, :])

    def gate(i, lo, width):
        c0 = i * D_MODEL + lo
        return _twice_sigmoid_of_double(proj(_G0 + c0, width) + bg_ref[:, c0:c0 + width])

    @pl.when(pl.program_id(1) == 0)
    def _():
        ubuf[0:HALO, :] = jnp.zeros((HALO, CONV_WIDTH), F32)

    cw_ = CONV_WIDTH // CONV_SPLIT
    first = HALO - (CONV_KERNEL - 1)

    def glu(g):
        cs = slice(g * cw_, (g + 1) * cw_)
        ubuf[HALO:HALO + ts, cs] = (proj(_C_VAL + g * cw_, cw_)
                                    * _twice_sigmoid_of_double(proj(_C_GLU + g * cw_, cw_)))
        groups = SHIFT_ROWS // SUBLANES
        this_tile = ubuf[0:SHIFT_ROWS, cs].reshape(groups, SUBLANES, cw_)
        next_tile = ubuf[SUBLANES:SUBLANES + SHIFT_ROWS, cs].reshape(groups, SUBLANES, cw_)
        sub = lax.broadcasted_iota(jnp.int32, (1, SUBLANES, cw_), 1)
        for r in range(1, SUBLANES):
            picked = jnp.where(sub >= r, this_tile, next_tile)
            ush[r - 1, :, cs] = pltpu.roll(picked, SUBLANES - r, 1).reshape(SHIFT_ROWS, cw_)

    def conv_chunk(g, i):
        cs = slice(g * cw_, (g + 1) * cw_)
        r0 = i * CONV_ROWS
        groups = CONV_ROWS // SUBLANES
        acc = jnp.zeros((groups, SUBLANES, cw_), F32) + cb_ref[:, cs]
        for kk in range(CONV_KERNEL):
            r = (first + kk) % SUBLANES
            row = r0 + first + kk - r
            tap = ubuf[row:row + CONV_ROWS, cs] if r == 0 else ush[r - 1, row:row + CONV_ROWS, cs]
            w = cw_ref[kk * SUBLANES:(kk + 1) * SUBLANES, cs]
            acc = acc + tap.reshape(groups, SUBLANES, cw_) * w
        conv_scr[r0:r0 + CONV_ROWS, cs] = acc.reshape(CONV_ROWS, cw_)

    st = {}
    half = D_MODEL // 2

    def mla_gate():
        gate_m = _silu_of_double(proj(_M_GATE, MLA_WIDTH))
        st["a_mla"] = (o_mla_ref[...].astype(F32) * gate_m).astype(BF16)

    def mla_out():
        st["y_mla"] = _dot(st["a_mla"], wmo_ref[...])

    def mla_merge(lo):
        part_scr[:, lo:lo + half] = gate(1, lo, half) * st["y_mla"][:, lo:lo + half]

    def x_query():
        st["xq"] = proj(_X_Q, X_WIDTH).astype(BF16)
        st["ox"] = []

    def x_heads(h0):
        for h in (h0, h0 + 1):
            hs = slice(X_HEAD_DIM * h, X_HEAD_DIM * (h + 1))
            s = _dot_nt(st["xq"][:, hs], kx_ref[:, hs])
            p = jnp.exp2((s - jnp.max(s, axis=-1, keepdims=True)) * (X_SCALE * LOG2E))
            inv = 1.0 / jnp.sum(p, axis=-1, keepdims=True)
            st["ox"].append(_dot(p.astype(BF16), vx_ref[:, hs]) * inv)

    def x_gate():
        st["xg"] = _silu_of_double(proj(_X_GATE, X_WIDTH))

    def x_out():
        a_x = jnp.concatenate(st["ox"], axis=-1) * st["xg"]
        st["y_x"] = _dot(a_x.astype(BF16), wxo_ref[...])

    def x_merge(lo):
        part_scr[:, lo:lo + half] += gate(2, lo, half) * st["y_x"][:, lo:lo + half]

    def conv_gate():
        cgate_scr[...] = _silu_of_double(proj(_C_GATE, CONV_WIDTH))

    def conv_merge_gate(lo):
        g0_scr[:, lo:lo + half] = gate(0, lo, half)

    side = [mla_gate, mla_out, functools.partial(mla_merge, 0), functools.partial(mla_merge, half),
            x_query, functools.partial(x_heads, 0), functools.partial(x_heads, 2), x_gate, x_out,
            functools.partial(x_merge, 0), functools.partial(x_merge, half), conv_gate,
            functools.partial(conv_merge_gate, 0), functools.partial(conv_merge_gate, half)]
    side = [functools.partial(glu, g) for g in range(1, CONV_SPLIT)] + side
    chunks = [(g, i) for g in range(CONV_SPLIT) for i in range(ts // CONV_ROWS)]
    glu(0)
    for n, (g, i) in enumerate(chunks):
        conv_chunk(g, i)
        for j, piece in enumerate(side):
            if j * len(chunks) // len(side) == n:
                piece()
    ubuf[0:HALO, :] = ubuf[ts:ts + HALO, :]

    conv = conv_scr[...]
    mu = jnp.mean(conv, axis=-1, keepdims=True)
    cen = conv - mu
    var = jnp.mean(cen * cen, axis=-1, keepdims=True)
    half_ln = cen * lax.rsqrt(var + EPS) * lng_ref[...] + lnb_ref[...]
    a_conv = _silu_of_double(half_ln) * cgate_scr[...]
    y_conv = _dot(a_conv.astype(BF16), wco_ref[...])
    for r0 in range(0, ts, ts // OUT_SPLIT):
        rows = slice(r0, r0 + ts // OUT_SPLIT)
        merged = g0_scr[rows, :] * y_conv[rows, :] + part_scr[rows, :]
        y = x_ref[rows, :] + _dot(merged.astype(BF16), wout_ref[...])
        out_ref[rows, :] = _rms(y, fg_ref[...])


def _block(x, hb, o_mla, kx, vx, wt, bg, cw, cb, lng, lnb, wco, wmo, wxo, wout, fg):
    b, s, d = x.shape
    ts = TS_BLK
    const = lambda shape: pl.BlockSpec(shape, lambda i, j: (0,) * len(shape),
                                       pipeline_mode=pl.Buffered(1))
    return pl.pallas_call(
        _block_kernel,
        grid=(b, s // ts),
        in_specs=[pl.BlockSpec((None, ts, d), lambda i, j: (i, j, 0)),
                  pl.BlockSpec((None, ts, d), lambda i, j: (i, j, 0)),
                  pl.BlockSpec((None, ts, MLA_WIDTH), lambda i, j: (i, j, 0)),
                  pl.BlockSpec((None, N_MEM, X_WIDTH), lambda i, j: (i, 0, 0)),
                  pl.BlockSpec((None, N_MEM, X_WIDTH), lambda i, j: (i, 0, 0)),
                  const(wt.shape), const(bg.shape),
                  const(cw.shape),
                  const((1, CONV_WIDTH)), const((1, CONV_WIDTH)), const((1, CONV_WIDTH)),
                  const(wco.shape), const(wmo.shape), const(wxo.shape), const(wout.shape),
                  const((1, d))],
        out_specs=pl.BlockSpec((None, ts, d), lambda i, j: (i, j, 0)),
        out_shape=jax.ShapeDtypeStruct((b, s, d), F32),
        scratch_shapes=[pltpu.VMEM((HALO + ts, CONV_WIDTH), F32),
                        pltpu.VMEM((SUBLANES - 1, SHIFT_ROWS, CONV_WIDTH), F32),
                        pltpu.VMEM((ts, CONV_WIDTH), F32),
                        pltpu.VMEM((ts, d), F32),
                        pltpu.VMEM((ts, CONV_WIDTH), F32),
                        pltpu.VMEM((ts, d), F32)],
        compiler_params=pltpu.CompilerParams(dimension_semantics=("arbitrary", "arbitrary"),
                                             vmem_limit_bytes=VMEM_LIMIT),
        name="block",
    )(x, hb, o_mla, kx, vx, wt, bg, cw, cb, lng, lnb, wco, wmo, wxo, wout, fg)


def _inv_freq_column():
    inv = ROPE_THETA ** (-jnp.arange(0, QK_ROPE, 2, dtype=F32) / QK_ROPE)
    return inv.reshape(ROPE_HALF, 1)


def kernel(x, mem, positions, norm_g, w_in, b_gate, conv_w, conv_b, conv_ln_g, conv_ln_b,
           w_conv_o, q_norm_g, w_uq, kv_norm_g, w_ukv, w_mla_o, mem_norm_g, w_mem_kv, w_x_o,
           w_out, final_norm_g):
    assert norm_g.shape[0] == 1, "single-layer trunk"
    b, s, d = x.shape
    row = lambda v: v.reshape(1, -1)
    halved = np.ones((D_IN, 1), np.float32)
    for lo, hi in ((_C_VAL, _Q_DOWN), (_M_GATE, _X_Q), (_X_GATE, D_IN)):
        halved[lo:hi] = 0.5
    wt = (w_in[0].T * halved).astype(BF16)
    wa = jnp.concatenate([wt[_Q_DOWN:_K_ROPE],
                          jnp.pad(wt[_K_ROPE:_M_GATE], ((ROPE_LO, LANES - ROPE_HI), (0, 0)))], axis=0)
    wuq = jnp.pad(w_uq[0].reshape(Q_LORA, MLA_HEADS, QK_NOPE + QK_ROPE),
                  ((0, 0), (0, 0), (0, HEAD_PAD - QK_NOPE - QK_ROPE)))
    wuq = wuq.reshape(Q_LORA, MLA_HEADS * HEAD_PAD).astype(BF16)
    wukv = w_ukv[0].reshape(KV_LORA, MLA_HEADS, QK_NOPE + V_HEAD)
    wuk = jnp.pad(wukv[:, :, :QK_NOPE], ((0, 0), (0, 0), (0, HEAD_PAD - QK_NOPE)))
    wuk = wuk.reshape(KV_LORA, MLA_HEADS * HEAD_PAD).astype(BF16)
    wuv = wukv[:, :, QK_NOPE:].reshape(KV_LORA, MLA_WIDTH).astype(BF16)

    kx, vx = _mem_kv(mem, row(mem_norm_g[0]), w_mem_kv[0].astype(BF16))
    qt, kp, vt, hb = _qkv(x, positions.reshape(b, 1, s), row(norm_g[0]), wa, row(q_norm_g[0]),
                          wuq, row(kv_norm_g[0]), wuk, wuv, _inv_freq_column())
    o_mla = _mla_attn(qt, kp, vt)
    half_bf = lambda w: (0.5 * w).astype(BF16)
    return _block(x, hb, o_mla, kx, vx, wt, row(0.5 * b_gate[0]),
                  jnp.repeat(conv_w[0], SUBLANES, axis=0),
                  row(conv_b[0]), row(0.5 * conv_ln_g[0]), row(0.5 * conv_ln_b[0]),
                  half_bf(w_conv_o[0]), half_bf(w_mla_o[0]), half_bf(w_x_o[0]),
                  w_out[0].astype(BF16), row(final_norm_g))
```

```python
import functools

import jax
import jax.numpy as jnp
import numpy as np
from jax import lax
from jax.experimental import pallas as pl
from jax.experimental.pallas import tpu as pltpu

F32 = jnp.float32
BF16 = jnp.bfloat16

D_MODEL = 1024
N_MEM = 256
CONV_WIDTH = 512
CONV_KERNEL = 31
MLA_HEADS = 8
QK_NOPE = 64
QK_ROPE = 32
V_HEAD = 64
Q_LORA = 384
KV_LORA = 256
MLA_WIDTH = MLA_HEADS * V_HEAD
X_HEADS = 4
X_HEAD_DIM = 128
X_WIDTH = X_HEADS * X_HEAD_DIM
ROPE_THETA = 10000.0
EPS = 1e-6
IN_SPLITS = (CONV_WIDTH, CONV_WIDTH, CONV_WIDTH, Q_LORA, KV_LORA, QK_ROPE, MLA_WIDTH,
             X_WIDTH, X_WIDTH, 3 * D_MODEL)

LANES = 128
HEAD_PAD = LANES
BF16_ROWS = 16
V_AUG = V_HEAD + BF16_ROWS
ROPE_LO = QK_NOPE
ROPE_HALF = QK_ROPE // 2
ROPE_HI = QK_NOPE + QK_ROPE
MLA_SCALE = (QK_NOPE + QK_ROPE) ** -0.5
X_SCALE = X_HEAD_DIM ** -0.5
LOG2E = float(np.log2(np.e))
NEG = float(np.finfo(np.float32).min)

TS_QKV = 1024
TQ = 512
TK = 512
TS_BLK = 512
HALO = 32
CONV_ROWS = 32
CONV_SPLIT = 2
OUT_SPLIT = 2
SUBLANES = 8
SHIFT_ROWS = TS_BLK + HALO - SUBLANES
VMEM_LIMIT = 56 * 1024 * 1024


def _rms(x, g):
    return x * lax.rsqrt(jnp.mean(x * x, axis=-1, keepdims=True) + EPS) * g


def _twice_sigmoid_of_double(h):
    return jnp.tanh(h) + 1.0


def _silu_of_double(h):
    return h * _twice_sigmoid_of_double(h)


def _dot(a, b):
    return jnp.dot(a, b, preferred_element_type=F32)


def _dot_nt(a, b):
    return lax.dot_general(a, b, (((1,), (1,)), ((), ())), preferred_element_type=F32)


def _qkv_kernel(x_ref, pos_ref, ng_ref, wa_ref, qg_ref, wuq_ref, kvg_ref, wuk_ref, wuv_ref,
                invf_ref, mem_ref, mg_ref, wmem_ref, qt_ref, k_ref, vt_ref, hb_ref, kx_ref, vx_ref):
    @pl.when(pl.program_id(1) == 0)
    def _():
        mkv = _dot(_rms(mem_ref[...], mg_ref[...]).astype(BF16), wmem_ref[...])
        kx_ref[...] = mkv[:, :X_WIDTH].astype(BF16)
        vx_ref[...] = mkv[:, X_WIDTH:].astype(BF16)

    hb = _rms(x_ref[...], ng_ref[...]).astype(BF16)
    hb_ref[...] = hb
    za = _dot_nt(hb, wa_ref[...])
    cq = _rms(za[:, :Q_LORA], qg_ref[...]).astype(BF16)
    ckv = _rms(za[:, Q_LORA:Q_LORA + KV_LORA], kvg_ref[...]).astype(BF16)
    kr = za[:, Q_LORA + KV_LORA:]

    ang = invf_ref[...] * pos_ref[...].astype(F32)
    cos, sin = jnp.cos(ang), jnp.sin(ang)

    def rope_t(t):
        x1 = t[ROPE_LO:ROPE_LO + ROPE_HALF]
        x2 = t[ROPE_LO + ROPE_HALF:ROPE_HI]
        return jnp.concatenate(
            [t[:ROPE_LO], x1 * cos - x2 * sin, x1 * sin + x2 * cos, t[ROPE_HI:]], axis=0)

    kr_rot = rope_t(kr.T).T
    q = _dot(cq, wuq_ref[...])
    kn = _dot(ckv, wuk_ref[...])
    for h in range(MLA_HEADS):
        sl = slice(HEAD_PAD * h, HEAD_PAD * (h + 1))
        qt_ref[h] = (rope_t(q[:, sl].T) * (MLA_SCALE * LOG2E)).astype(BF16)
        k_ref[:, sl] = (kn[:, sl] + kr_rot).astype(BF16)
    vt = _dot(ckv, wuv_ref[...]).T.astype(BF16)
    ones_row = (lax.broadcasted_iota(jnp.int32, (V_AUG - V_HEAD, vt.shape[1]), 0) == 0).astype(BF16)
    for h in range(MLA_HEADS):
        vt_ref[h, 0:V_HEAD, :] = vt[V_HEAD * h:V_HEAD * (h + 1), :]
        vt_ref[h, V_HEAD:V_AUG, :] = ones_row


def _qkv(x, pos3, ng, wa, qg, wuq, kvg, wuk, wuv, invf, mem, mg, wmem):
    b, s, d = x.shape
    m = mem.shape[1]
    ts = TS_QKV
    const = lambda shape: pl.BlockSpec(shape, lambda i, j: (0,) * len(shape))
    hp = MLA_HEADS * HEAD_PAD
    return pl.pallas_call(
        _qkv_kernel,
        grid=(b, s // ts),
        in_specs=[pl.BlockSpec((None, ts, d), lambda i, j: (i, j, 0)),
                  pl.BlockSpec((None, 1, ts), lambda i, j: (i, 0, j)),
                  const((1, d)), const(wa.shape), const((1, Q_LORA)), const(wuq.shape),
                  const((1, KV_LORA)), const(wuk.shape), const(wuv.shape), const((ROPE_HALF, 1)),
                  pl.BlockSpec((None, m, d), lambda i, j: (i, 0, 0)),
                  const((1, d)), const(wmem.shape)],
        out_specs=[pl.BlockSpec((None, MLA_HEADS, HEAD_PAD, ts), lambda i, j: (i, 0, 0, j)),
                   pl.BlockSpec((None, ts, hp), lambda i, j: (i, j, 0)),
                   pl.BlockSpec((None, MLA_HEADS, V_AUG, ts), lambda i, j: (i, 0, 0, j)),
                   pl.BlockSpec((None, ts, d), lambda i, j: (i, j, 0)),
                   pl.BlockSpec((None, m, X_WIDTH), lambda i, j: (i, 0, 0)),
                   pl.BlockSpec((None, m, X_WIDTH), lambda i, j: (i, 0, 0))],
        out_shape=[jax.ShapeDtypeStruct((b, MLA_HEADS, HEAD_PAD, s), BF16),
                   jax.ShapeDtypeStruct((b, s, hp), BF16),
                   jax.ShapeDtypeStruct((b, MLA_HEADS, V_AUG, s), BF16),
                   jax.ShapeDtypeStruct((b, s, d), BF16),
                   jax.ShapeDtypeStruct((b, m, X_WIDTH), BF16),
                   jax.ShapeDtypeStruct((b, m, X_WIDTH), BF16)],
        compiler_params=pltpu.CompilerParams(dimension_semantics=("arbitrary", "arbitrary"),
                                             vmem_limit_bytes=VMEM_LIMIT),
        name="qkv",
    )(x, pos3, ng, wa, qg, wuq, kvg, wuk, wuv, invf, mem, mg, wmem)


ATTN_SLOTS = 4
ATTN_UNROLL = 4
ATTN_STRIP = 256
MAX_CHAINS = 2


def _attn_schedule(n_tiles):
    below = [(qi, j) for qi in range(n_tiles) for j in range(qi)]
    diag = [(qi, qi) for qi in range(n_tiles)]
    assert len(below) % ATTN_UNROLL == 0 and len(diag) % ATTN_UNROLL == 0
    table = np.array(below + diag + [(0, 0)], np.int32).T
    return table, len(below), len(diag)


def _attn_kernel(n_below, n_diag, tbl_ref, qt_ref, k_ref, vt_ref, o_ref,
                 s_scr, mblk_scr, m_scr, acc_scr):
    heads = (0, 1)
    units = [(hh, c) for c in range(TQ // ATTN_STRIP) for hh in heads]

    def key_rows(c, diag):
        return min(TK, (c + 1) * ATTN_STRIP) if diag else TK

    def scores(f, slot, hh, c, diag):
        q0 = pl.multiple_of(tbl_ref[0, f] * TQ + c * ATTN_STRIP, ATTN_STRIP)
        k0 = pl.multiple_of(tbl_ref[1, f] * TK, TK)
        nk = key_rows(c, diag)
        k = k_ref[pl.ds(k0, nk), HEAD_PAD * hh:HEAD_PAD * (hh + 1)]
        s = _dot(k, qt_ref[hh, :, pl.ds(q0, ATTN_STRIP)])
        if diag:
            key = lax.broadcasted_iota(jnp.int32, (nk, ATTN_STRIP), 0)
            qry = lax.broadcasted_iota(jnp.int32, (nk, ATTN_STRIP), 1) + c * ATTN_STRIP
            s = jnp.where(key <= qry, s, NEG)
        s_scr[slot, hh, c, 0:nk, :] = s
        tiles = s.reshape(MAX_CHAINS, nk // (MAX_CHAINS * SUBLANES), SUBLANES, ATTN_STRIP)
        mblk_scr[slot, hh, c] = jnp.max(jnp.max(jnp.max(tiles, axis=1), axis=0),
                                        axis=0, keepdims=True)

    def softmax(f, slot, hh, c, diag):
        qi = tbl_ref[0, f]
        nk = key_rows(c, diag)
        m_old = m_scr[qi, hh, c]
        m_new = jnp.maximum(m_old, mblk_scr[slot, hh, c])
        m_scr[qi, hh, c] = m_new
        alpha = jnp.exp2(m_old - m_new)
        p = jnp.exp2((s_scr[slot, hh, c, 0:nk, :] - m_new).astype(BF16))
        return alpha, p

    def values(f, hh, c, diag, alpha, p):
        qi = tbl_ref[0, f]
        k0 = pl.multiple_of(tbl_ref[1, f] * TK, TK)
        vt = vt_ref[hh, :, pl.ds(k0, key_rows(c, diag))]
        acc_scr[qi, hh, c] = alpha * acc_scr[qi, hh, c] + _dot(vt, p)

    def trip(f, first, unroll, diag, last_next_diag):
        for u in range(unroll):
            slot, nxt = (first + u) % ATTN_SLOTS, (first + u + 1) % ATTN_SLOTS
            next_diag = last_next_diag if u == unroll - 1 else diag
            for hh, c in units:
                alpha, p = softmax(f + u, slot, hh, c, diag)
                scores(f + u + 1, nxt, hh, c, next_diag)
                values(f + u, hh, c, diag, alpha, p)

    def trips(first, count, unroll, diag, then_diag):
        assert count % unroll == 0 and (count == unroll or unroll % ATTN_SLOTS == 0)
        n_trips = count // unroll
        if n_trips > 1:
            pl.loop(0, n_trips - 1)(lambda t: trip(first + t * unroll, first, unroll, diag, diag))
        if n_trips > 0:
            trip(first + (n_trips - 1) * unroll, first, unroll, diag, then_diag)

    m_scr[...] = jnp.full(m_scr.shape, NEG, F32)
    acc_scr[...] = jnp.zeros(acc_scr.shape, F32)
    for hh, c in units:
        scores(0, 0, hh, c, n_below == 0)
    trips(0, n_below, ATTN_UNROLL, diag=False, then_diag=True)
    trips(n_below, n_diag, ATTN_UNROLL, diag=True, then_diag=True)

    @pl.loop(0, n_diag)
    def _(qi):
        ot = jnp.concatenate(
            [jnp.concatenate([acc_scr[qi, hh, c, 0:V_HEAD] / acc_scr[qi, hh, c, V_HEAD:V_HEAD + 1]
                              for c in range(TQ // ATTN_STRIP)], axis=1) for hh in heads],
            axis=0)
        o_ref[pl.ds(pl.multiple_of(qi * TQ, TQ), TQ), :] = ot.T.astype(BF16)


def _mla_attn(qt, kp, vt):
    b, s, _ = kp.shape
    pairs = MLA_HEADS // 2
    n_tiles = s // TQ
    n_strips = TQ // ATTN_STRIP
    table, n_below, n_diag = _attn_schedule(n_tiles)
    grid_spec = pltpu.PrefetchScalarGridSpec(
        num_scalar_prefetch=1,
        grid=(b, pairs),
        in_specs=[pl.BlockSpec((None, 2, HEAD_PAD, s), lambda i, p, tbl: (i, p, 0, 0)),
                  pl.BlockSpec((None, s, 2 * HEAD_PAD), lambda i, p, tbl: (i, 0, p)),
                  pl.BlockSpec((None, 2, V_AUG, s), lambda i, p, tbl: (i, p, 0, 0))],
        out_specs=pl.BlockSpec((None, s, LANES), lambda i, p, tbl: (i, 0, p)),
        scratch_shapes=[pltpu.VMEM((ATTN_SLOTS, 2, n_strips, TK, ATTN_STRIP), F32),
                        pltpu.VMEM((ATTN_SLOTS, 2, n_strips, 1, ATTN_STRIP), F32),
                        pltpu.VMEM((n_tiles, 2, n_strips, 1, ATTN_STRIP), F32),
                        pltpu.VMEM((n_tiles, 2, n_strips, V_AUG, ATTN_STRIP), F32)])
    return pl.pallas_call(
        functools.partial(_attn_kernel, n_below, n_diag),
        grid_spec=grid_spec,
        out_shape=jax.ShapeDtypeStruct((b, s, MLA_WIDTH), BF16),
        compiler_params=pltpu.CompilerParams(dimension_semantics=("arbitrary", "arbitrary"),
                                             vmem_limit_bytes=VMEM_LIMIT),
        name="mla_attn",
    )(jnp.asarray(table), qt, kp, vt)


_EDGES = tuple(int(e) for e in np.cumsum((0,) + IN_SPLITS))
assert all(e % BF16_ROWS == 0 for e in _EDGES)
(_C_VAL, _C_GLU, _C_GATE, _Q_DOWN, _KV_DOWN, _K_ROPE, _M_GATE, _X_Q, _X_GATE, _G0, D_IN) = _EDGES


def _block_kernel(x_ref, hb_ref, o_mla_ref, kx_ref, vx_ref, wt_ref, bg_ref, cw_ref, cb_ref,
                  lng_ref, lnb_ref, wco_ref, wmo_ref, wxo_ref, wout_ref, fg_ref,
                  out_ref, ubuf, ush, conv_scr, part_scr, cgate_scr, g0_scr):
    ts = TS_BLK

    def proj(lo, width):
        return _dot_nt(hb_ref[...], wt_ref[lo:lo + width, :])

    def gate(i, lo, width):
        c0 = i * D_MODEL + lo
        return _twice_sigmoid_of_double(proj(_G0 + c0, width) + bg_ref[:, c0:c0 + width])

    @pl.when(pl.program_id(1) == 0)
    def _():
        ubuf[0:HALO, :] = jnp.zeros((HALO, CONV_WIDTH), F32)

    cw_ = CONV_WIDTH // CONV_SPLIT
    first = HALO - (CONV_KERNEL - 1)

    def glu(g):
        cs = slice(g * cw_, (g + 1) * cw_)
        ubuf[HALO:HALO + ts, cs] = (proj(_C_VAL + g * cw_, cw_)
                                    * _twice_sigmoid_of_double(proj(_C_GLU + g * cw_, cw_)))
        groups = SHIFT_ROWS // SUBLANES
        this_tile = ubuf[0:SHIFT_ROWS, cs].reshape(groups, SUBLANES, cw_)
        next_tile = ubuf[SUBLANES:SUBLANES + SHIFT_ROWS, cs].reshape(groups, SUBLANES, cw_)
        sub = lax.broadcasted_iota(jnp.int32, (1, SUBLANES, cw_), 1)
        for r in range(1, SUBLANES):
            picked = jnp.where(sub >= r, this_tile, next_tile)
            ush[r - 1, :, cs] = pltpu.roll(picked, SUBLANES - r, 1).reshape(SHIFT_ROWS, cw_)

    def conv_chunk(g, i):
        cs = slice(g * cw_, (g + 1) * cw_)
        r0 = i * CONV_ROWS
        groups = CONV_ROWS // SUBLANES
        acc = jnp.zeros((groups, SUBLANES, cw_), F32) + cb_ref[:, cs]
        for kk in range(CONV_KERNEL):
            r = (first + kk) % SUBLANES
            row = r0 + first + kk - r
            tap = ubuf[row:row + CONV_ROWS, cs] if r == 0 else ush[r - 1, row:row + CONV_ROWS, cs]
            w = cw_ref[kk * SUBLANES:(kk + 1) * SUBLANES, cs]
            acc = acc + tap.reshape(groups, SUBLANES, cw_) * w
        conv_scr[r0:r0 + CONV_ROWS, cs] = acc.reshape(CONV_ROWS, cw_)

    st = {}
    half = D_MODEL // 2

    def mla_gate():
        gate_m = _silu_of_double(proj(_M_GATE, MLA_WIDTH))
        st["a_mla"] = (o_mla_ref[...].astype(F32) * gate_m).astype(BF16)

    def mla_out():
        st["y_mla"] = _dot(st["a_mla"], wmo_ref[...])

    def mla_merge(lo):
        part_scr[:, lo:lo + half] = gate(1, lo, half) * st["y_mla"][:, lo:lo + half]

    def x_query():
        st["xq"] = proj(_X_Q, X_WIDTH).astype(BF16)
        st["ox"] = []

    def x_heads(h0):
        for h in (h0, h0 + 1):
            hs = slice(X_HEAD_DIM * h, X_HEAD_DIM * (h + 1))
            s = _dot_nt(st["xq"][:, hs], kx_ref[:, hs])
            p = jnp.exp2((s - jnp.max(s, axis=-1, keepdims=True)) * (X_SCALE * LOG2E))
            inv = 1.0 / jnp.sum(p, axis=-1, keepdims=True)
            st["ox"].append(_dot(p.astype(BF16), vx_ref[:, hs]) * inv)

    def x_gate():
        st["xg"] = _silu_of_double(proj(_X_GATE, X_WIDTH))

    def x_out():
        a_x = jnp.concatenate(st["ox"], axis=-1) * st["xg"]
        st["y_x"] = _dot(a_x.astype(BF16), wxo_ref[...])

    def x_merge(lo):
        part_scr[:, lo:lo + half] += gate(2, lo, half) * st["y_x"][:, lo:lo + half]

    def conv_gate():
        cgate_scr[...] = _silu_of_double(proj(_C_GATE, CONV_WIDTH))

    def conv_merge_gate(lo):
        g0_scr[:, lo:lo + half] = gate(0, lo, half)

    side = [mla_gate, mla_out, functools.partial(mla_merge, 0), functools.partial(mla_merge, half),
            x_query, functools.partial(x_heads, 0), functools.partial(x_heads, 2), x_gate, x_out,
            functools.partial(x_merge, 0), functools.partial(x_merge, half), conv_gate,
            functools.partial(conv_merge_gate, 0), functools.partial(conv_merge_gate, half)]
    side = [functools.partial(glu, g) for g in range(1, CONV_SPLIT)] + side
    chunks = [(g, i) for g in range(CONV_SPLIT) for i in range(ts // CONV_ROWS)]
    glu(0)
    for n, (g, i) in enumerate(chunks):
        conv_chunk(g, i)
        for j, piece in enumerate(side):
            if j * len(chunks) // len(side) == n:
                piece()
    ubuf[0:HALO, :] = ubuf[ts:ts + HALO, :]

    conv = conv_scr[...]
    mu = jnp.mean(conv, axis=-1, keepdims=True)
    cen = conv - mu
    var = jnp.mean(cen * cen, axis=-1, keepdims=True)
    half_ln = cen * lax.rsqrt(var + EPS) * lng_ref[...] + lnb_ref[...]
    a_conv = _silu_of_double(half_ln) * cgate_scr[...]
    y_conv = _dot(a_conv.astype(BF16), wco_ref[...])
    for r0 in range(0, ts, ts // OUT_SPLIT):
        rows = slice(r0, r0 + ts // OUT_SPLIT)
        merged = g0_scr[rows, :] * y_conv[rows, :] + part_scr[rows, :]
        y = x_ref[rows, :] + _dot(merged.astype(BF16), wout_ref[...])
        out_ref[rows, :] = _rms(y, fg_ref[...])


def _block(x, hb, o_mla, kx, vx, wt, bg, cw, cb, lng, lnb, wco, wmo, wxo, wout, fg):
    b, s, d = x.shape
    ts = TS_BLK
    const = lambda shape: pl.BlockSpec(shape, lambda i, j: (0,) * len(shape),
                                       pipeline_mode=pl.Buffered(1))
    return pl.pallas_call(
        _block_kernel,
        grid=(b, s // ts),
        in_specs=[pl.BlockSpec((None, ts, d), lambda i, j: (i, j, 0)),
                  pl.BlockSpec((None, ts, d), lambda i, j: (i, j, 0)),
                  pl.BlockSpec((None, ts, MLA_WIDTH), lambda i, j: (i, j, 0)),
                  pl.BlockSpec((None, N_MEM, X_WIDTH), lambda i, j: (i, 0, 0)),
                  pl.BlockSpec((None, N_MEM, X_WIDTH), lambda i, j: (i, 0, 0)),
                  const(wt.shape), const(bg.shape),
                  const(cw.shape),
                  const((1, CONV_WIDTH)), const((1, CONV_WIDTH)), const((1, CONV_WIDTH)),
                  const(wco.shape), const(wmo.shape), const(wxo.shape), const(wout.shape),
                  const((1, d))],
        out_specs=pl.BlockSpec((None, ts, d), lambda i, j: (i, j, 0)),
        out_shape=jax.ShapeDtypeStruct((b, s, d), F32),
        scratch_shapes=[pltpu.VMEM((HALO + ts, CONV_WIDTH), F32),
                        pltpu.VMEM((SUBLANES - 1, SHIFT_ROWS, CONV_WIDTH), F32),
                        pltpu.VMEM((ts, CONV_WIDTH), F32),
                        pltpu.VMEM((ts, d), F32),
                        pltpu.VMEM((ts, CONV_WIDTH), F32),
                        pltpu.VMEM((ts, d), F32)],
        compiler_params=pltpu.CompilerParams(dimension_semantics=("arbitrary", "arbitrary"),
                                             vmem_limit_bytes=VMEM_LIMIT),
        name="block",
    )(x, hb, o_mla, kx, vx, wt, bg, cw, cb, lng, lnb, wco, wmo, wxo, wout, fg)


def _inv_freq_column():
    inv = ROPE_THETA ** (-jnp.arange(0, QK_ROPE, 2, dtype=F32) / QK_ROPE)
    return inv.reshape(ROPE_HALF, 1)


def kernel(x, mem, positions, norm_g, w_in, b_gate, conv_w, conv_b, conv_ln_g, conv_ln_b,
           w_conv_o, q_norm_g, w_uq, kv_norm_g, w_ukv, w_mla_o, mem_norm_g, w_mem_kv, w_x_o,
           w_out, final_norm_g):
    assert norm_g.shape[0] == 1, "single-layer trunk"
    b, s, d = x.shape
    row = lambda v: v.reshape(1, -1)
    halved = np.ones((D_IN, 1), np.float32)
    for lo, hi in ((_C_VAL, _Q_DOWN), (_M_GATE, _X_Q), (_X_GATE, D_IN)):
        halved[lo:hi] = 0.5
    wt = (w_in[0].T * halved).astype(BF16)
    wa = jnp.concatenate([wt[_Q_DOWN:_K_ROPE],
                          jnp.pad(wt[_K_ROPE:_M_GATE], ((ROPE_LO, LANES - ROPE_HI), (0, 0)))], axis=0)
    wuq = jnp.pad(w_uq[0].reshape(Q_LORA, MLA_HEADS, QK_NOPE + QK_ROPE),
                  ((0, 0), (0, 0), (0, HEAD_PAD - QK_NOPE - QK_ROPE)))
    wuq = wuq.reshape(Q_LORA, MLA_HEADS * HEAD_PAD).astype(BF16)
    wukv = w_ukv[0].reshape(KV_LORA, MLA_HEADS, QK_NOPE + V_HEAD)
    wuk = jnp.pad(wukv[:, :, :QK_NOPE], ((0, 0), (0, 0), (0, HEAD_PAD - QK_NOPE)))
    wuk = wuk.reshape(KV_LORA, MLA_HEADS * HEAD_PAD).astype(BF16)
    wuv = wukv[:, :, QK_NOPE:].reshape(KV_LORA, MLA_WIDTH).astype(BF16)

    qt, kp, vt, hb, kx, vx = _qkv(x, positions.reshape(b, 1, s), row(norm_g[0]), wa,
                                  row(q_norm_g[0]), wuq, row(kv_norm_g[0]), wuk, wuv,
                                  _inv_freq_column(), mem, row(mem_norm_g[0]),
                                  w_mem_kv[0].astype(BF16))
    o_mla = _mla_attn(qt, kp, vt)
    half_bf = lambda w: (0.5 * w).astype(BF16)
    return _block(x, hb, o_mla, kx, vx, wt, row(0.5 * b_gate[0]),
                  jnp.repeat(conv_w[0], SUBLANES, axis=0),
                  row(conv_b[0]), row(0.5 * conv_ln_g[0]), row(0.5 * conv_ln_b[0]),
                  half_bf(w_conv_o[0]), half_bf(w_mla_o[0]), half_bf(w_x_o[0]),
                  w_out[0].astype(BF16), row(final_norm_g))
```

```python
import functools

import jax
import jax.numpy as jnp
import numpy as np
from jax import lax
from jax.experimental import pallas as pl
from jax.experimental.pallas import tpu as pltpu

F32 = jnp.float32
BF16 = jnp.bfloat16

D_MODEL = 1024
N_MEM = 256
CONV_WIDTH = 512
CONV_KERNEL = 31
MLA_HEADS = 8
QK_NOPE = 64
QK_ROPE = 32
V_HEAD = 64
Q_LORA = 384
KV_LORA = 256
MLA_WIDTH = MLA_HEADS * V_HEAD
X_HEADS = 4
X_HEAD_DIM = 128
X_WIDTH = X_HEADS * X_HEAD_DIM
ROPE_THETA = 10000.0
EPS = 1e-6
IN_SPLITS = (CONV_WIDTH, CONV_WIDTH, CONV_WIDTH, Q_LORA, KV_LORA, QK_ROPE, MLA_WIDTH,
             X_WIDTH, X_WIDTH, 3 * D_MODEL)

LANES = 128
HEAD_PAD = LANES
BF16_ROWS = 16
V_AUG = V_HEAD + BF16_ROWS
ROPE_LO = QK_NOPE
ROPE_HALF = QK_ROPE // 2
ROPE_HI = QK_NOPE + QK_ROPE
MLA_SCALE = (QK_NOPE + QK_ROPE) ** -0.5
X_SCALE = X_HEAD_DIM ** -0.5
LOG2E = float(np.log2(np.e))
NEG = float(np.finfo(np.float32).min)

TS_QKV = 1024
X_RING = 3
TQ = 512
TK = 512
TS_BLK = 512
HALO = 32
CONV_ROWS = 32
CONV_SPLIT = 2
OUT_SPLIT = 2
SUBLANES = 8
SHIFT_ROWS = TS_BLK + HALO - SUBLANES
VMEM_LIMIT = 56 * 1024 * 1024


def _rms(x, g):
    return x * lax.rsqrt(jnp.mean(x * x, axis=-1, keepdims=True) + EPS) * g


def _twice_sigmoid_of_double(h):
    return jnp.tanh(h) + 1.0


def _silu_of_double(h):
    return h * _twice_sigmoid_of_double(h)


def _dot(a, b):
    return jnp.dot(a, b, preferred_element_type=F32)


def _dot_nt(a, b):
    return lax.dot_general(a, b, (((1,), (1,)), ((), ())), preferred_element_type=F32)


def _qkv_kernel(x_hbm, pos_ref, ng_ref, wa_ref, qg_ref, wuq_ref, kvg_ref, wuk_ref, wuv_ref,
                invf_ref, mem_ref, mg_ref, wmem_ref, qt_ref, k_ref, vt_ref, hb_ref, kx_ref, vx_ref,
                xbuf, xsem):
    n_seq = pl.num_programs(1)
    n_steps = pl.num_programs(0) * n_seq
    step = pl.program_id(0) * n_seq + pl.program_id(1)

    def x_copy(t):
        rows = pl.ds(pl.multiple_of((t % n_seq) * TS_QKV, TS_QKV), TS_QKV)
        slot = t % X_RING
        return pltpu.make_async_copy(x_hbm.at[t // n_seq, rows, :], xbuf.at[slot], xsem.at[slot])

    @pl.when(step == 0)
    def _():
        for t in range(X_RING - 1):
            x_copy(t).start()

    @pl.when(step + (X_RING - 1) < n_steps)
    def _():
        x_copy(step + (X_RING - 1)).start()

    @pl.when(pl.program_id(1) == 0)
    def _():
        mkv = _dot(_rms(mem_ref[...], mg_ref[...]).astype(BF16), wmem_ref[...])
        kx_ref[...] = mkv[:, :X_WIDTH].astype(BF16)
        vx_ref[...] = mkv[:, X_WIDTH:].astype(BF16)

    x_copy(step).wait()
    hb = _rms(xbuf[step % X_RING], ng_ref[...]).astype(BF16)
    hb_ref[...] = hb
    za = _dot_nt(hb, wa_ref[...])
    cq = _rms(za[:, :Q_LORA], qg_ref[...]).astype(BF16)
    ckv = _rms(za[:, Q_LORA:Q_LORA + KV_LORA], kvg_ref[...]).astype(BF16)
    kr = za[:, Q_LORA + KV_LORA:]

    ang = invf_ref[...] * pos_ref[...].astype(F32)
    cos, sin = jnp.cos(ang), jnp.sin(ang)

    def rope_t(t):
        x1 = t[ROPE_LO:ROPE_LO + ROPE_HALF]
        x2 = t[ROPE_LO + ROPE_HALF:ROPE_HI]
        return jnp.concatenate(
            [t[:ROPE_LO], x1 * cos - x2 * sin, x1 * sin + x2 * cos, t[ROPE_HI:]], axis=0)

    kr_rot = rope_t(kr.T).T
    q = _dot(cq, wuq_ref[...])
    kn = _dot(ckv, wuk_ref[...])
    for h in range(MLA_HEADS):
        sl = slice(HEAD_PAD * h, HEAD_PAD * (h + 1))
        qt_ref[h] = (rope_t(q[:, sl].T) * (MLA_SCALE * LOG2E)).astype(BF16)
        k_ref[:, sl] = (kn[:, sl] + kr_rot).astype(BF16)
    vt = _dot(ckv, wuv_ref[...]).T.astype(BF16)
    ones_row = (lax.broadcasted_iota(jnp.int32, (V_AUG - V_HEAD, vt.shape[1]), 0) == 0).astype(BF16)
    for h in range(MLA_HEADS):
        vt_ref[h, 0:V_HEAD, :] = vt[V_HEAD * h:V_HEAD * (h + 1), :]
        vt_ref[h, V_HEAD:V_AUG, :] = ones_row


def _qkv(x, pos3, ng, wa, qg, wuq, kvg, wuk, wuv, invf, mem, mg, wmem):
    b, s, d = x.shape
    m = mem.shape[1]
    ts = TS_QKV
    assert b * (s // ts) >= X_RING - 1, "the ring prologue starts X_RING - 1 copies"
    const = lambda shape: pl.BlockSpec(shape, lambda i, j: (0,) * len(shape))
    hp = MLA_HEADS * HEAD_PAD
    return pl.pallas_call(
        _qkv_kernel,
        grid=(b, s // ts),
        in_specs=[pl.BlockSpec(memory_space=pl.ANY),
                  pl.BlockSpec((None, 1, ts), lambda i, j: (i, 0, j)),
                  const((1, d)), const(wa.shape), const((1, Q_LORA)), const(wuq.shape),
                  const((1, KV_LORA)), const(wuk.shape), const(wuv.shape), const((ROPE_HALF, 1)),
                  pl.BlockSpec((None, m, d), lambda i, j: (i, 0, 0)),
                  const((1, d)), const(wmem.shape)],
        out_specs=[pl.BlockSpec((None, MLA_HEADS, HEAD_PAD, ts), lambda i, j: (i, 0, 0, j)),
                   pl.BlockSpec((None, ts, hp), lambda i, j: (i, j, 0)),
                   pl.BlockSpec((None, MLA_HEADS, V_AUG, ts), lambda i, j: (i, 0, 0, j)),
                   pl.BlockSpec((None, ts, d), lambda i, j: (i, j, 0)),
                   pl.BlockSpec((None, m, X_WIDTH), lambda i, j: (i, 0, 0)),
                   pl.BlockSpec((None, m, X_WIDTH), lambda i, j: (i, 0, 0))],
        out_shape=[jax.ShapeDtypeStruct((b, MLA_HEADS, HEAD_PAD, s), BF16),
                   jax.ShapeDtypeStruct((b, s, hp), BF16),
                   jax.ShapeDtypeStruct((b, MLA_HEADS, V_AUG, s), BF16),
                   jax.ShapeDtypeStruct((b, s, d), BF16),
                   jax.ShapeDtypeStruct((b, m, X_WIDTH), BF16),
                   jax.ShapeDtypeStruct((b, m, X_WIDTH), BF16)],
        scratch_shapes=[pltpu.VMEM((X_RING, ts, d), F32),
                        pltpu.SemaphoreType.DMA((X_RING,))],
        compiler_params=pltpu.CompilerParams(dimension_semantics=("arbitrary", "arbitrary"),
                                             vmem_limit_bytes=VMEM_LIMIT),
        name="qkv",
    )(x, pos3, ng, wa, qg, wuq, kvg, wuk, wuv, invf, mem, mg, wmem)


ATTN_SLOTS = 4
ATTN_UNROLL = 4
ATTN_STRIP = 256
MAX_CHAINS = 2


def _attn_schedule(n_tiles):
    below = [(qi, j) for qi in range(n_tiles) for j in range(qi)]
    diag = [(qi, qi) for qi in range(n_tiles)]
    assert len(below) % ATTN_UNROLL == 0 and len(diag) % ATTN_UNROLL == 0
    table = np.array(below + diag + [(0, 0)], np.int32).T
    return table, len(below), len(diag)


def _attn_kernel(n_below, n_diag, tbl_ref, qt_ref, k_ref, vt_ref, o_ref,
                 s_scr, mblk_scr, m_scr, acc_scr):
    heads = (0, 1)
    units = [(hh, c) for c in range(TQ // ATTN_STRIP) for hh in heads]

    def key_rows(c, diag):
        return min(TK, (c + 1) * ATTN_STRIP) if diag else TK

    def scores(f, slot, hh, c, diag):
        q0 = pl.multiple_of(tbl_ref[0, f] * TQ + c * ATTN_STRIP, ATTN_STRIP)
        k0 = pl.multiple_of(tbl_ref[1, f] * TK, TK)
        nk = key_rows(c, diag)
        k = k_ref[pl.ds(k0, nk), HEAD_PAD * hh:HEAD_PAD * (hh + 1)]
        s = _dot(k, qt_ref[hh, :, pl.ds(q0, ATTN_STRIP)])
        if diag:
            key = lax.broadcasted_iota(jnp.int32, (nk, ATTN_STRIP), 0)
            qry = lax.broadcasted_iota(jnp.int32, (nk, ATTN_STRIP), 1) + c * ATTN_STRIP
            s = jnp.where(key <= qry, s, NEG)
        s_scr[slot, hh, c, 0:nk, :] = s
        tiles = s.reshape(MAX_CHAINS, nk // (MAX_CHAINS * SUBLANES), SUBLANES, ATTN_STRIP)
        mblk_scr[slot, hh, c] = jnp.max(jnp.max(jnp.max(tiles, axis=1), axis=0),
                                        axis=0, keepdims=True)

    def softmax(f, slot, hh, c, diag):
        qi = tbl_ref[0, f]
        nk = key_rows(c, diag)
        m_old = m_scr[qi, hh, c]
        m_new = jnp.maximum(m_old, mblk_scr[slot, hh, c])
        m_scr[qi, hh, c] = m_new
        alpha = jnp.exp2(m_old - m_new)
        p = jnp.exp2((s_scr[slot, hh, c, 0:nk, :] - m_new).astype(BF16))
        return alpha, p

    def values(f, hh, c, diag, alpha, p):
        qi = tbl_ref[0, f]
        k0 = pl.multiple_of(tbl_ref[1, f] * TK, TK)
        vt = vt_ref[hh, :, pl.ds(k0, key_rows(c, diag))]
        acc_scr[qi, hh, c] = alpha * acc_scr[qi, hh, c] + _dot(vt, p)

    def trip(f, first, unroll, diag, last_next_diag):
        for u in range(unroll):
            slot, nxt = (first + u) % ATTN_SLOTS, (first + u + 1) % ATTN_SLOTS
            next_diag = last_next_diag if u == unroll - 1 else diag
            for hh, c in units:
                alpha, p = softmax(f + u, slot, hh, c, diag)
                scores(f + u + 1, nxt, hh, c, next_diag)
                values(f + u, hh, c, diag, alpha, p)

    def trips(first, count, unroll, diag, then_diag):
        assert count % unroll == 0 and (count == unroll or unroll % ATTN_SLOTS == 0)
        n_trips = count // unroll
        if n_trips > 1:
            pl.loop(0, n_trips - 1)(lambda t: trip(first + t * unroll, first, unroll, diag, diag))
        if n_trips > 0:
            trip(first + (n_trips - 1) * unroll, first, unroll, diag, then_diag)

    m_scr[...] = jnp.full(m_scr.shape, NEG, F32)
    acc_scr[...] = jnp.zeros(acc_scr.shape, F32)
    for hh, c in units:
        scores(0, 0, hh, c, n_below == 0)
    trips(0, n_below, ATTN_UNROLL, diag=False, then_diag=True)
    trips(n_below, n_diag, ATTN_UNROLL, diag=True, then_diag=True)

    @pl.loop(0, n_diag)
    def _(qi):
        ot = jnp.concatenate(
            [jnp.concatenate([acc_scr[qi, hh, c, 0:V_HEAD] / acc_scr[qi, hh, c, V_HEAD:V_HEAD + 1]
                              for c in range(TQ // ATTN_STRIP)], axis=1) for hh in heads],
            axis=0)
        o_ref[pl.ds(pl.multiple_of(qi * TQ, TQ), TQ), :] = ot.T.astype(BF16)


def _mla_attn(qt, kp, vt):
    b, s, _ = kp.shape
    pairs = MLA_HEADS // 2
    n_tiles = s // TQ
    n_strips = TQ // ATTN_STRIP
    table, n_below, n_diag = _attn_schedule(n_tiles)
    grid_spec = pltpu.PrefetchScalarGridSpec(
        num_scalar_prefetch=1,
        grid=(b, pairs),
        in_specs=[pl.BlockSpec((None, 2, HEAD_PAD, s), lambda i, p, tbl: (i, p, 0, 0)),
                  pl.BlockSpec((None, s, 2 * HEAD_PAD), lambda i, p, tbl: (i, 0, p)),
                  pl.BlockSpec((None, 2, V_AUG, s), lambda i, p, tbl: (i, p, 0, 0))],
        out_specs=pl.BlockSpec((None, s, LANES), lambda i, p, tbl: (i, 0, p)),
        scratch_shapes=[pltpu.VMEM((ATTN_SLOTS, 2, n_strips, TK, ATTN_STRIP), F32),
                        pltpu.VMEM((ATTN_SLOTS, 2, n_strips, 1, ATTN_STRIP), F32),
                        pltpu.VMEM((n_tiles, 2, n_strips, 1, ATTN_STRIP), F32),
                        pltpu.VMEM((n_tiles, 2, n_strips, V_AUG, ATTN_STRIP), F32)])
    return pl.pallas_call(
        functools.partial(_attn_kernel, n_below, n_diag),
        grid_spec=grid_spec,
        out_shape=jax.ShapeDtypeStruct((b, s, MLA_WIDTH), BF16),
        compiler_params=pltpu.CompilerParams(dimension_semantics=("arbitrary", "arbitrary"),
                                             vmem_limit_bytes=VMEM_LIMIT),
        name="mla_attn",
    )(jnp.asarray(table), qt, kp, vt)


_EDGES = tuple(int(e) for e in np.cumsum((0,) + IN_SPLITS))
assert all(e % BF16_ROWS == 0 for e in _EDGES)
(_C_VAL, _C_GLU, _C_GATE, _Q_DOWN, _KV_DOWN, _K_ROPE, _M_GATE, _X_Q, _X_GATE, _G0, D_IN) = _EDGES


def _block_kernel(x_ref, hb_ref, o_mla_ref, kx_ref, vx_ref, wt_ref, bg_ref, cw_ref, cb_ref,
                  lng_ref, lnb_ref, wco_ref, wmo_ref, wxo_ref, wout_ref, fg_ref,
                  out_ref, ubuf, ush, conv_scr, part_scr, cgate_scr, g0_scr):
    ts = TS_BLK

    def proj(lo, width):
        return _dot_nt(hb_ref[...], wt_ref[lo:lo + width, :])

    def gate(i, lo, width):
        c0 = i * D_MODEL + lo
        return _twice_sigmoid_of_double(proj(_G0 + c0, width) + bg_ref[:, c0:c0 + width])

    @pl.when(pl.program_id(1) == 0)
    def _():
        ubuf[0:HALO, :] = jnp.zeros((HALO, CONV_WIDTH), F32)

    cw_ = CONV_WIDTH // CONV_SPLIT
    first = HALO - (CONV_KERNEL - 1)

    def glu(g):
        cs = slice(g * cw_, (g + 1) * cw_)
        ubuf[HALO:HALO + ts, cs] = (proj(_C_VAL + g * cw_, cw_)
                                    * _twice_sigmoid_of_double(proj(_C_GLU + g * cw_, cw_)))
        groups = SHIFT_ROWS // SUBLANES
        this_tile = ubuf[0:SHIFT_ROWS, cs].reshape(groups, SUBLANES, cw_)
        next_tile = ubuf[SUBLANES:SUBLANES + SHIFT_ROWS, cs].reshape(groups, SUBLANES, cw_)
        sub = lax.broadcasted_iota(jnp.int32, (1, SUBLANES, cw_), 1)
        for r in range(1, SUBLANES):
            picked = jnp.where(sub >= r, this_tile, next_tile)
            ush[r - 1, :, cs] = pltpu.roll(picked, SUBLANES - r, 1).reshape(SHIFT_ROWS, cw_)

    def conv_chunk(g, i):
        cs = slice(g * cw_, (g + 1) * cw_)
        r0 = i * CONV_ROWS
        groups = CONV_ROWS // SUBLANES
        acc = jnp.zeros((groups, SUBLANES, cw_), F32) + cb_ref[:, cs]
        for kk in range(CONV_KERNEL):
            r = (first + kk) % SUBLANES
            row = r0 + first + kk - r
            tap = ubuf[row:row + CONV_ROWS, cs] if r == 0 else ush[r - 1, row:row + CONV_ROWS, cs]
            w = cw_ref[kk * SUBLANES:(kk + 1) * SUBLANES, cs]
            acc = acc + tap.reshape(groups, SUBLANES, cw_) * w
        conv_scr[r0:r0 + CONV_ROWS, cs] = acc.reshape(CONV_ROWS, cw_)

    st = {}
    half = D_MODEL // 2

    def mla_gate():
        gate_m = _silu_of_double(proj(_M_GATE, MLA_WIDTH))
        st["a_mla"] = (o_mla_ref[...].astype(F32) * gate_m).astype(BF16)

    def mla_out():
        st["y_mla"] = _dot(st["a_mla"], wmo_ref[...])

    def mla_merge(lo):
        part_scr[:, lo:lo + half] = gate(1, lo, half) * st["y_mla"][:, lo:lo + half]

    def x_query():
        st["xq"] = proj(_X_Q, X_WIDTH).astype(BF16)
        st["ox"] = []

    def x_heads(h0):
        for h in (h0, h0 + 1):
            hs = slice(X_HEAD_DIM * h, X_HEAD_DIM * (h + 1))
            s = _dot_nt(st["xq"][:, hs], kx_ref[:, hs])
            p = jnp.exp2((s - jnp.max(s, axis=-1, keepdims=True)) * (X_SCALE * LOG2E))
            inv = 1.0 / jnp.sum(p, axis=-1, keepdims=True)
            st["ox"].append(_dot(p.astype(BF16), vx_ref[:, hs]) * inv)

    def x_gate():
        st["xg"] = _silu_of_double(proj(_X_GATE, X_WIDTH))

    def x_out():
        a_x = jnp.concatenate(st["ox"], axis=-1) * st["xg"]
        st["y_x"] = _dot(a_x.astype(BF16), wxo_ref[...])

    def x_merge(lo):
        part_scr[:, lo:lo + half] += gate(2, lo, half) * st["y_x"][:, lo:lo + half]

    def conv_gate():
        cgate_scr[...] = _silu_of_double(proj(_C_GATE, CONV_WIDTH))

    def conv_merge_gate(lo):
        g0_scr[:, lo:lo + half] = gate(0, lo, half)

    side = [mla_gate, mla_out, functools.partial(mla_merge, 0), functools.partial(mla_merge, half),
            x_query, functools.partial(x_heads, 0), functools.partial(x_heads, 2), x_gate, x_out,
            functools.partial(x_merge, 0), functools.partial(x_merge, half), conv_gate,
            functools.partial(conv_merge_gate, 0), functools.partial(conv_merge_gate, half)]
    side = [functools.partial(glu, g) for g in range(1, CONV_SPLIT)] + side
    chunks = [(g, i) for g in range(CONV_SPLIT) for i in range(ts // CONV_ROWS)]
    glu(0)
    for n, (g, i) in enumerate(chunks):
        conv_chunk(g, i)
        for j, piece in enumerate(side):
            if j * len(chunks) // len(side) == n:
                piece()
    ubuf[0:HALO, :] = ubuf[ts:ts + HALO, :]

    conv = conv_scr[...]
    mu = jnp.mean(conv, axis=-1, keepdims=True)
    cen = conv - mu
    var = jnp.mean(cen * cen, axis=-1, keepdims=True)
    half_ln = cen * lax.rsqrt(var + EPS) * lng_ref[...] + lnb_ref[...]
    a_conv = _silu_of_double(half_ln) * cgate_scr[...]
    y_conv = _dot(a_conv.astype(BF16), wco_ref[...])
    for r0 in range(0, ts, ts // OUT_SPLIT):
        rows = slice(r0, r0 + ts // OUT_SPLIT)
        merged = g0_scr[rows, :] * y_conv[rows, :] + part_scr[rows, :]
        y = x_ref[rows, :] + _dot(merged.astype(BF16), wout_ref[...])
        out_ref[rows, :] = _rms(y, fg_ref[...])


def _block(x, hb, o_mla, kx, vx, wt, bg, cw, cb, lng, lnb, wco, wmo, wxo, wout, fg):
    b, s, d = x.shape
    ts = TS_BLK
    const = lambda shape: pl.BlockSpec(shape, lambda i, j: (0,) * len(shape),
                                       pipeline_mode=pl.Buffered(1))
    return pl.pallas_call(
        _block_kernel,
        grid=(b, s // ts),
        in_specs=[pl.BlockSpec((None, ts, d), lambda i, j: (i, j, 0)),
                  pl.BlockSpec((None, ts, d), lambda i, j: (i, j, 0)),
                  pl.BlockSpec((None, ts, MLA_WIDTH), lambda i, j: (i, j, 0)),
                  pl.BlockSpec((None, N_MEM, X_WIDTH), lambda i, j: (i, 0, 0)),
                  pl.BlockSpec((None, N_MEM, X_WIDTH), lambda i, j: (i, 0, 0)),
                  const(wt.shape), const(bg.shape),
                  const(cw.shape),
                  const((1, CONV_WIDTH)), const((1, CONV_WIDTH)), const((1, CONV_WIDTH)),
                  const(wco.shape), const(wmo.shape), const(wxo.shape), const(wout.shape),
                  const((1, d))],
        out_specs=pl.BlockSpec((None, ts, d), lambda i, j: (i, j, 0)),
        out_shape=jax.ShapeDtypeStruct((b, s, d), F32),
        scratch_shapes=[pltpu.VMEM((HALO + ts, CONV_WIDTH), F32),
                        pltpu.VMEM((SUBLANES - 1, SHIFT_ROWS, CONV_WIDTH), F32),
                        pltpu.VMEM((ts, CONV_WIDTH), F32),
                        pltpu.VMEM((ts, d), F32),
                        pltpu.VMEM((ts, CONV_WIDTH), F32),
                        pltpu.VMEM((ts, d), F32)],
        compiler_params=pltpu.CompilerParams(dimension_semantics=("arbitrary", "arbitrary"),
                                             vmem_limit_bytes=VMEM_LIMIT),
        name="block",
    )(x, hb, o_mla, kx, vx, wt, bg, cw, cb, lng, lnb, wco, wmo, wxo, wout, fg)


def _inv_freq_column():
    inv = ROPE_THETA ** (-jnp.arange(0, QK_ROPE, 2, dtype=F32) / QK_ROPE)
    return inv.reshape(ROPE_HALF, 1)


def kernel(x, mem, positions, norm_g, w_in, b_gate, conv_w, conv_b, conv_ln_g, conv_ln_b,
           w_conv_o, q_norm_g, w_uq, kv_norm_g, w_ukv, w_mla_o, mem_norm_g, w_mem_kv, w_x_o,
           w_out, final_norm_g):
    assert norm_g.shape[0] == 1, "single-layer trunk"
    b, s, d = x.shape
    row = lambda v: v.reshape(1, -1)
    halved = np.ones((D_IN, 1), np.float32)
    for lo, hi in ((_C_VAL, _Q_DOWN), (_M_GATE, _X_Q), (_X_GATE, D_IN)):
        halved[lo:hi] = 0.5
    wt = (w_in[0].T * halved).astype(BF16)
    wa = jnp.concatenate([wt[_Q_DOWN:_K_ROPE],
                          jnp.pad(wt[_K_ROPE:_M_GATE], ((ROPE_LO, LANES - ROPE_HI), (0, 0)))], axis=0)
    wuq = jnp.pad(w_uq[0].reshape(Q_LORA, MLA_HEADS, QK_NOPE + QK_ROPE),
                  ((0, 0), (0, 0), (0, HEAD_PAD - QK_NOPE - QK_ROPE)))
    wuq = wuq.reshape(Q_LORA, MLA_HEADS * HEAD_PAD).astype(BF16)
    wukv = w_ukv[0].reshape(KV_LORA, MLA_HEADS, QK_NOPE + V_HEAD)
    wuk = jnp.pad(wukv[:, :, :QK_NOPE], ((0, 0), (0, 0), (0, HEAD_PAD - QK_NOPE)))
    wuk = wuk.reshape(KV_LORA, MLA_HEADS * HEAD_PAD).astype(BF16)
    wuv = wukv[:, :, QK_NOPE:].reshape(KV_LORA, MLA_WIDTH).astype(BF16)

    qt, kp, vt, hb, kx, vx = _qkv(x, positions.reshape(b, 1, s), row(norm_g[0]), wa,
                                  row(q_norm_g[0]), wuq, row(kv_norm_g[0]), wuk, wuv,
                                  _inv_freq_column(), mem, row(mem_norm_g[0]),
                                  w_mem_kv[0].astype(BF16))
    o_mla = _mla_attn(qt, kp, vt)
    half_bf = lambda w: (0.5 * w).astype(BF16)
    return _block(x, hb, o_mla, kx, vx, wt, row(0.5 * b_gate[0]),
                  jnp.repeat(conv_w[0], SUBLANES, axis=0),
                  row(conv_b[0]), row(0.5 * conv_ln_g[0]), row(0.5 * conv_ln_b[0]),
                  half_bf(w_conv_o[0]), half_bf(w_mla_o[0]), half_bf(w_x_o[0]),
                  w_out[0].astype(BF16), row(final_norm_g))
```

```python
import functools

import jax
import jax.numpy as jnp
import numpy as np
from jax import lax
from jax.experimental import pallas as pl
from jax.experimental.pallas import tpu as pltpu

F32 = jnp.float32
BF16 = jnp.bfloat16

D_MODEL = 1024
N_MEM = 256
CONV_WIDTH = 512
CONV_KERNEL = 31
MLA_HEADS = 8
QK_NOPE = 64
QK_ROPE = 32
V_HEAD = 64
Q_LORA = 384
KV_LORA = 256
MLA_WIDTH = MLA_HEADS * V_HEAD
X_HEADS = 4
X_HEAD_DIM = 128
X_WIDTH = X_HEADS * X_HEAD_DIM
ROPE_THETA = 10000.0
EPS = 1e-6
IN_SPLITS = (CONV_WIDTH, CONV_WIDTH, CONV_WIDTH, Q_LORA, KV_LORA, QK_ROPE, MLA_WIDTH,
             X_WIDTH, X_WIDTH, 3 * D_MODEL)

LANES = 128
HEAD_PAD = LANES
BF16_ROWS = 16
V_AUG = V_HEAD + BF16_ROWS
ROPE_LO = QK_NOPE
ROPE_HALF = QK_ROPE // 2
ROPE_HI = QK_NOPE + QK_ROPE
MLA_SCALE = (QK_NOPE + QK_ROPE) ** -0.5
X_SCALE = X_HEAD_DIM ** -0.5
LOG2E = float(np.log2(np.e))
NEG = float(np.finfo(np.float32).min)

TS_QKV = 1024
TQ = 512
TK = 512
TS_BLK = 512
HALO = 32
CONV_ROWS = 64
CONV_SPLIT = 2
OUT_SPLIT = 2
SUBLANES = 8
SHIFT_ROWS = TS_BLK + HALO - SUBLANES
VMEM_LIMIT = 56 * 1024 * 1024


def _rms(x, g):
    return x * lax.rsqrt(jnp.mean(x * x, axis=-1, keepdims=True) + EPS) * g


def _twice_sigmoid_of_double(h):
    return jnp.tanh(h) + 1.0


def _silu_of_double(h):
    return h * _twice_sigmoid_of_double(h)


def _dot(a, b):
    return jnp.dot(a, b, preferred_element_type=F32)


def _dot_nt(a, b):
    return lax.dot_general(a, b, (((1,), (1,)), ((), ())), preferred_element_type=F32)


def _qkv_kernel(x_ref, pos_ref, ng_ref, wa_ref, qg_ref, wuq_ref, kvg_ref, wuk_ref, wuv_ref,
                invf_ref, mem_ref, mg_ref, wmem_ref, qt_ref, k_ref, vt_ref, hb_ref, kx_ref, vx_ref):
    @pl.when(pl.program_id(1) == 0)
    def _():
        mkv = _dot(_rms(mem_ref[...], mg_ref[...]).astype(BF16), wmem_ref[...])
        kx_ref[...] = mkv[:, :X_WIDTH].astype(BF16)
        vx_ref[...] = mkv[:, X_WIDTH:].astype(BF16)

    hb = _rms(x_ref[...], ng_ref[...]).astype(BF16)
    hb_ref[...] = hb
    za = _dot_nt(hb, wa_ref[...])
    cq = _rms(za[:, :Q_LORA], qg_ref[...]).astype(BF16)
    ckv = _rms(za[:, Q_LORA:Q_LORA + KV_LORA], kvg_ref[...]).astype(BF16)
    kr = za[:, Q_LORA + KV_LORA:]

    ang = invf_ref[...] * pos_ref[...].astype(F32)
    cos, sin = jnp.cos(ang), jnp.sin(ang)

    def rope_t(t):
        x1 = t[ROPE_LO:ROPE_LO + ROPE_HALF]
        x2 = t[ROPE_LO + ROPE_HALF:ROPE_HI]
        return jnp.concatenate(
            [t[:ROPE_LO], x1 * cos - x2 * sin, x1 * sin + x2 * cos, t[ROPE_HI:]], axis=0)

    kr_rot = rope_t(kr.T).T
    q = _dot(cq, wuq_ref[...])
    kn = _dot(ckv, wuk_ref[...])
    for h in range(MLA_HEADS):
        sl = slice(HEAD_PAD * h, HEAD_PAD * (h + 1))
        qt_ref[h] = (rope_t(q[:, sl].T) * (MLA_SCALE * LOG2E)).astype(BF16)
        k_ref[:, sl] = (kn[:, sl] + kr_rot).astype(BF16)
    vt = _dot(ckv, wuv_ref[...]).T.astype(BF16)
    ones_row = (lax.broadcasted_iota(jnp.int32, (V_AUG - V_HEAD, vt.shape[1]), 0) == 0).astype(BF16)
    for h in range(MLA_HEADS):
        vt_ref[h, 0:V_HEAD, :] = vt[V_HEAD * h:V_HEAD * (h + 1), :]
        vt_ref[h, V_HEAD:V_AUG, :] = ones_row


def _qkv(x, pos3, ng, wa, qg, wuq, kvg, wuk, wuv, invf, mem, mg, wmem):
    b, s, d = x.shape
    m = mem.shape[1]
    ts = TS_QKV
    const = lambda shape: pl.BlockSpec(shape, lambda i, j: (0,) * len(shape))
    hp = MLA_HEADS * HEAD_PAD
    return pl.pallas_call(
        _qkv_kernel,
        grid=(b, s // ts),
        in_specs=[pl.BlockSpec((None, ts, d), lambda i, j: (i, j, 0)),
                  pl.BlockSpec((None, 1, ts), lambda i, j: (i, 0, j)),
                  const((1, d)), const(wa.shape), const((1, Q_LORA)), const(wuq.shape),
                  const((1, KV_LORA)), const(wuk.shape), const(wuv.shape), const((ROPE_HALF, 1)),
                  pl.BlockSpec((None, m, d), lambda i, j: (i, 0, 0)),
                  const((1, d)), const(wmem.shape)],
        out_specs=[pl.BlockSpec((None, MLA_HEADS, HEAD_PAD, ts), lambda i, j: (i, 0, 0, j)),
                   pl.BlockSpec((None, ts, hp), lambda i, j: (i, j, 0)),
                   pl.BlockSpec((None, MLA_HEADS, V_AUG, ts), lambda i, j: (i, 0, 0, j)),
                   pl.BlockSpec((None, ts, d), lambda i, j: (i, j, 0)),
                   pl.BlockSpec((None, m, X_WIDTH), lambda i, j: (i, 0, 0)),
                   pl.BlockSpec((None, m, X_WIDTH), lambda i, j: (i, 0, 0))],
        out_shape=[jax.ShapeDtypeStruct((b, MLA_HEADS, HEAD_PAD, s), BF16),
                   jax.ShapeDtypeStruct((b, s, hp), BF16),
                   jax.ShapeDtypeStruct((b, MLA_HEADS, V_AUG, s), BF16),
                   jax.ShapeDtypeStruct((b, s, d), BF16),
                   jax.ShapeDtypeStruct((b, m, X_WIDTH), BF16),
                   jax.ShapeDtypeStruct((b, m, X_WIDTH), BF16)],
        compiler_params=pltpu.CompilerParams(dimension_semantics=("arbitrary", "arbitrary"),
                                             vmem_limit_bytes=VMEM_LIMIT),
        name="qkv",
    )(x, pos3, ng, wa, qg, wuq, kvg, wuk, wuv, invf, mem, mg, wmem)


ATTN_SLOTS = 4
ATTN_UNROLL = 4
ATTN_PEEL = 2
ATTN_STRIP = 256
MAX_CHAINS = 2


def _attn_schedule(n_tiles):
    below = [(qi, j) for qi in range(n_tiles) for j in range(qi)]
    diag = [(qi, qi) for qi in range(n_tiles)]
    assert len(below) % ATTN_UNROLL == 0 and len(diag) % ATTN_UNROLL == 0
    table = np.array(below + diag + [(0, 0)], np.int32).T
    return table, len(below), len(diag)


def _attn_kernel(n_below, n_diag, tbl_ref, qt_ref, k_ref, vt_ref, o_ref,
                 s_scr, mblk_scr, m_scr, acc_scr):
    heads = (0, 1)
    units = [(hh, c) for c in range(TQ // ATTN_STRIP) for hh in heads]

    def key_rows(c, diag):
        return min(TK, (c + 1) * ATTN_STRIP) if diag else TK

    def scores(f, slot, hh, c, diag):
        q0 = pl.multiple_of(tbl_ref[0, f] * TQ + c * ATTN_STRIP, ATTN_STRIP)
        k0 = pl.multiple_of(tbl_ref[1, f] * TK, TK)
        nk = key_rows(c, diag)
        k = k_ref[pl.ds(k0, nk), HEAD_PAD * hh:HEAD_PAD * (hh + 1)]
        s = _dot(k, qt_ref[hh, :, pl.ds(q0, ATTN_STRIP)])
        if diag:
            key = lax.broadcasted_iota(jnp.int32, (nk, ATTN_STRIP), 0)
            qry = lax.broadcasted_iota(jnp.int32, (nk, ATTN_STRIP), 1) + c * ATTN_STRIP
            s = jnp.where(key <= qry, s, NEG)
        s_scr[slot, hh, c, 0:nk, :] = s
        tiles = s.reshape(MAX_CHAINS, nk // (MAX_CHAINS * SUBLANES), SUBLANES, ATTN_STRIP)
        mblk_scr[slot, hh, c] = jnp.max(jnp.max(jnp.max(tiles, axis=1), axis=0),
                                        axis=0, keepdims=True)

    def softmax(f, slot, hh, c, diag):
        qi = tbl_ref[0, f]
        nk = key_rows(c, diag)
        m_old = m_scr[qi, hh, c]
        m_new = jnp.maximum(m_old, mblk_scr[slot, hh, c])
        m_scr[qi, hh, c] = m_new
        alpha = jnp.exp2(m_old - m_new)
        p = jnp.exp2((s_scr[slot, hh, c, 0:nk, :] - m_new).astype(BF16))
        return alpha, p

    def values(f, hh, c, diag, alpha, p):
        qi = tbl_ref[0, f]
        k0 = pl.multiple_of(tbl_ref[1, f] * TK, TK)
        vt = vt_ref[hh, :, pl.ds(k0, key_rows(c, diag))]
        acc_scr[qi, hh, c] = alpha * acc_scr[qi, hh, c] + _dot(vt, p)

    def trip(f, first, unroll, diag, last_next_diag):
        for u in range(unroll):
            slot, nxt = (first + u) % ATTN_SLOTS, (first + u + 1) % ATTN_SLOTS
            next_diag = last_next_diag if u == unroll - 1 else diag
            for hh, c in units:
                alpha, p = softmax(f + u, slot, hh, c, diag)
                scores(f + u + 1, nxt, hh, c, next_diag)
                values(f + u, hh, c, diag, alpha, p)

    def trips(first, count, unroll, diag, then_diag):
        assert count % unroll == 0 and (count == unroll or unroll % ATTN_SLOTS == 0)
        n_trips = count // unroll
        looped = max(0, n_trips - ATTN_PEEL)
        if looped > 0:
            pl.loop(0, looped)(lambda t: trip(first + t * unroll, first, unroll, diag, diag))
        for t in range(looped, n_trips):
            trip(first + t * unroll, first, unroll, diag, then_diag if t == n_trips - 1 else diag)

    m_scr[...] = jnp.full(m_scr.shape, NEG, F32)
    acc_scr[...] = jnp.zeros(acc_scr.shape, F32)
    for hh, c in units:
        scores(0, 0, hh, c, n_below == 0)
    trips(0, n_below, ATTN_UNROLL, diag=False, then_diag=True)
    trips(n_below, n_diag, ATTN_UNROLL, diag=True, then_diag=True)

    @pl.loop(0, n_diag)
    def _(qi):
        ot = jnp.concatenate(
            [jnp.concatenate([acc_scr[qi, hh, c, 0:V_HEAD] / acc_scr[qi, hh, c, V_HEAD:V_HEAD + 1]
                              for c in range(TQ // ATTN_STRIP)], axis=1) for hh in heads],
            axis=0)
        o_ref[pl.ds(pl.multiple_of(qi * TQ, TQ), TQ), :] = ot.T.astype(BF16)


def _mla_attn(qt, kp, vt):
    b, s, _ = kp.shape
    pairs = MLA_HEADS // 2
    n_tiles = s // TQ
    n_strips = TQ // ATTN_STRIP
    table, n_below, n_diag = _attn_schedule(n_tiles)
    grid_spec = pltpu.PrefetchScalarGridSpec(
        num_scalar_prefetch=1,
        grid=(b, pairs),
        in_specs=[pl.BlockSpec((None, 2, HEAD_PAD, s), lambda i, p, tbl: (i, p, 0, 0)),
                  pl.BlockSpec((None, s, 2 * HEAD_PAD), lambda i, p, tbl: (i, 0, p)),
                  pl.BlockSpec((None, 2, V_AUG, s), lambda i, p, tbl: (i, p, 0, 0))],
        out_specs=pl.BlockSpec((None, s, LANES), lambda i, p, tbl: (i, 0, p)),
        scratch_shapes=[pltpu.VMEM((ATTN_SLOTS, 2, n_strips, TK, ATTN_STRIP), F32),
                        pltpu.VMEM((ATTN_SLOTS, 2, n_strips, 1, ATTN_STRIP), F32),
                        pltpu.VMEM((n_tiles, 2, n_strips, 1, ATTN_STRIP), F32),
                        pltpu.VMEM((n_tiles, 2, n_strips, V_AUG, ATTN_STRIP), F32)])
    return pl.pallas_call(
        functools.partial(_attn_kernel, n_below, n_diag),
        grid_spec=grid_spec,
        out_shape=jax.ShapeDtypeStruct((b, s, MLA_WIDTH), BF16),
        compiler_params=pltpu.CompilerParams(dimension_semantics=("arbitrary", "arbitrary"),
                                             vmem_limit_bytes=VMEM_LIMIT),
        name="mla_attn",
    )(jnp.asarray(table), qt, kp, vt)


_EDGES = tuple(int(e) for e in np.cumsum((0,) + IN_SPLITS))
assert all(e % BF16_ROWS == 0 for e in _EDGES)
(_C_VAL, _C_GLU, _C_GATE, _Q_DOWN, _KV_DOWN, _K_ROPE, _M_GATE, _X_Q, _X_GATE, _G0, D_IN) = _EDGES


def _block_kernel(x_ref, hb_ref, o_mla_ref, kx_ref, vx_ref, wt_ref, bg_ref, cw_ref, cb_ref,
                  lng_ref, lnb_ref, wco_ref, wmo_ref, wxo_ref, wout_ref, fg_ref,
                  out_ref, ubuf, ush, conv_scr, part_scr, cgate_scr, g0_scr):
    ts = TS_BLK

    def proj(lo, width):
        return _dot_nt(hb_ref[...], wt_ref[lo:lo + width, :])

    def gate(i, lo, width):
        c0 = i * D_MODEL + lo
        return _twice_sigmoid_of_double(proj(_G0 + c0, width) + bg_ref[:, c0:c0 + width])

    @pl.when(pl.program_id(1) == 0)
    def _():
        ubuf[0:HALO, :] = jnp.zeros((HALO, CONV_WIDTH), F32)

    cw_ = CONV_WIDTH // CONV_SPLIT
    first = HALO - (CONV_KERNEL - 1)

    def glu(g):
        cs = slice(g * cw_, (g + 1) * cw_)
        ubuf[HALO:HALO + ts, cs] = (proj(_C_VAL + g * cw_, cw_)
                                    * _twice_sigmoid_of_double(proj(_C_GLU + g * cw_, cw_)))
        groups = SHIFT_ROWS // SUBLANES
        this_tile = ubuf[0:SHIFT_ROWS, cs].reshape(groups, SUBLANES, cw_)
        next_tile = ubuf[SUBLANES:SUBLANES + SHIFT_ROWS, cs].reshape(groups, SUBLANES, cw_)
        sub = lax.broadcasted_iota(jnp.int32, (1, SUBLANES, cw_), 1)
        for r in range(1, SUBLANES):
            picked = jnp.where(sub >= r, this_tile, next_tile)
            ush[r - 1, :, cs] = pltpu.roll(picked, SUBLANES - r, 1).reshape(SHIFT_ROWS, cw_)

    def conv_chunk(g, i):
        cs = slice(g * cw_, (g + 1) * cw_)
        r0 = i * CONV_ROWS
        groups = CONV_ROWS // SUBLANES
        acc = jnp.zeros((groups, SUBLANES, cw_), F32) + cb_ref[:, cs]
        for kk in range(CONV_KERNEL):
            r = (first + kk) % SUBLANES
            row = r0 + first + kk - r
            tap = ubuf[row:row + CONV_ROWS, cs] if r == 0 else ush[r - 1, row:row + CONV_ROWS, cs]
            w = cw_ref[kk * SUBLANES:(kk + 1) * SUBLANES, cs]
            acc = acc + tap.reshape(groups, SUBLANES, cw_) * w
        conv_scr[r0:r0 + CONV_ROWS, cs] = acc.reshape(CONV_ROWS, cw_)

    st = {}
    half = D_MODEL // 2

    def mla_gate():
        gate_m = _silu_of_double(proj(_M_GATE, MLA_WIDTH))
        st["a_mla"] = (o_mla_ref[...].astype(F32) * gate_m).astype(BF16)

    def mla_out():
        st["y_mla"] = _dot(st["a_mla"], wmo_ref[...])

    def mla_merge(lo):
        part_scr[:, lo:lo + half] = gate(1, lo, half) * st["y_mla"][:, lo:lo + half]

    def x_query():
        st["xq"] = proj(_X_Q, X_WIDTH).astype(BF16)
        st["ox"] = []

    def x_heads(h0):
        for h in (h0, h0 + 1):
            hs = slice(X_HEAD_DIM * h, X_HEAD_DIM * (h + 1))
            s = _dot_nt(st["xq"][:, hs], kx_ref[:, hs])
            p = jnp.exp2((s - jnp.max(s, axis=-1, keepdims=True)) * (X_SCALE * LOG2E))
            inv = 1.0 / jnp.sum(p, axis=-1, keepdims=True)
            st["ox"].append(_dot(p.astype(BF16), vx_ref[:, hs]) * inv)

    def x_gate():
        st["xg"] = _silu_of_double(proj(_X_GATE, X_WIDTH))

    def x_out():
        a_x = jnp.concatenate(st["ox"], axis=-1) * st["xg"]
        st["y_x"] = _dot(a_x.astype(BF16), wxo_ref[...])

    def x_merge(lo):
        part_scr[:, lo:lo + half] += gate(2, lo, half) * st["y_x"][:, lo:lo + half]

    def conv_gate():
        cgate_scr[...] = _silu_of_double(proj(_C_GATE, CONV_WIDTH))

    def conv_merge_gate(lo):
        g0_scr[:, lo:lo + half] = gate(0, lo, half)

    side = [mla_gate, mla_out, functools.partial(mla_merge, 0), functools.partial(mla_merge, half),
            x_query, functools.partial(x_heads, 0), functools.partial(x_heads, 2), x_gate, x_out,
            functools.partial(x_merge, 0), functools.partial(x_merge, half), conv_gate,
            functools.partial(conv_merge_gate, 0), functools.partial(conv_merge_gate, half)]
    side = [functools.partial(glu, g) for g in range(1, CONV_SPLIT)] + side
    chunks = [(g, i) for g in range(CONV_SPLIT) for i in range(ts // CONV_ROWS)]
    glu(0)
    for n, (g, i) in enumerate(chunks):
        conv_chunk(g, i)
        for j, piece in enumerate(side):
            if j * len(chunks) // len(side) == n:
                piece()
    ubuf[0:HALO, :] = ubuf[ts:ts + HALO, :]

    conv = conv_scr[...]
    mu = jnp.mean(conv, axis=-1, keepdims=True)
    cen = conv - mu
    var = jnp.mean(cen * cen, axis=-1, keepdims=True)
    half_ln = cen * lax.rsqrt(var + EPS) * lng_ref[...] + lnb_ref[...]
    a_conv = _silu_of_double(half_ln) * cgate_scr[...]
    y_conv = _dot(a_conv.astype(BF16), wco_ref[...])
    for r0 in range(0, ts, ts // OUT_SPLIT):
        rows = slice(r0, r0 + ts // OUT_SPLIT)
        merged = g0_scr[rows, :] * y_conv[rows, :] + part_scr[rows, :]
        y = x_ref[rows, :] + _dot(merged.astype(BF16), wout_ref[...])
        out_ref[rows, :] = _rms(y, fg_ref[...])


def _block(x, hb, o_mla, kx, vx, wt, bg, cw, cb, lng, lnb, wco, wmo, wxo, wout, fg):
    b, s, d = x.shape
    ts = TS_BLK
    const = lambda shape: pl.BlockSpec(shape, lambda i, j: (0,) * len(shape),
                                       pipeline_mode=pl.Buffered(1))
    return pl.pallas_call(
        _block_kernel,
        grid=(b, s // ts),
        in_specs=[pl.BlockSpec((None, ts, d), lambda i, j: (i, j, 0)),
                  pl.BlockSpec((None, ts, d), lambda i, j: (i, j, 0)),
                  pl.BlockSpec((None, ts, MLA_WIDTH), lambda i, j: (i, j, 0)),
                  pl.BlockSpec((None, N_MEM, X_WIDTH), lambda i, j: (i, 0, 0)),
                  pl.BlockSpec((None, N_MEM, X_WIDTH), lambda i, j: (i, 0, 0)),
                  const(wt.shape), const(bg.shape),
                  const(cw.shape),
                  const((1, CONV_WIDTH)), const((1, CONV_WIDTH)), const((1, CONV_WIDTH)),
                  const(wco.shape), const(wmo.shape), const(wxo.shape), const(wout.shape),
                  const((1, d))],
        out_specs=pl.BlockSpec((None, ts, d), lambda i, j: (i, j, 0)),
        out_shape=jax.ShapeDtypeStruct((b, s, d), F32),
        scratch_shapes=[pltpu.VMEM((HALO + ts, CONV_WIDTH), F32),
                        pltpu.VMEM((SUBLANES - 1, SHIFT_ROWS, CONV_WIDTH), F32),
                        pltpu.VMEM((ts, CONV_WIDTH), F32),
                        pltpu.VMEM((ts, d), F32),
                        pltpu.VMEM((ts, CONV_WIDTH), F32),
                        pltpu.VMEM((ts, d), F32)],
        compiler_params=pltpu.CompilerParams(dimension_semantics=("arbitrary", "arbitrary"),
                                             vmem_limit_bytes=VMEM_LIMIT),
        name="block",
    )(x, hb, o_mla, kx, vx, wt, bg, cw, cb, lng, lnb, wco, wmo, wxo, wout, fg)


def _inv_freq_column():
    inv = ROPE_THETA ** (-jnp.arange(0, QK_ROPE, 2, dtype=F32) / QK_ROPE)
    return inv.reshape(ROPE_HALF, 1)


def kernel(x, mem, positions, norm_g, w_in, b_gate, conv_w, conv_b, conv_ln_g, conv_ln_b,
           w_conv_o, q_norm_g, w_uq, kv_norm_g, w_ukv, w_mla_o, mem_norm_g, w_mem_kv, w_x_o,
           w_out, final_norm_g):
    assert norm_g.shape[0] == 1, "single-layer trunk"
    b, s, d = x.shape
    row = lambda v: v.reshape(1, -1)
    halved = np.ones((D_IN, 1), np.float32)
    for lo, hi in ((_C_VAL, _Q_DOWN), (_M_GATE, _X_Q), (_X_GATE, D_IN)):
        halved[lo:hi] = 0.5
    wt = (w_in[0].T * halved).astype(BF16)
    wa = jnp.concatenate([wt[_Q_DOWN:_K_ROPE],
                          jnp.pad(wt[_K_ROPE:_M_GATE], ((ROPE_LO, LANES - ROPE_HI), (0, 0)))], axis=0)
    wuq = jnp.pad(w_uq[0].reshape(Q_LORA, MLA_HEADS, QK_NOPE + QK_ROPE),
                  ((0, 0), (0, 0), (0, HEAD_PAD - QK_NOPE - QK_ROPE)))
    wuq = wuq.reshape(Q_LORA, MLA_HEADS * HEAD_PAD).astype(BF16)
    wukv = w_ukv[0].reshape(KV_LORA, MLA_HEADS, QK_NOPE + V_HEAD)
    wuk = jnp.pad(wukv[:, :, :QK_NOPE], ((0, 0), (0, 0), (0, HEAD_PAD - QK_NOPE)))
    wuk = wuk.reshape(KV_LORA, MLA_HEADS * HEAD_PAD).astype(BF16)
    wuv = wukv[:, :, QK_NOPE:].reshape(KV_LORA, MLA_WIDTH).astype(BF16)

    qt, kp, vt, hb, kx, vx = _qkv(x, positions.reshape(b, 1, s), row(norm_g[0]), wa,
                                  row(q_norm_g[0]), wuq, row(kv_norm_g[0]), wuk, wuv,
                                  _inv_freq_column(), mem, row(mem_norm_g[0]),
                                  w_mem_kv[0].astype(BF16))
    o_mla = _mla_attn(qt, kp, vt)
    half_bf = lambda w: (0.5 * w).astype(BF16)
    return _block(x, hb, o_mla, kx, vx, wt, row(0.5 * b_gate[0]),
                  jnp.repeat(conv_w[0], SUBLANES, axis=0),
                  row(conv_b[0]), row(0.5 * conv_ln_g[0]), row(0.5 * conv_ln_b[0]),
                  half_bf(w_conv_o[0]), half_bf(w_mla_o[0]), half_bf(w_x_o[0]),
                  w_out[0].astype(BF16), row(final_norm_g))
```
